```python
import functools
import jax, jax.numpy as jnp
from jax import lax
import numpy as np

D_MODEL = 1024
BATCH = 8
SEQ = 8192
DEPTH = 4
DEC_BATCH = 32
DEC_SEQ = 16
PAST_LEN = 1024

CHUNK = 64
N_LAYERS_A = DEPTH // 2
N_LAYERS_B = DEPTH - N_LAYERS_A
HEAD_DIM = 64
N_HEADS = D_MODEL // HEAD_DIM
N_KV_HEADS = 4
GROUP = N_HEADS // N_KV_HEADS
WINDOW_A = 128
LEFT_CHUNKS_A = WINDOW_A // CHUNK
LEFT_CHUNKS_B = 8
WINDOW_B = LEFT_CHUNKS_B * CHUNK
MAX_REL = 128
N_REL = 2 * MAX_REL + 1
D_FF = 4 * D_MODEL
EPS = 1e-6
NEG = -1e30

kernel_name = "yoco_chunk_stream_encoder_step"


def rmsnorm(x, g):
    xf = x.astype(jnp.float32)
    y = xf * lax.rsqrt(jnp.mean(xf * xf, axis=-1, keepdims=True) + EPS)
    return (y * g.astype(jnp.float32)).astype(x.dtype)


def band_mask(q_pos, k_pos, left_chunks):
    qc = q_pos[:, None] // CHUNK
    kc = k_pos[None, :] // CHUNK
    return (k_pos[None, :] >= 0) & (kc <= qc) & (kc >= qc - left_chunks)


def alibi_bias(q_pos, k_pos, slopes):
    dist = jnp.abs(q_pos[:, None] - k_pos[None, :]).astype(jnp.float32)
    bias = -slopes[:, None, None] * dist[None]
    return jnp.where(band_mask(q_pos, k_pos, LEFT_CHUNKS_A)[None], bias, NEG)


def relpos_bias(q_pos, k_pos, table):
    rel = jnp.clip(q_pos[:, None] - k_pos[None, :], -MAX_REL, MAX_REL) + MAX_REL
    bias = table.astype(jnp.float32)[:, rel]
    return jnp.where(band_mask(q_pos, k_pos, LEFT_CHUNKS_B)[None], bias, NEG)


def attend(q, k, v, bias, sink):
    b, tq = q.shape[:2]
    qg = q.reshape(b, tq, N_KV_HEADS, GROUP, HEAD_DIM)
    s = jnp.einsum('bqhgd,bkhd->bhgqk', qg, k).astype(jnp.float32) * (HEAD_DIM ** -0.5)
    s = s + bias.reshape(N_KV_HEADS, GROUP, tq, -1)
    if sink is None:
        p = jax.nn.softmax(s, axis=-1)
    else:
        sk = sink.astype(jnp.float32).reshape(N_KV_HEADS, GROUP, 1, 1)
        m = jnp.maximum(jnp.max(s, axis=-1, keepdims=True), sk)
        e = jnp.exp(s - m)
        p = e / (jnp.sum(e, axis=-1, keepdims=True) + jnp.exp(sk - m))
    o = jnp.einsum('bhgqk,bkhd->bqhgd', p.astype(v.dtype), v)
    return o.reshape(b, tq, N_HEADS * HEAD_DIM)


def band_attention_prompt(q, k, v, left_chunks, bias_fn, sink):
    b, s = q.shape[:2]
    n_chunks = s // CHUNK
    pad = left_chunks * CHUNK
    band = pad + CHUNK
    k_pad = jnp.pad(k, ((0, 0), (pad, 0), (0, 0), (0, 0)))
    v_pad = jnp.pad(v, ((0, 0), (pad, 0), (0, 0), (0, 0)))
    q_chunks = q.reshape(b, n_chunks, CHUNK, N_HEADS, HEAD_DIM).swapaxes(0, 1)

    def one_chunk(args):
        c, q_c = args
        start = c * CHUNK
        k_c = lax.dynamic_slice_in_dim(k_pad, start, band, axis=1)
        v_c = lax.dynamic_slice_in_dim(v_pad, start, band, axis=1)
        q_pos = start + jnp.arange(CHUNK)
        k_pos = start - pad + jnp.arange(band)
        return attend(q_c, k_c, v_c, bias_fn(q_pos, k_pos), sink)

    out = lax.map(one_chunk, (jnp.arange(n_chunks), q_chunks))
    return out.swapaxes(0, 1).reshape(b, s, N_HEADS * HEAD_DIM)


def project_a(x, g_norm, w_qkv, g_q, g_k):
    b, t = x.shape[:2]
    qkv = rmsnorm(x, g_norm) @ w_qkv
    q, k, v = jnp.split(qkv, [N_HEADS * HEAD_DIM, (N_HEADS + N_KV_HEADS) * HEAD_DIM], axis=-1)
    q = rmsnorm(q.reshape(b, t, N_HEADS, HEAD_DIM), g_q)
    k = rmsnorm(k.reshape(b, t, N_KV_HEADS, HEAD_DIM), g_k)
    return q, k, v.reshape(b, t, N_KV_HEADS, HEAD_DIM)


def project_q_b(x, g_norm, w_q, g_q):
    b, t = x.shape[:2]
    return rmsnorm((rmsnorm(x, g_norm) @ w_q).reshape(b, t, N_HEADS, HEAD_DIM), g_q)


def shared_kv(x, g_norm, w_kv, g_k):
    b, t = x.shape[:2]
    kv = (rmsnorm(x, g_norm) @ w_kv).reshape(b, t, 2, N_KV_HEADS, HEAD_DIM)
    return rmsnorm(kv[:, :, 0], g_k), kv[:, :, 1]


def mlp(x, g_norm, w_up, w_down):
    return jnp.square(jax.nn.relu(rmsnorm(x, g_norm) @ w_up)) @ w_down


def setup_inputs(seed: int = 0) -> dict:
    key = jax.random.key(seed)
    ks = jax.random.split(key, 24)
    nrm = lambda k, shape, scale: jax.random.normal(k, shape, jnp.float32) * scale
    len_a = min(WINDOW_A, PAST_LEN)
    len_b = min(WINDOW_B, PAST_LEN)
    qkv_w = (N_HEADS + 2 * N_KV_HEADS) * HEAD_DIM
    return {
        "x_prompt": nrm(ks[0], (BATCH, SEQ, D_MODEL), 1.0),
        "x_sample": nrm(ks[1], (DEC_BATCH, DEC_SEQ, D_MODEL), 1.0),
        "cache_k_a": nrm(ks[2], (N_LAYERS_A, DEC_BATCH, len_a, N_KV_HEADS, HEAD_DIM), 1.0),
        "cache_v_a": nrm(ks[3], (N_LAYERS_A, DEC_BATCH, len_a, N_KV_HEADS, HEAD_DIM), 1.0),
        "cache_k_b": nrm(ks[4], (DEC_BATCH, len_b, N_KV_HEADS, HEAD_DIM), 1.0),
        "cache_v_b": nrm(ks[5], (DEC_BATCH, len_b, N_KV_HEADS, HEAD_DIM), 1.0),
        "g_attn": 1.0 + nrm(ks[6], (DEPTH, D_MODEL), 0.05),
        "g_mlp": 1.0 + nrm(ks[7], (DEPTH, D_MODEL), 0.05),
        "w_qkv_a": nrm(ks[8], (N_LAYERS_A, D_MODEL, qkv_w), D_MODEL ** -0.5),
        "g_q_a": 1.0 + nrm(ks[9], (N_LAYERS_A, HEAD_DIM), 0.05),
        "g_k_a": 1.0 + nrm(ks[10], (N_LAYERS_A, HEAD_DIM), 0.05),
        "sink_a": nrm(ks[11], (N_LAYERS_A, N_HEADS), 0.5),
        "w_o_a": nrm(ks[12], (N_LAYERS_A, N_HEADS * HEAD_DIM, D_MODEL), (N_HEADS * HEAD_DIM) ** -0.5),
        "g_kv": 1.0 + nrm(ks[13], (D_MODEL,), 0.05),
        "w_kv": nrm(ks[14], (D_MODEL, 2 * N_KV_HEADS * HEAD_DIM), D_MODEL ** -0.5),
        "g_k_b": 1.0 + nrm(ks[15], (HEAD_DIM,), 0.05),
        "w_q_b": nrm(ks[16], (N_LAYERS_B, D_MODEL, N_HEADS * HEAD_DIM), D_MODEL ** -0.5),
        "g_q_b": 1.0 + nrm(ks[17], (N_LAYERS_B, HEAD_DIM), 0.05),
        "rel_bias_b": nrm(ks[18], (N_LAYERS_B, N_HEADS, N_REL), 0.5),
        "w_o_b": nrm(ks[19], (N_LAYERS_B, N_HEADS * HEAD_DIM, D_MODEL), (N_HEADS * HEAD_DIM) ** -0.5),
        "w_up": nrm(ks[20], (DEPTH, D_MODEL, D_FF), D_MODEL ** -0.5),
        "w_down": nrm(ks[21], (DEPTH, D_FF, D_MODEL), D_FF ** -0.5),
    }


def reference(x_prompt, x_sample, cache_k_a, cache_v_a, cache_k_b, cache_v_b,
              g_attn, g_mlp, w_qkv_a, g_q_a, g_k_a, sink_a, w_o_a,
              g_kv, w_kv, g_k_b, w_q_b, g_q_b, rel_bias_b, w_o_b, w_up, w_down):
    slopes = 2.0 ** (-8.0 * jnp.arange(1, N_HEADS + 1, dtype=jnp.float32) / N_HEADS)
    hp, hs = x_prompt, x_sample
    seq = x_prompt.shape[1]
    t_new = x_sample.shape[1]
    pos_s = PAST_LEN + jnp.arange(t_new)
    len_a = cache_k_a.shape[2]
    len_b = cache_k_b.shape[1]
    kpos_a = jnp.concatenate([PAST_LEN - len_a + jnp.arange(len_a), pos_s])
    kpos_b = jnp.concatenate([PAST_LEN - len_b + jnp.arange(len_b), pos_s])
    keep_a = min(WINDOW_A, seq)
    keep_b = min(WINDOW_B, seq)
    alibi_fn = functools.partial(alibi_bias, slopes=slopes)
    ka_p, va_p, ka_s, va_s = [], [], [], []

    for layer in range(DEPTH):
        if layer < N_LAYERS_A:
            i = layer
            qp, kp, vp = project_a(hp, g_attn[layer], w_qkv_a[i], g_q_a[i], g_k_a[i])
            ap = band_attention_prompt(qp, kp, vp, LEFT_CHUNKS_A, alibi_fn, sink_a[i])
            ka_p.append(kp[:, seq - keep_a:])
            va_p.append(vp[:, seq - keep_a:])
            qs, ks_, vs_ = project_a(hs, g_attn[layer], w_qkv_a[i], g_q_a[i], g_k_a[i])
            k_all = jnp.concatenate([cache_k_a[i], ks_], axis=1)
            v_all = jnp.concatenate([cache_v_a[i], vs_], axis=1)
            as_ = attend(qs, k_all, v_all, alibi_bias(pos_s, kpos_a, slopes), sink_a[i])
            ka_s.append(ks_)
            va_s.append(vs_)
            hp = hp + ap @ w_o_a[i]
            hs = hs + as_ @ w_o_a[i]
        else:
            j = layer - N_LAYERS_A
            if j == 0:
                kb_p, vb_p = shared_kv(hp, g_kv, w_kv, g_k_b)
                kb_s, vb_s = shared_kv(hs, g_kv, w_kv, g_k_b)
                kb_all = jnp.concatenate([cache_k_b, kb_s], axis=1)
                vb_all = jnp.concatenate([cache_v_b, vb_s], axis=1)
            rel_fn = functools.partial(relpos_bias, table=rel_bias_b[j])
            qp = project_q_b(hp, g_attn[layer], w_q_b[j], g_q_b[j])
            ap = band_attention_prompt(qp, kb_p, vb_p, LEFT_CHUNKS_B, rel_fn, None)
            qs = project_q_b(hs, g_attn[layer], w_q_b[j], g_q_b[j])
            as_ = attend(qs, kb_all, vb_all, relpos_bias(pos_s, kpos_b, rel_bias_b[j]), None)
            hp = hp + ap @ w_o_b[j]
            hs = hs + as_ @ w_o_b[j]
        hp = hp + mlp(hp, g_mlp[layer], w_up[layer], w_down[layer])
        hs = hs + mlp(hs, g_mlp[layer], w_up[layer], w_down[layer])

    new_k_a_prompt = jnp.stack(ka_p)
    new_v_a_prompt = jnp.stack(va_p)
    new_k_a_sample = jnp.stack(ka_s)
    new_v_a_sample = jnp.stack(va_s)
    new_k_b_prompt = kb_p[:, seq - keep_b:]
    new_v_b_prompt = vb_p[:, seq - keep_b:]
    return (hp, hs, new_k_a_prompt, new_v_a_prompt, new_k_b_prompt, new_v_b_prompt,
            new_k_a_sample, new_v_a_sample, kb_s, vb_s)
```

```python
import functools
import math

import numpy as np
import jax
import jax.numpy as jnp
from jax import lax
from jax.experimental import pallas as pl
from jax.experimental.pallas import tpu as pltpu

CHUNK = 64
HEAD_DIM = 64
N_KV_HEADS = 4
GROUP = 4
N_HEADS = N_KV_HEADS * GROUP
LEFT_CHUNKS_A = 2
LEFT_CHUNKS_B = 8
MAX_REL = 128
N_REL = 2 * MAX_REL + 1
EPS = 1e-6
NEG = -1e30
LOG2E = math.log2(math.e)

KV_WIDTH = N_KV_HEADS * HEAD_DIM
GROUP_WIDTH = GROUP * HEAD_DIM
ROW_TILE = 512
VMEM_LIMIT = 56 * 1024 * 1024

BF16 = jnp.bfloat16
F32 = jnp.float32


def _const_spec(shape):
    zeros = (0,) * len(shape)
    return pl.BlockSpec(shape, lambda *_: zeros, pipeline_mode=pl.Buffered(1))


def _head_sum_matrix():
    idx = np.arange(GROUP_WIDTH) // HEAD_DIM
    return jnp.asarray(idx[:, None] == idx[None, :], dtype=BF16)


def _replicate_matrices():
    src = np.arange(KV_WIDTH)[:, None]
    dst = np.arange(GROUP_WIDTH)[None, :]
    rep = [(src == h * HEAD_DIM + dst % HEAD_DIM) for h in range(N_KV_HEADS)]
    return jnp.asarray(np.stack(rep), dtype=BF16)


def _rms_rows(x):
    return x * lax.rsqrt(jnp.mean(x * x, axis=-1, keepdims=True) + EPS)


def _proj_kernel(x_ref, g_ref, w_ref, hsum_ref, gain_ref, *out_refs, segments):
    xn = (_rms_rows(x_ref[...]) * g_ref[...]).astype(BF16)
    y = jnp.dot(xn, w_ref[...], preferred_element_type=F32)
    for (start, width, normed), o_ref in zip(segments, out_refs):
        for t in range(width // GROUP_WIDTH):
            lo = start + t * GROUP_WIDTH
            yt = y[:, lo:lo + GROUP_WIDTH]
            if normed:
                ss = jnp.dot((yt * yt).astype(BF16), hsum_ref[...], preferred_element_type=F32)
                yt = yt * lax.rsqrt(ss * (1.0 / HEAD_DIM) + EPS) * gain_ref[:, lo:lo + GROUP_WIDTH]
            o_ref[:, t * GROUP_WIDTH:(t + 1) * GROUP_WIDTH] = yt.astype(o_ref.dtype)


def _project(x, g, w, gains, segments, out_dtypes):
    n, d = x.shape
    c = w.shape[1]
    tile = min(ROW_TILE, n)
    assert n % tile == 0
    out_shape = [jax.ShapeDtypeStruct((n, width), dt) for (_, width, _), dt in zip(segments, out_dtypes)]
    out_specs = [pl.BlockSpec((tile, width), lambda i: (i, 0)) for (_, width, _) in segments]
    return pl.pallas_call(
        functools.partial(_proj_kernel, segments=tuple(segments)),
        grid=(n // tile,),
        in_specs=[
            pl.BlockSpec((tile, d), lambda i: (i, 0)),
            _const_spec((1, d)),
            _const_spec((d, c)),
            _const_spec((GROUP_WIDTH, GROUP_WIDTH)),
            _const_spec((1, c)),
        ],
        out_specs=out_specs,
        out_shape=out_shape,
        compiler_params=pltpu.CompilerParams(
            dimension_semantics=("arbitrary",), vmem_limit_bytes=VMEM_LIMIT),
        name="project",
    )(x, g.reshape(1, d), w, _head_sum_matrix(), gains.reshape(1, c))


def _attn_kernel(*refs, tq, nq, prev, has_sink, dyn_mask):
    if has_sink:
        q_ref, kp_ref, kc_ref, vp_ref, vc_ref, bias_ref, rep_ref, sink_ref, o_ref, krep, vrep = refs
    else:
        q_ref, kp_ref, kc_ref, vp_ref, vc_ref, bias_ref, rep_ref, o_ref, krep, vrep = refs
        sink_ref = None
    tile = nq * tq
    tk = prev + tq

    def replicate(dst, src_ref, off, rows):
        xb = src_ref[...].astype(BF16)
        for h in range(N_KV_HEADS):
            dst[off:off + rows, h * GROUP_WIDTH:(h + 1) * GROUP_WIDTH] = jnp.dot(
                xb, rep_ref[h], preferred_element_type=F32).astype(BF16)

    replicate(krep, kp_ref, 0, prev)
    replicate(krep, kc_ref, prev, tile)
    replicate(vrep, vp_ref, 0, prev)
    replicate(vrep, vc_ref, prev, tile)

    lane_group = lax.broadcasted_iota(jnp.int32, (tq, GROUP_WIDTH), 1) // HEAD_DIM
    in_group = [lane_group == g for g in range(GROUP)]
    first_chunk = pl.program_id(1) * nq

    def one_chunk(c, carry):
        r0 = pl.multiple_of(c * tq, tq)
        qc = q_ref[pl.ds(r0, tq), :]
        if dyn_mask:
            col = lax.broadcasted_iota(jnp.int32, (1, tk), 1)
            col_bias = jnp.where(col >= prev - (first_chunk + c) * tq, 0.0, NEG)
        for h in range(N_KV_HEADS):
            qg = qc[:, h * GROUP_WIDTH:(h + 1) * GROUP_WIDTH]
            qs = jnp.concatenate(
                [jnp.where(in_group[g], qg, jnp.zeros_like(qg)) for g in range(GROUP)], axis=0)
            kr = krep[pl.ds(r0, tk), h * GROUP_WIDTH:(h + 1) * GROUP_WIDTH]
            s = lax.dot_general(qs, kr, (((1,), (1,)), ((), ())), preferred_element_type=F32)
            s = s + bias_ref[h]
            if dyn_mask:
                s = s + col_bias
            m = jnp.max(s, axis=-1, keepdims=True)
            if has_sink:
                m = jnp.maximum(m, sink_ref[h])
            e = jnp.exp2(s - m)
            denom = jnp.sum(e, axis=-1, keepdims=True)
            if has_sink:
                denom = denom + jnp.exp2(sink_ref[h] - m)
            p = (e * (1.0 / denom)).astype(BF16)
            vr = vrep[pl.ds(r0, tk), h * GROUP_WIDTH:(h + 1) * GROUP_WIDTH]
            o = jnp.zeros((tq, GROUP_WIDTH), F32)
            for g in range(GROUP):
                og = jnp.dot(p[g * tq:(g + 1) * tq], vr, preferred_element_type=F32)
                o = jnp.where(in_group[g], og, o)
            o_ref[pl.ds(r0, tq), h * GROUP_WIDTH:(h + 1) * GROUP_WIDTH] = o.astype(o_ref.dtype)
        return carry

    lax.fori_loop(0, nq, one_chunk, 0)


def _attention(q, k_prev, k_cur, v_prev, v_cur, bias, sink, *, tq, nq, prev, prev_map, dyn_mask):
    b, s, dq = q.shape
    tile = nq * tq
    tk = prev + tq
    assert s % tile == 0 and bias.shape == (N_KV_HEADS, GROUP * tq, tk)
    has_sink = sink is not None
    cur_map = lambda bi, i: (bi, i, 0)
    in_specs = [
        pl.BlockSpec((None, tile, dq), cur_map),
        pl.BlockSpec((None, prev, KV_WIDTH), prev_map),
        pl.BlockSpec((None, tile, KV_WIDTH), cur_map),
        pl.BlockSpec((None, prev, KV_WIDTH), prev_map),
        pl.BlockSpec((None, tile, KV_WIDTH), cur_map),
        _const_spec(bias.shape),
        _const_spec((N_KV_HEADS, KV_WIDTH, GROUP_WIDTH)),
    ]
    args = [q, k_prev, k_cur, v_prev, v_cur, bias, _replicate_matrices()]
    if has_sink:
        in_specs.append(_const_spec(sink.shape))
        args.append(sink)
    return pl.pallas_call(
        functools.partial(_attn_kernel, tq=tq, nq=nq, prev=prev, has_sink=has_sink, dyn_mask=dyn_mask),
        grid=(b, s // tile),
        in_specs=in_specs,
        out_specs=pl.BlockSpec((None, tile, dq), cur_map),
        out_shape=jax.ShapeDtypeStruct((b, s, dq), BF16),
        scratch_shapes=[
            pltpu.VMEM((prev + tile, N_KV_HEADS * GROUP_WIDTH), BF16),
            pltpu.VMEM((prev + tile, N_KV_HEADS * GROUP_WIDTH), BF16),
        ],
        compiler_params=pltpu.CompilerParams(
            dimension_semantics=("arbitrary", "arbitrary"), vmem_limit_bytes=VMEM_LIMIT),
        name="band_attention",
    )(*args)


def _prompt_attention(q, k, v, bias, sink, left_chunks):
    s = q.shape[1]
    prev = left_chunks * CHUNK
    tile = min(ROW_TILE, s)
    assert tile % prev == 0 or prev % tile == 0
    assert prev <= tile
    ratio = tile // prev
    prev_map = lambda bi, i: (bi, jnp.maximum(i * ratio - 1, 0), 0)
    return _attention(q, k, k, v, v, bias, sink, tq=CHUNK, nq=tile // CHUNK, prev=prev,
                      prev_map=prev_map, dyn_mask=True)


def _sample_attention(q, k_cache, k_new, v_cache, v_new, bias, sink):
    t = q.shape[1]
    prev_map = lambda bi, i: (bi, 0, 0)
    return _attention(q, k_cache, k_new, v_cache, v_new, bias, sink, tq=t, nq=1,
                      prev=k_cache.shape[1], prev_map=prev_map, dyn_mask=False)


def _out_mlp_kernel(x_ref, a_ref, wo_ref, g_ref, wup_ref, wdn_ref, o_ref, *, ff_chunks):
    h = x_ref[...] + jnp.dot(a_ref[...], wo_ref[...], preferred_element_type=F32)
    hn = (_rms_rows(h) * g_ref[...]).astype(BF16)
    d_ff = wup_ref.shape[1]
    step = d_ff // ff_chunks
    acc = h
    for c in range(ff_chunks):
        u = jnp.dot(hn, wup_ref[:, c * step:(c + 1) * step], preferred_element_type=F32)
        u = jnp.square(jnp.maximum(u, 0.0)).astype(BF16)
        acc = acc + jnp.dot(u, wdn_ref[c * step:(c + 1) * step, :], preferred_element_type=F32)
    o_ref[...] = acc


def _out_mlp(x, a, wo, g, wup, wdn):
    n, d = x.shape
    d_ff = wup.shape[1]
    tile = min(ROW_TILE, n)
    assert n % tile == 0
    row_spec = pl.BlockSpec((tile, d), lambda i: (i, 0))
    return pl.pallas_call(
        functools.partial(_out_mlp_kernel, ff_chunks=4),
        grid=(n // tile,),
        in_specs=[
            row_spec,
            row_spec,
            _const_spec((d, d)),
            _const_spec((1, d)),
            _const_spec((d, d_ff)),
            _const_spec((d_ff, d)),
        ],
        out_specs=row_spec,
        out_shape=jax.ShapeDtypeStruct((n, d), F32),
        compiler_params=pltpu.CompilerParams(
            dimension_semantics=("arbitrary",), vmem_limit_bytes=VMEM_LIMIT),
        name="out_proj_mlp",
    )(x, a, wo, g.reshape(1, d), wup, wdn)


def _rel_bias_kernel(tab_ref, o_ref, *, tk, delta):
    q = pl.program_id(1)
    n_pad = tab_ref.shape[-1]
    r = lax.broadcasted_iota(jnp.int32, (n_pad, tk), 0)
    j = lax.broadcasted_iota(jnp.int32, (n_pad, tk), 1)
    idx = jnp.clip(q + delta - j, -MAX_REL, MAX_REL) + MAX_REL
    onehot = jnp.where(r == idx, 1.0, 0.0).astype(BF16)
    acc = jnp.dot(tab_ref[0], onehot, preferred_element_type=F32)
    acc = acc + jnp.dot(tab_ref[1], onehot, preferred_element_type=F32)
    acc = acc + jnp.dot(tab_ref[2], onehot, preferred_element_type=F32)
    o_ref[...] = acc


def _rel_bias(table, tq, tk, delta):
    n_layers, h, n_rel = table.shape
    n_pad = -(-n_rel // 128) * 128
    t = jnp.pad(table, ((0, 0), (0, 0), (0, n_pad - n_rel)))
    hi = t.astype(BF16)
    mid = (t - hi.astype(F32)).astype(BF16)
    lo = (t - hi.astype(F32) - mid.astype(F32)).astype(BF16)
    parts = jnp.stack([hi, mid, lo], axis=1)
    return pl.pallas_call(
        functools.partial(_rel_bias_kernel, tk=tk, delta=delta),
        grid=(n_layers, tq),
        in_specs=[pl.BlockSpec((None, 3, h, n_pad), lambda l, q: (l, 0, 0, 0))],
        out_specs=pl.BlockSpec((None, None, h, tk), lambda l, q: (l, q, 0, 0)),
        out_shape=jax.ShapeDtypeStruct((n_layers, tq, h, tk), F32),
        compiler_params=pltpu.CompilerParams(dimension_semantics=("arbitrary", "arbitrary")),
        name="rel_bias_table",
    )(parts)


def _stack_heads(per_head, tq):
    return per_head.reshape(N_KV_HEADS, GROUP * tq, per_head.shape[-1])


def _np_band_mask(q_pos, k_pos, left_chunks):
    qc = q_pos[:, None] // CHUNK
    kc = k_pos[None, :] // CHUNK
    return (k_pos[None, :] >= 0) & (kc <= qc) & (kc >= qc - left_chunks)


def _alibi_bias(q_pos, k_pos, static_mask):
    slopes = (2.0 ** (-8.0 * np.arange(1, N_HEADS + 1, dtype=np.float32) / N_HEADS)).astype(np.float32)
    dist = np.abs(q_pos[:, None] - k_pos[None, :]).astype(np.float32)
    bias = -slopes[:, None, None] * dist[None] * np.float32(LOG2E)
    if static_mask:
        bias = np.where(_np_band_mask(q_pos, k_pos, LEFT_CHUNKS_A)[None], bias, np.float32(NEG))
    return jnp.asarray(_stack_heads(bias.astype(np.float32), len(q_pos)))


def _stack_sink(sink, tq):
    return jnp.repeat(sink.astype(F32) * LOG2E, tq).reshape(N_KV_HEADS, GROUP * tq, 1)


def kernel(x_prompt, x_sample, cache_k_a, cache_v_a, cache_k_b, cache_v_b, g_attn, g_mlp, w_qkv_a,
           g_q_a, g_k_a, sink_a, w_o_a, g_kv, w_kv, g_k_b, w_q_b, g_q_b, rel_bias_b, w_o_b, w_up, w_down):
    batch, seq, d = x_prompt.shape
    dec_batch, t_new, _ = x_sample.shape
    n_layers_a = w_qkv_a.shape[0]
    n_layers_b = w_q_b.shape[0]
    len_a = cache_k_a.shape[2]
    len_b = cache_k_b.shape[1]
    past_len = 1024
    keep_a = min(LEFT_CHUNKS_A * CHUNK, seq)
    keep_b = min(LEFT_CHUNKS_B * CHUNK, seq)
    dq = N_HEADS * HEAD_DIM
    q_scale = HEAD_DIM ** -0.5 * LOG2E

    hp = x_prompt.reshape(batch * seq, d)
    hs = x_sample.reshape(dec_batch * t_new, d)

    pos_s = past_len + np.arange(t_new)
    kpos_a = np.concatenate([past_len - len_a + np.arange(len_a), pos_s])
    kpos_b = np.concatenate([past_len - len_b + np.arange(len_b), pos_s])
    chunk_q = np.arange(CHUNK)
    bias_a_prompt = _alibi_bias(chunk_q, np.arange(-LEFT_CHUNKS_A * CHUNK, CHUNK), static_mask=False)
    bias_a_sample = _alibi_bias(pos_s, kpos_a, static_mask=True)

    ones_kv = jnp.ones((KV_WIDTH,), F32)
    seg_a = [(0, dq, True), (dq, KV_WIDTH, True), (dq + KV_WIDTH, KV_WIDTH, False)]

    ka_p, va_p, ka_s, va_s = [], [], [], []
    for i in range(n_layers_a):
        w = w_qkv_a[i].astype(BF16)
        gains = jnp.concatenate([jnp.tile(g_q_a[i] * q_scale, N_HEADS), jnp.tile(g_k_a[i], N_KV_HEADS), ones_kv])
        sink_p = _stack_sink(sink_a[i], CHUNK)
        sink_s = _stack_sink(sink_a[i], t_new)
        wo = w_o_a[i].astype(BF16)
        wup = w_up[i].astype(BF16)
        wdn = w_down[i].astype(BF16)

        qp, kp, vp = _project(hp, g_attn[i], w, gains, seg_a, (BF16, F32, F32))
        kp3 = kp.reshape(batch, seq, KV_WIDTH)
        vp3 = vp.reshape(batch, seq, KV_WIDTH)
        ap = _prompt_attention(qp.reshape(batch, seq, dq), kp3, vp3, bias_a_prompt, sink_p, LEFT_CHUNKS_A)
        ka_p.append(kp3[:, seq - keep_a:].reshape(batch, keep_a, N_KV_HEADS, HEAD_DIM))
        va_p.append(vp3[:, seq - keep_a:].reshape(batch, keep_a, N_KV_HEADS, HEAD_DIM))

        qs, ks, vs = _project(hs, g_attn[i], w, gains, seg_a, (BF16, F32, F32))
        ks3 = ks.reshape(dec_batch, t_new, KV_WIDTH)
        vs3 = vs.reshape(dec_batch, t_new, KV_WIDTH)
        as_ = _sample_attention(qs.reshape(dec_batch, t_new, dq),
                                cache_k_a[i].reshape(dec_batch, len_a, KV_WIDTH), ks3,
                                cache_v_a[i].reshape(dec_batch, len_a, KV_WIDTH), vs3,
                                bias_a_sample, sink_s)
        ka_s.append(ks3.reshape(dec_batch, t_new, N_KV_HEADS, HEAD_DIM))
        va_s.append(vs3.reshape(dec_batch, t_new, N_KV_HEADS, HEAD_DIM))

        hp = _out_mlp(hp, ap.reshape(batch * seq, dq), wo, g_mlp[i], wup, wdn)
        hs = _out_mlp(hs, as_.reshape(dec_batch * t_new, dq), wo, g_mlp[i], wup, wdn)

    wkv = w_kv.astype(BF16)
    gains_kv = jnp.concatenate([jnp.tile(g_k_b, N_KV_HEADS), ones_kv])
    seg_kv = [(0, KV_WIDTH, True), (KV_WIDTH, KV_WIDTH, False)]
    kb_p, vb_p = _project(hp, g_kv, wkv, gains_kv, seg_kv, (F32, F32))
    kb_s, vb_s = _project(hs, g_kv, wkv, gains_kv, seg_kv, (F32, F32))
    kb_p3 = kb_p.reshape(batch, seq, KV_WIDTH)
    vb_p3 = vb_p.reshape(batch, seq, KV_WIDTH)
    kb_s3 = kb_s.reshape(dec_batch, t_new, KV_WIDTH)
    vb_s3 = vb_s.reshape(dec_batch, t_new, KV_WIDTH)
    cache_kb = cache_k_b.reshape(dec_batch, len_b, KV_WIDTH)
    cache_vb = cache_v_b.reshape(dec_batch, len_b, KV_WIDTH)

    pad_b = LEFT_CHUNKS_B * CHUNK
    tk_b = pad_b + CHUNK
    assert int(pos_s[0] - kpos_b[0]) == pad_b and t_new <= CHUNK and len_b + t_new <= tk_b
    assert np.all(np.diff(kpos_b) == 1)
    rel = _rel_bias(rel_bias_b.astype(F32) * LOG2E, CHUNK, tk_b, pad_b)
    rel = rel.transpose(0, 2, 1, 3)
    mask_s = np.where(_np_band_mask(pos_s, kpos_b, LEFT_CHUNKS_B), 0.0, NEG).astype(np.float32)

    for j in range(n_layers_b):
        layer = n_layers_a + j
        wq = w_q_b[j].astype(BF16)
        gains = jnp.tile(g_q_b[j] * q_scale, N_HEADS)
        wo = w_o_b[j].astype(BF16)
        wup = w_up[layer].astype(BF16)
        wdn = w_down[layer].astype(BF16)
        bias_p = _stack_heads(rel[j], CHUNK)
        bias_s = _stack_heads(rel[j][:, :t_new, :len_b + t_new] + mask_s[None], t_new)
        seg_q = [(0, dq, True)]

        (qp,) = _project(hp, g_attn[layer], wq, gains, seg_q, (BF16,))
        ap = _prompt_attention(qp.reshape(batch, seq, dq), kb_p3, vb_p3, bias_p, None, LEFT_CHUNKS_B)
        (qs,) = _project(hs, g_attn[layer], wq, gains, seg_q, (BF16,))
        as_ = _sample_attention(qs.reshape(dec_batch, t_new, dq), cache_kb, kb_s3, cache_vb, vb_s3, bias_s, None)

        hp = _out_mlp(hp, ap.reshape(batch * seq, dq), wo, g_mlp[layer], wup, wdn)
        hs = _out_mlp(hs, as_.reshape(dec_batch * t_new, dq), wo, g_mlp[layer], wup, wdn)

    kv4 = lambda a, n, t: a.reshape(n, t, N_KV_HEADS, HEAD_DIM)
    return (hp.reshape(batch, seq, d), hs.reshape(dec_batch, t_new, d),
            jnp.stack(ka_p), jnp.stack(va_p),
            kv4(kb_p3[:, seq - keep_b:], batch, keep_b), kv4(vb_p3[:, seq - keep_b:], batch, keep_b),
            jnp.stack(ka_s), jnp.stack(va_s),
            kv4(kb_s3, dec_batch, t_new), kv4(vb_s3, dec_batch, t_new))
```

```python
import functools
import math

import numpy as np
import jax
import jax.numpy as jnp
from jax import lax
from jax.experimental import pallas as pl
from jax.experimental.pallas import tpu as pltpu

CHUNK = 64
HEAD_DIM = 64
N_KV_HEADS = 4
GROUP = 4
N_HEADS = N_KV_HEADS * GROUP
LEFT_CHUNKS_A = 2
LEFT_CHUNKS_B = 8
MAX_REL = 128
N_REL = 2 * MAX_REL + 1
PAST_LEN = 1024
EPS = 1e-6
NEG = -1e30
LOG2E = math.log2(math.e)

LANES = 128
KV_WIDTH = N_KV_HEADS * HEAD_DIM
GROUP_WIDTH = GROUP * HEAD_DIM
PAIR = 2 * CHUNK
ROW_TILE = 512
VMEM_LIMIT = 56 * 1024 * 1024

BF16 = jnp.bfloat16
F32 = jnp.float32


def _const_spec(shape):
    zeros = (0,) * len(shape)
    return pl.BlockSpec(shape, lambda *_: zeros, pipeline_mode=pl.Buffered(1))


def _head_sum_matrix():
    idx = np.arange(GROUP_WIDTH) // HEAD_DIM
    return jnp.asarray(idx[:, None] == idx[None, :], dtype=BF16)


def _replicate_matrices():
    src = np.arange(KV_WIDTH)[:, None]
    dst = np.arange(GROUP_WIDTH)[None, :]
    rep = [(src == h * HEAD_DIM + dst % HEAD_DIM) for h in range(N_KV_HEADS)]
    return jnp.asarray(np.stack(rep), dtype=BF16)


def _group_major_perm():
    g, h, d = np.meshgrid(np.arange(GROUP), np.arange(N_KV_HEADS), np.arange(HEAD_DIM), indexing="ij")
    return ((h * GROUP + g) * HEAD_DIM + d).reshape(-1)


def _rms_rows(x):
    return x * lax.rsqrt(jnp.mean(x * x, axis=-1, keepdims=True) + EPS)


def _proj_kernel(*refs, segments, has_vt):
    if has_vt:
        x_ref, g_ref, w_ref, hsum_ref, gain_ref, wt_ref = refs[:6]
        out_refs = refs[6:]
    else:
        x_ref, g_ref, w_ref, hsum_ref, gain_ref = refs[:5]
        out_refs = refs[5:]
    xn = (_rms_rows(x_ref[...]) * g_ref[...]).astype(BF16)
    y = jnp.dot(xn, w_ref[...], preferred_element_type=F32)
    for (start, width, normed), o_ref in zip(segments, out_refs):
        for t in range(width // GROUP_WIDTH):
            lo = start + t * GROUP_WIDTH
            yt = y[:, lo:lo + GROUP_WIDTH]
            if normed:
                ss = jnp.dot((yt * yt).astype(BF16), hsum_ref[...], preferred_element_type=F32)
                yt = yt * lax.rsqrt(ss * (1.0 / HEAD_DIM) + EPS) * gain_ref[:, lo:lo + GROUP_WIDTH]
            o_ref[:, t * GROUP_WIDTH:(t + 1) * GROUP_WIDTH] = yt.astype(o_ref.dtype)
    if has_vt:
        vt = lax.dot_general(wt_ref[...], xn, (((1,), (1,)), ((), ())), preferred_element_type=F32)
        out_refs[-1][...] = vt.astype(out_refs[-1].dtype)


def _project(x, g, w, gains, segments, out_dtypes, wt=None, seq=None):
    n, d = x.shape
    c = w.shape[1]
    tile = min(ROW_TILE, n)
    assert n % tile == 0
    out_shape = [jax.ShapeDtypeStruct((n, width), dt) for (_, width, _), dt in zip(segments, out_dtypes)]
    out_specs = [pl.BlockSpec((tile, width), lambda i: (i, 0)) for (_, width, _) in segments]
    in_specs = [
        pl.BlockSpec((tile, d), lambda i: (i, 0)),
        _const_spec((1, d)),
        _const_spec((d, c)),
        _const_spec((GROUP_WIDTH, GROUP_WIDTH)),
        _const_spec((1, c)),
    ]
    args = [x, g.reshape(1, d), w, _head_sum_matrix(), gains.reshape(1, c)]
    if wt is not None:
        assert seq % tile == 0 and n % seq == 0
        per_seq = seq // tile
        in_specs.append(_const_spec(wt.shape))
        args.append(wt)
        out_shape.append(jax.ShapeDtypeStruct((n // seq, wt.shape[0], seq), BF16))
        out_specs.append(pl.BlockSpec((None, wt.shape[0], tile), lambda i: (i // per_seq, 0, i % per_seq)))
    return pl.pallas_call(
        functools.partial(_proj_kernel, segments=tuple(segments), has_vt=wt is not None),
        grid=(n // tile,),
        in_specs=in_specs,
        out_specs=out_specs,
        out_shape=out_shape,
        compiler_params=pltpu.CompilerParams(
            dimension_semantics=("arbitrary",), vmem_limit_bytes=VMEM_LIMIT),
        name="project",
    )(*args)


def _band_attn_kernel(*refs, prev, n_pairs, has_sink):
    if has_sink:
        q_ref, kp_ref, kc_ref, vtp_ref, vtc_ref, bias_ref, sink_ref, o_ref, kcat, vt3 = refs
    else:
        q_ref, kp_ref, kc_ref, vtp_ref, vtc_ref, bias_ref, o_ref, kcat, vt3 = refs
        sink_ref = None
    tile = n_pairs * PAIR
    band = prev + PAIR
    n_cols = 2 * GROUP * CHUNK

    kcat[0:prev, :] = kp_ref[...]
    kcat[prev:prev + tile, :] = kc_ref[...]
    for b in range(prev // LANES):
        vt3[b] = vtp_ref[:, b * LANES:(b + 1) * LANES]
    for b in range(tile // LANES):
        vt3[prev // LANES + b] = vtc_ref[:, b * LANES:(b + 1) * LANES]

    lane_head = lax.broadcasted_iota(jnp.int32, (CHUNK, KV_WIDTH), 1) // HEAD_DIM

    def one_pair(j, masked):
        r0 = pl.multiple_of(j * PAIR, PAIR)
        qp = q_ref[pl.ds(r0, PAIR), :]
        kb = kcat[pl.ds(r0, band), :]
        if masked:
            valid = lax.broadcasted_iota(jnp.int32, (band, n_cols), 0) >= prev - r0
        outs = []
        for h in range(N_KV_HEADS):
            qs = jnp.concatenate(
                [jnp.where(lane_head == h, qp[c * CHUNK:(c + 1) * CHUNK, g * KV_WIDTH:(g + 1) * KV_WIDTH], 0)
                 for c in range(2) for g in range(GROUP)], axis=0).astype(BF16)
            s = lax.dot_general(kb, qs, (((1,), (1,)), ((), ())), preferred_element_type=F32)
            s = s + bias_ref[h]
            if masked:
                s = jnp.where(valid, s, NEG)
            m = jnp.max(s, axis=0, keepdims=True)
            if has_sink:
                m = jnp.maximum(m, sink_ref[h])
            e = jnp.exp2(s - m)
            denom = jnp.sum(e, axis=0, keepdims=True)
            if has_sink:
                denom = denom + jnp.exp2(sink_ref[h] - m)
            eb = e.astype(BF16)
            ot = jnp.zeros((HEAD_DIM, n_cols), F32)
            for b in range(band // LANES):
                ot = ot + jnp.dot(vt3[j + b, h * HEAD_DIM:(h + 1) * HEAD_DIM, :],
                                  eb[b * LANES:(b + 1) * LANES, :], preferred_element_type=F32)
            outs.append(ot * (1.0 / denom))
        ot_all = jnp.concatenate(outs, axis=0)
        for c in range(2):
            oc = ot_all[:, c * GROUP_WIDTH:(c + 1) * GROUP_WIDTH].T
            for g in range(GROUP):
                o_ref[pl.ds(r0 + c * CHUNK, CHUNK), g * KV_WIDTH:(g + 1) * KV_WIDTH] = (
                    oc[g * CHUNK:(g + 1) * CHUNK, :].astype(o_ref.dtype))

    def run(masked):
        def body(j, carry):
            one_pair(j, masked)
            return carry
        lax.fori_loop(0, n_pairs, body, 0)

    @pl.when(pl.program_id(1) == 0)
    def _():
        run(True)

    @pl.when(pl.program_id(1) > 0)
    def _():
        run(False)


def _prompt_attention(q, k, vt, bias, sink, left_chunks):
    b, s, dq = q.shape
    prev = left_chunks * CHUNK
    tile = min(ROW_TILE, s)
    band = prev + PAIR
    assert s % tile == 0 and tile % prev == 0 and prev % LANES == 0 and tile % PAIR == 0
    assert bias.shape == (N_KV_HEADS, band, 2 * GROUP * CHUNK)
    ratio = tile // prev
    has_sink = sink is not None
    cur_rows = lambda bi, i: (bi, i, 0)
    prev_rows = lambda bi, i: (bi, jnp.maximum(i * ratio - 1, 0), 0)
    cur_cols = lambda bi, i: (bi, 0, i)
    prev_cols = lambda bi, i: (bi, 0, jnp.maximum(i * ratio - 1, 0))
    in_specs = [
        pl.BlockSpec((None, tile, dq), cur_rows),
        pl.BlockSpec((None, prev, KV_WIDTH), prev_rows),
        pl.BlockSpec((None, tile, KV_WIDTH), cur_rows),
        pl.BlockSpec((None, KV_WIDTH, prev), prev_cols),
        pl.BlockSpec((None, KV_WIDTH, tile), cur_cols),
        _const_spec(bias.shape),
    ]
    args = [q, k, k, vt, vt, bias]
    if has_sink:
        in_specs.append(_const_spec(sink.shape))
        args.append(sink)
    return pl.pallas_call(
        functools.partial(_band_attn_kernel, prev=prev, n_pairs=tile // PAIR, has_sink=has_sink),
        grid=(b, s // tile),
        in_specs=in_specs,
        out_specs=pl.BlockSpec((None, tile, dq), cur_rows),
        out_shape=jax.ShapeDtypeStruct((b, s, dq), BF16),
        scratch_shapes=[
            pltpu.VMEM((prev + tile, KV_WIDTH), BF16),
            pltpu.VMEM(((prev + tile) // LANES, KV_WIDTH, LANES), BF16),
        ],
        compiler_params=pltpu.CompilerParams(
            dimension_semantics=("arbitrary", "arbitrary"), vmem_limit_bytes=VMEM_LIMIT),
        name="band_attention",
    )(*args)


def _pair_bias(per_head):
    h, tq, tk = per_head.shape
    neg = jnp.full((h, tq, CHUNK), NEG, F32)
    both = jnp.stack([jnp.concatenate([per_head, neg], axis=-1),
                      jnp.concatenate([neg, per_head], axis=-1)], axis=1)
    both = both.reshape(N_KV_HEADS, GROUP, 2, tq, tk + CHUNK)
    return both.transpose(0, 4, 2, 1, 3).reshape(N_KV_HEADS, tk + CHUNK, 2 * GROUP * tq)


def _pair_sink(sink):
    s = (sink.astype(F32) * LOG2E).reshape(N_KV_HEADS, 1, GROUP, 1)
    return jnp.broadcast_to(s, (N_KV_HEADS, 2, GROUP, CHUNK)).reshape(N_KV_HEADS, 1, 2 * GROUP * CHUNK)


def _sample_attn_kernel(*refs, tq, prev, has_sink):
    if has_sink:
        q_ref, kp_ref, kc_ref, vp_ref, vc_ref, bias_ref, rep_ref, sink_ref, o_ref, krep, vrep = refs
    else:
        q_ref, kp_ref, kc_ref, vp_ref, vc_ref, bias_ref, rep_ref, o_ref, krep, vrep = refs
        sink_ref = None

    def replicate(dst, src_ref, off, rows):
        xb = src_ref[...].astype(BF16)
        for h in range(N_KV_HEADS):
            dst[off:off + rows, h * GROUP_WIDTH:(h + 1) * GROUP_WIDTH] = jnp.dot(
                xb, rep_ref[h], preferred_element_type=F32).astype(BF16)

    replicate(krep, kp_ref, 0, prev)
    replicate(krep, kc_ref, prev, tq)
    replicate(vrep, vp_ref, 0, prev)
    replicate(vrep, vc_ref, prev, tq)

    lane_group = lax.broadcasted_iota(jnp.int32, (tq, GROUP_WIDTH), 1) // HEAD_DIM
    in_group = [lane_group == g for g in range(GROUP)]
    qc = q_ref[...]
    for h in range(N_KV_HEADS):
        qg = qc[:, h * GROUP_WIDTH:(h + 1) * GROUP_WIDTH]
        qs = jnp.concatenate(
            [jnp.where(in_group[g], qg, jnp.zeros_like(qg)) for g in range(GROUP)], axis=0)
        kr = krep[:, h * GROUP_WIDTH:(h + 1) * GROUP_WIDTH]
        s = lax.dot_general(qs, kr, (((1,), (1,)), ((), ())), preferred_element_type=F32)
        s = s + bias_ref[h]
        m = jnp.max(s, axis=-1, keepdims=True)
        if has_sink:
            m = jnp.maximum(m, sink_ref[h])
        e = jnp.exp2(s - m)
        denom = jnp.sum(e, axis=-1, keepdims=True)
        if has_sink:
            denom = denom + jnp.exp2(sink_ref[h] - m)
        p = (e * (1.0 / denom)).astype(BF16)
        vr = vrep[:, h * GROUP_WIDTH:(h + 1) * GROUP_WIDTH]
        o = jnp.zeros((tq, GROUP_WIDTH), F32)
        for g in range(GROUP):
            og = jnp.dot(p[g * tq:(g + 1) * tq], vr, preferred_element_type=F32)
            o = jnp.where(in_group[g], og, o)
        o_ref[:, h * GROUP_WIDTH:(h + 1) * GROUP_WIDTH] = o.astype(o_ref.dtype)


def _sample_attention(q, k_cache, k_new, v_cache, v_new, bias, sink):
    b, t, dq = q.shape
    prev = k_cache.shape[1]
    assert bias.shape == (N_KV_HEADS, GROUP * t, prev + t)
    has_sink = sink is not None
    row_map = lambda bi: (bi, 0, 0)
    in_specs = [
        pl.BlockSpec((None, t, dq), row_map),
        pl.BlockSpec((None, prev, KV_WIDTH), row_map),
        pl.BlockSpec((None, t, KV_WIDTH), row_map),
        pl.BlockSpec((None, prev, KV_WIDTH), row_map),
        pl.BlockSpec((None, t, KV_WIDTH), row_map),
        _const_spec(bias.shape),
        _const_spec((N_KV_HEADS, KV_WIDTH, GROUP_WIDTH)),
    ]
    args = [q, k_cache, k_new, v_cache, v_new, bias, _replicate_matrices()]
    if has_sink:
        in_specs.append(_const_spec(sink.shape))
        args.append(sink)
    return pl.pallas_call(
        functools.partial(_sample_attn_kernel, tq=t, prev=prev, has_sink=has_sink),
        grid=(b,),
        in_specs=in_specs,
        out_specs=pl.BlockSpec((None, t, dq), row_map),
        out_shape=jax.ShapeDtypeStruct((b, t, dq), BF16),
        scratch_shapes=[
            pltpu.VMEM((prev + t, N_KV_HEADS * GROUP_WIDTH), BF16),
            pltpu.VMEM((prev + t, N_KV_HEADS * GROUP_WIDTH), BF16),
        ],
        compiler_params=pltpu.CompilerParams(
            dimension_semantics=("arbitrary",), vmem_limit_bytes=VMEM_LIMIT),
        name="sample_attention",
    )(*args)


def _out_mlp_kernel(x_ref, a_ref, wo_ref, g_ref, wup_ref, wdn_ref, o_ref, *, ff_chunks):
    h = x_ref[...] + jnp.dot(a_ref[...], wo_ref[...], preferred_element_type=F32)
    hn = (_rms_rows(h) * g_ref[...]).astype(BF16)
    d_ff = wup_ref.shape[1]
    step = d_ff // ff_chunks
    acc = h
    for c in range(ff_chunks):
        u = jnp.dot(hn, wup_ref[:, c * step:(c + 1) * step], preferred_element_type=F32)
        u = jnp.square(jnp.maximum(u, 0.0)).astype(BF16)
        acc = acc + jnp.dot(u, wdn_ref[c * step:(c + 1) * step, :], preferred_element_type=F32)
    o_ref[...] = acc


def _out_mlp(x, a, wo, g, wup, wdn):
    n, d = x.shape
    d_ff = wup.shape[1]
    tile = min(ROW_TILE, n)
    assert n % tile == 0
    row_spec = pl.BlockSpec((tile, d), lambda i: (i, 0))
    return pl.pallas_call(
        functools.partial(_out_mlp_kernel, ff_chunks=4),
        grid=(n // tile,),
        in_specs=[
            row_spec,
            row_spec,
            _const_spec((d, d)),
            _const_spec((1, d)),
            _const_spec((d, d_ff)),
            _const_spec((d_ff, d)),
        ],
        out_specs=row_spec,
        out_shape=jax.ShapeDtypeStruct((n, d), F32),
        compiler_params=pltpu.CompilerParams(
            dimension_semantics=("arbitrary",), vmem_limit_bytes=VMEM_LIMIT),
        name="out_proj_mlp",
    )(x, a, wo, g.reshape(1, d), wup, wdn)


def _rel_bias_kernel(tab_ref, o_ref, *, tk, delta):
    q = pl.program_id(1)
    n_pad = tab_ref.shape[-1]
    r = lax.broadcasted_iota(jnp.int32, (n_pad, tk), 0)
    j = lax.broadcasted_iota(jnp.int32, (n_pad, tk), 1)
    idx = jnp.clip(q + delta - j, -MAX_REL, MAX_REL) + MAX_REL
    onehot = jnp.where(r == idx, 1.0, 0.0).astype(BF16)
    acc = jnp.dot(tab_ref[0], onehot, preferred_element_type=F32)
    acc = acc + jnp.dot(tab_ref[1], onehot, preferred_element_type=F32)
    acc = acc + jnp.dot(tab_ref[2], onehot, preferred_element_type=F32)
    o_ref[...] = acc


def _rel_bias(table, tq, tk, delta):
    n_layers, h, n_rel = table.shape
    n_pad = -(-n_rel // LANES) * LANES
    t = jnp.pad(table, ((0, 0), (0, 0), (0, n_pad - n_rel)))
    hi = t.astype(BF16)
    mid = (t - hi.astype(F32)).astype(BF16)
    lo = (t - hi.astype(F32) - mid.astype(F32)).astype(BF16)
    parts = jnp.stack([hi, mid, lo], axis=1)
    return pl.pallas_call(
        functools.partial(_rel_bias_kernel, tk=tk, delta=delta),
        grid=(n_layers, tq),
        in_specs=[pl.BlockSpec((None, 3, h, n_pad), lambda l, q: (l, 0, 0, 0))],
        out_specs=pl.BlockSpec((None, None, h, tk), lambda l, q: (l, q, 0, 0)),
        out_shape=jax.ShapeDtypeStruct((n_layers, tq, h, tk), F32),
        compiler_params=pltpu.CompilerParams(dimension_semantics=("arbitrary", "arbitrary")),
        name="rel_bias_table",
    )(parts)


def _stack_heads(per_head, tq):
    return per_head.reshape(N_KV_HEADS, GROUP * tq, per_head.shape[-1])


def _np_band_mask(q_pos, k_pos, left_chunks):
    qc = q_pos[:, None] // CHUNK
    kc = k_pos[None, :] // CHUNK
    return (k_pos[None, :] >= 0) & (kc <= qc) & (kc >= qc - left_chunks)


def _alibi_bias(q_pos, k_pos, static_mask):
    slopes = (2.0 ** (-8.0 * np.arange(1, N_HEADS + 1, dtype=np.float32) / N_HEADS)).astype(np.float32)
    dist = np.abs(q_pos[:, None] - k_pos[None, :]).astype(np.float32)
    bias = -slopes[:, None, None] * dist[None] * np.float32(LOG2E)
    if static_mask:
        bias = np.where(_np_band_mask(q_pos, k_pos, LEFT_CHUNKS_A)[None], bias, np.float32(NEG))
    return jnp.asarray(bias.astype(np.float32))


def _stack_sink(sink, tq):
    return jnp.repeat(sink.astype(F32) * LOG2E, tq).reshape(N_KV_HEADS, GROUP * tq, 1)


def kernel(x_prompt, x_sample, cache_k_a, cache_v_a, cache_k_b, cache_v_b, g_attn, g_mlp, w_qkv_a,
           g_q_a, g_k_a, sink_a, w_o_a, g_kv, w_kv, g_k_b, w_q_b, g_q_b, rel_bias_b, w_o_b, w_up, w_down):
    batch, seq, d = x_prompt.shape
    dec_batch, t_new, _ = x_sample.shape
    n_layers_a = w_qkv_a.shape[0]
    n_layers_b = w_q_b.shape[0]
    len_a = cache_k_a.shape[2]
    len_b = cache_k_b.shape[1]
    keep_a = min(LEFT_CHUNKS_A * CHUNK, seq)
    keep_b = min(LEFT_CHUNKS_B * CHUNK, seq)
    dq = N_HEADS * HEAD_DIM
    q_scale = HEAD_DIM ** -0.5 * LOG2E
    perm = _group_major_perm()

    hp = x_prompt.reshape(batch * seq, d)
    hs = x_sample.reshape(dec_batch * t_new, d)

    pos_s = PAST_LEN + np.arange(t_new)
    kpos_a = np.concatenate([PAST_LEN - len_a + np.arange(len_a), pos_s])
    kpos_b = np.concatenate([PAST_LEN - len_b + np.arange(len_b), pos_s])
    chunk_q = np.arange(CHUNK)
    bias_a_prompt = _pair_bias(_alibi_bias(chunk_q, np.arange(-LEFT_CHUNKS_A * CHUNK, CHUNK), static_mask=False))
    bias_a_sample = _stack_heads(_alibi_bias(pos_s, kpos_a, static_mask=True), t_new)

    ones_kv = jnp.ones((KV_WIDTH,), F32)
    seg_a_p = [(0, dq, True), (dq, KV_WIDTH, True), (dq + KV_WIDTH, KV_WIDTH, False), (dq, KV_WIDTH, True)]
    seg_a_s = seg_a_p[:3]

    ka_p, va_p, ka_s, va_s = [], [], [], []
    for i in range(n_layers_a):
        w = w_qkv_a[i].astype(BF16)
        w_p = jnp.concatenate([w[:, :dq][:, perm], w[:, dq:]], axis=1)
        wvt = w[:, dq + KV_WIDTH:].T
        gains = jnp.concatenate([jnp.tile(g_q_a[i] * q_scale, N_HEADS), jnp.tile(g_k_a[i], N_KV_HEADS), ones_kv])
        wo = w_o_a[i].astype(BF16)
        wup = w_up[i].astype(BF16)
        wdn = w_down[i].astype(BF16)

        qp, kp, vp, kp16, vtp = _project(hp, g_attn[i], w_p, gains, seg_a_p, (BF16, F32, F32, BF16),
                                         wt=wvt, seq=seq)
        ap = _prompt_attention(qp.reshape(batch, seq, dq), kp16.reshape(batch, seq, KV_WIDTH), vtp,
                               bias_a_prompt, _pair_sink(sink_a[i]), LEFT_CHUNKS_A)
        kp3 = kp.reshape(batch, seq, KV_WIDTH)
        vp3 = vp.reshape(batch, seq, KV_WIDTH)
        ka_p.append(kp3[:, seq - keep_a:].reshape(batch, keep_a, N_KV_HEADS, HEAD_DIM))
        va_p.append(vp3[:, seq - keep_a:].reshape(batch, keep_a, N_KV_HEADS, HEAD_DIM))

        qs, ks, vs = _project(hs, g_attn[i], w, gains, seg_a_s, (BF16, F32, F32))
        ks3 = ks.reshape(dec_batch, t_new, KV_WIDTH)
        vs3 = vs.reshape(dec_batch, t_new, KV_WIDTH)
        as_ = _sample_attention(qs.reshape(dec_batch, t_new, dq),
                                cache_k_a[i].reshape(dec_batch, len_a, KV_WIDTH), ks3,
                                cache_v_a[i].reshape(dec_batch, len_a, KV_WIDTH), vs3,
                                bias_a_sample, _stack_sink(sink_a[i], t_new))
        ka_s.append(ks3.reshape(dec_batch, t_new, N_KV_HEADS, HEAD_DIM))
        va_s.append(vs3.reshape(dec_batch, t_new, N_KV_HEADS, HEAD_DIM))

        hp = _out_mlp(hp, ap.reshape(batch * seq, dq), wo[perm, :], g_mlp[i], wup, wdn)
        hs = _out_mlp(hs, as_.reshape(dec_batch * t_new, dq), wo, g_mlp[i], wup, wdn)

    wkv = w_kv.astype(BF16)
    gains_kv = jnp.concatenate([jnp.tile(g_k_b, N_KV_HEADS), ones_kv])
    seg_kv_s = [(0, KV_WIDTH, True), (KV_WIDTH, KV_WIDTH, False)]
    seg_kv_p = seg_kv_s + [(0, KV_WIDTH, True)]
    kb_p, vb_p, kb_p16, vtb_p = _project(hp, g_kv, wkv, gains_kv, seg_kv_p, (F32, F32, BF16),
                                         wt=wkv[:, KV_WIDTH:].T, seq=seq)
    kb_s, vb_s = _project(hs, g_kv, wkv, gains_kv, seg_kv_s, (F32, F32))
    kb_p3 = kb_p.reshape(batch, seq, KV_WIDTH)
    vb_p3 = vb_p.reshape(batch, seq, KV_WIDTH)
    kb_p16 = kb_p16.reshape(batch, seq, KV_WIDTH)
    kb_s3 = kb_s.reshape(dec_batch, t_new, KV_WIDTH)
    vb_s3 = vb_s.reshape(dec_batch, t_new, KV_WIDTH)
    cache_kb = cache_k_b.reshape(dec_batch, len_b, KV_WIDTH)
    cache_vb = cache_v_b.reshape(dec_batch, len_b, KV_WIDTH)

    pad_b = LEFT_CHUNKS_B * CHUNK
    tk_b = pad_b + CHUNK
    assert int(pos_s[0] - kpos_b[0]) == pad_b and t_new <= CHUNK and len_b + t_new <= tk_b
    assert np.all(np.diff(kpos_b) == 1)
    rel = _rel_bias(rel_bias_b.astype(F32) * LOG2E, CHUNK, tk_b, pad_b)
    rel = rel.transpose(0, 2, 1, 3)
    mask_s = np.where(_np_band_mask(pos_s, kpos_b, LEFT_CHUNKS_B), 0.0, NEG).astype(np.float32)

    seg_q = [(0, dq, True)]
    for j in range(n_layers_b):
        layer = n_layers_a + j
        wq = w_q_b[j].astype(BF16)
        gains = jnp.tile(g_q_b[j] * q_scale, N_HEADS)
        wo = w_o_b[j].astype(BF16)
        wup = w_up[layer].astype(BF16)
        wdn = w_down[layer].astype(BF16)
        bias_p = _pair_bias(rel[j])
        bias_s = _stack_heads(rel[j][:, :t_new, :len_b + t_new] + mask_s[None], t_new)

        (qp,) = _project(hp, g_attn[layer], wq[:, perm], gains, seg_q, (BF16,))
        ap = _prompt_attention(qp.reshape(batch, seq, dq), kb_p16, vtb_p, bias_p, None, LEFT_CHUNKS_B)
        (qs,) = _project(hs, g_attn[layer], wq, gains, seg_q, (BF16,))
        as_ = _sample_attention(qs.reshape(dec_batch, t_new, dq), cache_kb, kb_s3, cache_vb, vb_s3, bias_s, None)

        hp = _out_mlp(hp, ap.reshape(batch * seq, dq), wo[perm, :], g_mlp[layer], wup, wdn)
        hs = _out_mlp(hs, as_.reshape(dec_batch * t_new, dq), wo, g_mlp[layer], wup, wdn)

    kv4 = lambda a, n, t: a.reshape(n, t, N_KV_HEADS, HEAD_DIM)
    return (hp.reshape(batch, seq, d), hs.reshape(dec_batch, t_new, d),
            jnp.stack(ka_p), jnp.stack(va_p),
            kv4(kb_p3[:, seq - keep_b:], batch, keep_b), kv4(vb_p3[:, seq - keep_b:], batch, keep_b),
            jnp.stack(ka_s), jnp.stack(va_s),
            kv4(kb_s3, dec_batch, t_new), kv4(vb_s3, dec_batch, t_new))
```

```python
import functools
import math

import numpy as np
import jax
import jax.numpy as jnp
from jax import lax
from jax.experimental import pallas as pl
from jax.experimental.pallas import tpu as pltpu

CHUNK = 64
HEAD_DIM = 64
N_KV_HEADS = 4
GROUP = 4
N_HEADS = N_KV_HEADS * GROUP
LEFT_CHUNKS_A = 2
LEFT_CHUNKS_B = 8
MAX_REL = 128
N_REL = 2 * MAX_REL + 1
PAST_LEN = 1024
EPS = 1e-6
NEG = -1e30
LOG2E = math.log2(math.e)

LANES = 128
KV_WIDTH = N_KV_HEADS * HEAD_DIM
GROUP_WIDTH = GROUP * HEAD_DIM
PAIR = 2 * CHUNK
ONES_ROWS = 16
ROW_TILE = 512
VMEM_LIMIT = 56 * 1024 * 1024

BF16 = jnp.bfloat16
F32 = jnp.float32


def _const_spec(shape):
    zeros = (0,) * len(shape)
    return pl.BlockSpec(shape, lambda *_: zeros, pipeline_mode=pl.Buffered(1))


def _head_sum_matrix():
    idx = np.arange(GROUP_WIDTH) // HEAD_DIM
    return jnp.asarray(idx[:, None] == idx[None, :], dtype=BF16)


def _replicate_matrices():
    src = np.arange(KV_WIDTH)[:, None]
    dst = np.arange(GROUP_WIDTH)[None, :]
    rep = [(src == h * HEAD_DIM + dst % HEAD_DIM) for h in range(N_KV_HEADS)]
    return jnp.asarray(np.stack(rep), dtype=BF16)


def _group_major_perm():
    g, h, d = np.meshgrid(np.arange(GROUP), np.arange(N_KV_HEADS), np.arange(HEAD_DIM), indexing="ij")
    return ((h * GROUP + g) * HEAD_DIM + d).reshape(-1)


def _rms_rows(x):
    return x * lax.rsqrt(jnp.mean(x * x, axis=-1, keepdims=True) + EPS)


def _proj_kernel(*refs, segments, has_vt):
    if has_vt:
        x_ref, g_ref, w_ref, hsum_ref, gain_ref, wt_ref = refs[:6]
        out_refs = refs[6:]
    else:
        x_ref, g_ref, w_ref, hsum_ref, gain_ref = refs[:5]
        out_refs = refs[5:]
    xn = (_rms_rows(x_ref[...]) * g_ref[...]).astype(BF16)
    y = jnp.dot(xn, w_ref[...], preferred_element_type=F32)
    for (start, width, normed), o_ref in zip(segments, out_refs):
        for t in range(width // GROUP_WIDTH):
            lo = start + t * GROUP_WIDTH
            yt = y[:, lo:lo + GROUP_WIDTH]
            if normed:
                ss = jnp.dot((yt * yt).astype(BF16), hsum_ref[...], preferred_element_type=F32)
                yt = yt * lax.rsqrt(ss * (1.0 / HEAD_DIM) + EPS) * gain_ref[:, lo:lo + GROUP_WIDTH]
            o_ref[:, t * GROUP_WIDTH:(t + 1) * GROUP_WIDTH] = yt.astype(o_ref.dtype)
    if has_vt:
        vt = lax.dot_general(wt_ref[...], xn, (((1,), (1,)), ((), ())), preferred_element_type=F32)
        out_refs[-1][...] = vt.astype(out_refs[-1].dtype)


def _project(x, g, w, gains, segments, out_dtypes, wt=None, seq=None):
    n, d = x.shape
    c = w.shape[1]
    tile = min(ROW_TILE, n)
    assert n % tile == 0
    out_shape = [jax.ShapeDtypeStruct((n, width), dt) for (_, width, _), dt in zip(segments, out_dtypes)]
    out_specs = [pl.BlockSpec((tile, width), lambda i: (i, 0)) for (_, width, _) in segments]
    in_specs = [
        pl.BlockSpec((tile, d), lambda i: (i, 0)),
        _const_spec((1, d)),
        _const_spec((d, c)),
        _const_spec((GROUP_WIDTH, GROUP_WIDTH)),
        _const_spec((1, c)),
    ]
    args = [x, g.reshape(1, d), w, _head_sum_matrix(), gains.reshape(1, c)]
    if wt is not None:
        assert seq % tile == 0 and n % seq == 0
        per_seq = seq // tile
        in_specs.append(_const_spec(wt.shape))
        args.append(wt)
        out_shape.append(jax.ShapeDtypeStruct((n // seq, wt.shape[0], seq), BF16))
        out_specs.append(pl.BlockSpec((None, wt.shape[0], tile), lambda i: (i // per_seq, 0, i % per_seq)))
    return pl.pallas_call(
        functools.partial(_proj_kernel, segments=tuple(segments), has_vt=wt is not None),
        grid=(n // tile,),
        in_specs=in_specs,
        out_specs=out_specs,
        out_shape=out_shape,
        compiler_params=pltpu.CompilerParams(
            dimension_semantics=("arbitrary",), vmem_limit_bytes=VMEM_LIMIT),
        name="project",
    )(*args)


def _band_attn_kernel(*refs, prev, n_pairs, has_sink):
    if has_sink:
        q_ref, kp_ref, kc_ref, vtp_ref, vtc_ref, bias_ref, sink_ref, o_ref, kcat, vt3, s_next = refs
    else:
        q_ref, kp_ref, kc_ref, vtp_ref, vtc_ref, bias_ref, o_ref, kcat, vt3, s_next = refs
        sink_ref = None
    tile = n_pairs * PAIR
    band = prev + PAIR
    n_cols = 2 * GROUP * CHUNK

    kcat[0:prev, :] = kp_ref[...]
    kcat[prev:prev + tile, :] = kc_ref[...]
    ones = jnp.ones((ONES_ROWS, LANES), BF16)
    blocks = [vtp_ref[:, b * LANES:(b + 1) * LANES] for b in range(prev // LANES)]
    blocks += [vtc_ref[:, b * LANES:(b + 1) * LANES] for b in range(tile // LANES)]
    for b, blk in enumerate(blocks):
        for h in range(N_KV_HEADS):
            vt3[b, h, 0:HEAD_DIM, :] = blk[h * HEAD_DIM:(h + 1) * HEAD_DIM, :]
            vt3[b, h, HEAD_DIM:HEAD_DIM + ONES_ROWS, :] = ones

    lane_head = lax.broadcasted_iota(jnp.int32, (CHUNK, KV_WIDTH), 1) // HEAD_DIM

    def scores(j, h, masked):
        r0 = pl.multiple_of(j * PAIR, PAIR)
        qp = q_ref[pl.ds(r0, PAIR), :]
        kb = kcat[pl.ds(r0, band), :]
        qs = jnp.concatenate(
            [jnp.where(lane_head == h, qp[c * CHUNK:(c + 1) * CHUNK, g * KV_WIDTH:(g + 1) * KV_WIDTH], 0)
             for c in range(2) for g in range(GROUP)], axis=0).astype(BF16)
        s = lax.dot_general(kb, qs, (((1,), (1,)), ((), ())), preferred_element_type=F32)
        s = s + bias_ref[h]
        if masked:
            valid = lax.broadcasted_iota(jnp.int32, (band, n_cols), 0) >= prev - r0
            s = jnp.where(valid, s, NEG)
        return s

    def softmax_pv(s, j, h):
        m = jnp.max(s, axis=0, keepdims=True)
        if has_sink:
            m = jnp.maximum(m, sink_ref[h])
        eb = jnp.exp2(s - m).astype(BF16)
        ot = jnp.zeros((HEAD_DIM + ONES_ROWS, n_cols), F32)
        for b in range(band // LANES):
            ot = ot + jnp.dot(vt3[j + b, h], eb[b * LANES:(b + 1) * LANES, :], preferred_element_type=F32)
        denom = ot[HEAD_DIM:HEAD_DIM + 1, :]
        if has_sink:
            denom = denom + jnp.exp2(sink_ref[h] - m)
        return ot[0:HEAD_DIM, :] * (1.0 / denom)

    def run(masked):
        s_next[...] = scores(0, 0, masked)

        def body(j, carry):
            r0 = pl.multiple_of(j * PAIR, PAIR)
            s = s_next[...]
            outs = []
            for h in range(N_KV_HEADS):
                if h + 1 < N_KV_HEADS:
                    s_ahead = scores(j, h + 1, masked)
                else:
                    s_next[...] = scores(jnp.minimum(j + 1, n_pairs - 1), 0, masked)
                outs.append(softmax_pv(s, j, h))
                s = s_ahead
            ot_all = jnp.concatenate(outs, axis=0)
            for c in range(2):
                oc = ot_all[:, c * GROUP_WIDTH:(c + 1) * GROUP_WIDTH].T
                for g in range(GROUP):
                    o_ref[pl.ds(r0 + c * CHUNK, CHUNK), g * KV_WIDTH:(g + 1) * KV_WIDTH] = (
                        oc[g * CHUNK:(g + 1) * CHUNK, :].astype(o_ref.dtype))
            return carry

        lax.fori_loop(0, n_pairs, body, 0)

    @pl.when(pl.program_id(1) == 0)
    def _():
        run(True)

    @pl.when(pl.program_id(1) > 0)
    def _():
        run(False)


def _prompt_attention(q, k, vt, bias, sink, left_chunks):
    b, s, dq = q.shape
    prev = left_chunks * CHUNK
    tile = min(ROW_TILE, s)
    band = prev + PAIR
    assert s % tile == 0 and tile % prev == 0 and prev % LANES == 0 and tile % PAIR == 0
    assert bias.shape == (N_KV_HEADS, band, 2 * GROUP * CHUNK)
    ratio = tile // prev
    has_sink = sink is not None
    cur_rows = lambda bi, i: (bi, i, 0)
    prev_rows = lambda bi, i: (bi, jnp.maximum(i * ratio - 1, 0), 0)
    cur_cols = lambda bi, i: (bi, 0, i)
    prev_cols = lambda bi, i: (bi, 0, jnp.maximum(i * ratio - 1, 0))
    in_specs = [
        pl.BlockSpec((None, tile, dq), cur_rows),
        pl.BlockSpec((None, prev, KV_WIDTH), prev_rows),
        pl.BlockSpec((None, tile, KV_WIDTH), cur_rows),
        pl.BlockSpec((None, KV_WIDTH, prev), prev_cols),
        pl.BlockSpec((None, KV_WIDTH, tile), cur_cols),
        _const_spec(bias.shape),
    ]
    args = [q, k, k, vt, vt, bias]
    if has_sink:
        in_specs.append(_const_spec(sink.shape))
        args.append(sink)
    return pl.pallas_call(
        functools.partial(_band_attn_kernel, prev=prev, n_pairs=tile // PAIR, has_sink=has_sink),
        grid=(b, s // tile),
        in_specs=in_specs,
        out_specs=pl.BlockSpec((None, tile, dq), cur_rows),
        out_shape=jax.ShapeDtypeStruct((b, s, dq), BF16),
        scratch_shapes=[
            pltpu.VMEM((prev + tile, KV_WIDTH), BF16),
            pltpu.VMEM(((prev + tile) // LANES, N_KV_HEADS, HEAD_DIM + ONES_ROWS, LANES), BF16),
            pltpu.VMEM((band, 2 * GROUP * CHUNK), F32),
        ],
        compiler_params=pltpu.CompilerParams(
            dimension_semantics=("arbitrary", "arbitrary"), vmem_limit_bytes=VMEM_LIMIT),
        name="band_attention",
    )(*args)


def _pair_bias(per_head):
    h, tq, tk = per_head.shape
    neg = jnp.full((h, tq, CHUNK), NEG, F32)
    both = jnp.stack([jnp.concatenate([per_head, neg], axis=-1),
                      jnp.concatenate([neg, per_head], axis=-1)], axis=1)
    both = both.reshape(N_KV_HEADS, GROUP, 2, tq, tk + CHUNK)
    return both.transpose(0, 4, 2, 1, 3).reshape(N_KV_HEADS, tk + CHUNK, 2 * GROUP * tq)


def _pair_sink(sink):
    s = (sink.astype(F32) * LOG2E).reshape(N_KV_HEADS, 1, GROUP, 1)
    return jnp.broadcast_to(s, (N_KV_HEADS, 2, GROUP, CHUNK)).reshape(N_KV_HEADS, 1, 2 * GROUP * CHUNK)


def _sample_attn_kernel(*refs, tq, prev, has_sink):
    if has_sink:
        q_ref, kp_ref, kc_ref, vp_ref, vc_ref, bias_ref, rep_ref, sink_ref, o_ref, krep, vrep = refs
    else:
        q_ref, kp_ref, kc_ref, vp_ref, vc_ref, bias_ref, rep_ref, o_ref, krep, vrep = refs
        sink_ref = None

    def replicate(dst, src_ref, off, rows):
        xb = src_ref[...].astype(BF16)
        for h in range(N_KV_HEADS):
            dst[off:off + rows, h * GROUP_WIDTH:(h + 1) * GROUP_WIDTH] = jnp.dot(
                xb, rep_ref[h], preferred_element_type=F32).astype(BF16)

    replicate(krep, kp_ref, 0, prev)
    replicate(krep, kc_ref, prev, tq)
    replicate(vrep, vp_ref, 0, prev)
    replicate(vrep, vc_ref, prev, tq)

    lane_group = lax.broadcasted_iota(jnp.int32, (tq, GROUP_WIDTH), 1) // HEAD_DIM
    in_group = [lane_group == g for g in range(GROUP)]
    qc = q_ref[...]
    for h in range(N_KV_HEADS):
        qg = qc[:, h * GROUP_WIDTH:(h + 1) * GROUP_WIDTH]
        qs = jnp.concatenate(
            [jnp.where(in_group[g], qg, jnp.zeros_like(qg)) for g in range(GROUP)], axis=0)
        kr = krep[:, h * GROUP_WIDTH:(h + 1) * GROUP_WIDTH]
        s = lax.dot_general(qs, kr, (((1,), (1,)), ((), ())), preferred_element_type=F32)
        s = s + bias_ref[h]
        m = jnp.max(s, axis=-1, keepdims=True)
        if has_sink:
            m = jnp.maximum(m, sink_ref[h])
        e = jnp.exp2(s - m)
        denom = jnp.sum(e, axis=-1, keepdims=True)
        if has_sink:
            denom = denom + jnp.exp2(sink_ref[h] - m)
        p = (e * (1.0 / denom)).astype(BF16)
        vr = vrep[:, h * GROUP_WIDTH:(h + 1) * GROUP_WIDTH]
        o = jnp.zeros((tq, GROUP_WIDTH), F32)
        for g in range(GROUP):
            og = jnp.dot(p[g * tq:(g + 1) * tq], vr, preferred_element_type=F32)
            o = jnp.where(in_group[g], og, o)
        o_ref[:, h * GROUP_WIDTH:(h + 1) * GROUP_WIDTH] = o.astype(o_ref.dtype)


def _sample_attention(q, k_cache, k_new, v_cache, v_new, bias, sink):
    b, t, dq = q.shape
    prev = k_cache.shape[1]
    assert bias.shape == (N_KV_HEADS, GROUP * t, prev + t)
    has_sink = sink is not None
    row_map = lambda bi: (bi, 0, 0)
    in_specs = [
        pl.BlockSpec((None, t, dq), row_map),
        pl.BlockSpec((None, prev, KV_WIDTH), row_map),
        pl.BlockSpec((None, t, KV_WIDTH), row_map),
        pl.BlockSpec((None, prev, KV_WIDTH), row_map),
        pl.BlockSpec((None, t, KV_WIDTH), row_map),
        _const_spec(bias.shape),
        _const_spec((N_KV_HEADS, KV_WIDTH, GROUP_WIDTH)),
    ]
    args = [q, k_cache, k_new, v_cache, v_new, bias, _replicate_matrices()]
    if has_sink:
        in_specs.append(_const_spec(sink.shape))
        args.append(sink)
    return pl.pallas_call(
        functools.partial(_sample_attn_kernel, tq=t, prev=prev, has_sink=has_sink),
        grid=(b,),
        in_specs=in_specs,
        out_specs=pl.BlockSpec((None, t, dq), row_map),
        out_shape=jax.ShapeDtypeStruct((b, t, dq), BF16),
        scratch_shapes=[
            pltpu.VMEM((prev + t, N_KV_HEADS * GROUP_WIDTH), BF16),
            pltpu.VMEM((prev + t, N_KV_HEADS * GROUP_WIDTH), BF16),
        ],
        compiler_params=pltpu.CompilerParams(
            dimension_semantics=("arbitrary",), vmem_limit_bytes=VMEM_LIMIT),
        name="sample_attention",
    )(*args)


def _out_mlp_kernel(x_ref, a_ref, wo_ref, g_ref, wup_ref, wdn_ref, o_ref, *, ff_chunks):
    h = x_ref[...] + jnp.dot(a_ref[...], wo_ref[...], preferred_element_type=F32)
    hn = (_rms_rows(h) * g_ref[...]).astype(BF16)
    d_ff = wup_ref.shape[1]
    step = d_ff // ff_chunks
    acc = h
    for c in range(ff_chunks):
        u = jnp.dot(hn, wup_ref[:, c * step:(c + 1) * step], preferred_element_type=F32)
        u = jnp.square(jnp.maximum(u, 0.0)).astype(BF16)
        acc = acc + jnp.dot(u, wdn_ref[c * step:(c + 1) * step, :], preferred_element_type=F32)
    o_ref[...] = acc


def _out_mlp(x, a, wo, g, wup, wdn):
    n, d = x.shape
    d_ff = wup.shape[1]
    tile = min(ROW_TILE, n)
    assert n % tile == 0
    row_spec = pl.BlockSpec((tile, d), lambda i: (i, 0))
    return pl.pallas_call(
        functools.partial(_out_mlp_kernel, ff_chunks=4),
        grid=(n // tile,),
        in_specs=[
            row_spec,
            row_spec,
            _const_spec((d, d)),
            _const_spec((1, d)),
            _const_spec((d, d_ff)),
            _const_spec((d_ff, d)),
        ],
        out_specs=row_spec,
        out_shape=jax.ShapeDtypeStruct((n, d), F32),
        compiler_params=pltpu.CompilerParams(
            dimension_semantics=("arbitrary",), vmem_limit_bytes=VMEM_LIMIT),
        name="out_proj_mlp",
    )(x, a, wo, g.reshape(1, d), wup, wdn)


def _rel_bias_kernel(tab_ref, o_ref, *, tk, delta):
    q = pl.program_id(1)
    n_pad = tab_ref.shape[-1]
    r = lax.broadcasted_iota(jnp.int32, (n_pad, tk), 0)
    j = lax.broadcasted_iota(jnp.int32, (n_pad, tk), 1)
    idx = jnp.clip(q + delta - j, -MAX_REL, MAX_REL) + MAX_REL
    onehot = jnp.where(r == idx, 1.0, 0.0).astype(BF16)
    acc = jnp.dot(tab_ref[0], onehot, preferred_element_type=F32)
    acc = acc + jnp.dot(tab_ref[1], onehot, preferred_element_type=F32)
    acc = acc + jnp.dot(tab_ref[2], onehot, preferred_element_type=F32)
    o_ref[...] = acc


def _rel_bias(table, tq, tk, delta):
    n_layers, h, n_rel = table.shape
    n_pad = -(-n_rel // LANES) * LANES
    t = jnp.pad(table, ((0, 0), (0, 0), (0, n_pad - n_rel)))
    hi = t.astype(BF16)
    mid = (t - hi.astype(F32)).astype(BF16)
    lo = (t - hi.astype(F32) - mid.astype(F32)).astype(BF16)
    parts = jnp.stack([hi, mid, lo], axis=1)
    return pl.pallas_call(
        functools.partial(_rel_bias_kernel, tk=tk, delta=delta),
        grid=(n_layers, tq),
        in_specs=[pl.BlockSpec((None, 3, h, n_pad), lambda l, q: (l, 0, 0, 0))],
        out_specs=pl.BlockSpec((None, None, h, tk), lambda l, q: (l, q, 0, 0)),
        out_shape=jax.ShapeDtypeStruct((n_layers, tq, h, tk), F32),
        compiler_params=pltpu.CompilerParams(dimension_semantics=("arbitrary", "arbitrary")),
        name="rel_bias_table",
    )(parts)


def _stack_heads(per_head, tq):
    return per_head.reshape(N_KV_HEADS, GROUP * tq, per_head.shape[-1])


def _np_band_mask(q_pos, k_pos, left_chunks):
    qc = q_pos[:, None] // CHUNK
    kc = k_pos[None, :] // CHUNK
    return (k_pos[None, :] >= 0) & (kc <= qc) & (kc >= qc - left_chunks)


def _alibi_bias(q_pos, k_pos, static_mask):
    slopes = (2.0 ** (-8.0 * np.arange(1, N_HEADS + 1, dtype=np.float32) / N_HEADS)).astype(np.float32)
    dist = np.abs(q_pos[:, None] - k_pos[None, :]).astype(np.float32)
    bias = -slopes[:, None, None] * dist[None] * np.float32(LOG2E)
    if static_mask:
        bias = np.where(_np_band_mask(q_pos, k_pos, LEFT_CHUNKS_A)[None], bias, np.float32(NEG))
    return jnp.asarray(bias.astype(np.float32))


def _stack_sink(sink, tq):
    return jnp.repeat(sink.astype(F32) * LOG2E, tq).reshape(N_KV_HEADS, GROUP * tq, 1)


def kernel(x_prompt, x_sample, cache_k_a, cache_v_a, cache_k_b, cache_v_b, g_attn, g_mlp, w_qkv_a,
           g_q_a, g_k_a, sink_a, w_o_a, g_kv, w_kv, g_k_b, w_q_b, g_q_b, rel_bias_b, w_o_b, w_up, w_down):
    batch, seq, d = x_prompt.shape
    dec_batch, t_new, _ = x_sample.shape
    n_layers_a = w_qkv_a.shape[0]
    n_layers_b = w_q_b.shape[0]
    len_a = cache_k_a.shape[2]
    len_b = cache_k_b.shape[1]
    keep_a = min(LEFT_CHUNKS_A * CHUNK, seq)
    keep_b = min(LEFT_CHUNKS_B * CHUNK, seq)
    dq = N_HEADS * HEAD_DIM
    q_scale = HEAD_DIM ** -0.5 * LOG2E
    perm = _group_major_perm()

    hp = x_prompt.reshape(batch * seq, d)
    hs = x_sample.reshape(dec_batch * t_new, d)

    pos_s = PAST_LEN + np.arange(t_new)
    kpos_a = np.concatenate([PAST_LEN - len_a + np.arange(len_a), pos_s])
    kpos_b = np.concatenate([PAST_LEN - len_b + np.arange(len_b), pos_s])
    chunk_q = np.arange(CHUNK)
    bias_a_prompt = _pair_bias(_alibi_bias(chunk_q, np.arange(-LEFT_CHUNKS_A * CHUNK, CHUNK), static_mask=False))
    bias_a_sample = _stack_heads(_alibi_bias(pos_s, kpos_a, static_mask=True), t_new)

    ones_kv = jnp.ones((KV_WIDTH,), F32)
    seg_a_p = [(0, dq, True), (dq, KV_WIDTH, True)]
    seg_a_s = seg_a_p + [(dq + KV_WIDTH, KV_WIDTH, False)]
    seg_kv = [(0, KV_WIDTH, True), (KV_WIDTH, KV_WIDTH, False)]
    last_rows = lambda h, keep: h.reshape(batch, seq, d)[:, seq - keep:].reshape(batch * keep, d)
    kv4 = lambda a, n, t: a.reshape(n, t, N_KV_HEADS, HEAD_DIM)

    ka_p, va_p, ka_s, va_s = [], [], [], []
    for i in range(n_layers_a):
        w = w_qkv_a[i].astype(BF16)
        w_p = jnp.concatenate([w[:, :dq][:, perm], w[:, dq:dq + KV_WIDTH]], axis=1)
        wvt = w[:, dq + KV_WIDTH:].T
        gains = jnp.concatenate([jnp.tile(g_q_a[i] * q_scale, N_HEADS), jnp.tile(g_k_a[i], N_KV_HEADS), ones_kv])
        wo = w_o_a[i].astype(BF16)
        wup = w_up[i].astype(BF16)
        wdn = w_down[i].astype(BF16)

        qp, kp16, vtp = _project(hp, g_attn[i], w_p, gains[:dq + KV_WIDTH], seg_a_p, (BF16, BF16),
                                 wt=wvt, seq=seq)
        ap = _prompt_attention(qp.reshape(batch, seq, dq), kp16.reshape(batch, seq, KV_WIDTH), vtp,
                               bias_a_prompt, _pair_sink(sink_a[i]), LEFT_CHUNKS_A)
        kp, vp = _project(last_rows(hp, keep_a), g_attn[i], w[:, dq:], gains[dq:], seg_kv, (F32, F32))
        ka_p.append(kv4(kp, batch, keep_a))
        va_p.append(kv4(vp, batch, keep_a))

        qs, ks, vs = _project(hs, g_attn[i], w, gains, seg_a_s, (BF16, F32, F32))
        ks3 = ks.reshape(dec_batch, t_new, KV_WIDTH)
        vs3 = vs.reshape(dec_batch, t_new, KV_WIDTH)
        as_ = _sample_attention(qs.reshape(dec_batch, t_new, dq),
                                cache_k_a[i].reshape(dec_batch, len_a, KV_WIDTH), ks3,
                                cache_v_a[i].reshape(dec_batch, len_a, KV_WIDTH), vs3,
                                bias_a_sample, _stack_sink(sink_a[i], t_new))
        ka_s.append(ks3.reshape(dec_batch, t_new, N_KV_HEADS, HEAD_DIM))
        va_s.append(vs3.reshape(dec_batch, t_new, N_KV_HEADS, HEAD_DIM))

        hp = _out_mlp(hp, ap.reshape(batch * seq, dq), wo[perm, :], g_mlp[i], wup, wdn)
        hs = _out_mlp(hs, as_.reshape(dec_batch * t_new, dq), wo, g_mlp[i], wup, wdn)

    wkv = w_kv.astype(BF16)
    gains_kv = jnp.concatenate([jnp.tile(g_k_b, N_KV_HEADS), ones_kv])
    kb_p16, vtb_p = _project(hp, g_kv, wkv[:, :KV_WIDTH], gains_kv[:KV_WIDTH], seg_kv[:1], (BF16,),
                             wt=wkv[:, KV_WIDTH:].T, seq=seq)
    kb_p, vb_p = _project(last_rows(hp, keep_b), g_kv, wkv, gains_kv, seg_kv, (F32, F32))
    kb_s, vb_s = _project(hs, g_kv, wkv, gains_kv, seg_kv, (F32, F32))
    kb_p16 = kb_p16.reshape(batch, seq, KV_WIDTH)
    kb_s3 = kb_s.reshape(dec_batch, t_new, KV_WIDTH)
    vb_s3 = vb_s.reshape(dec_batch, t_new, KV_WIDTH)
    cache_kb = cache_k_b.reshape(dec_batch, len_b, KV_WIDTH)
    cache_vb = cache_v_b.reshape(dec_batch, len_b, KV_WIDTH)

    pad_b = LEFT_CHUNKS_B * CHUNK
    tk_b = pad_b + CHUNK
    assert int(pos_s[0] - kpos_b[0]) == pad_b and t_new <= CHUNK and len_b + t_new <= tk_b
    assert np.all(np.diff(kpos_b) == 1)
    rel = _rel_bias(rel_bias_b.astype(F32) * LOG2E, CHUNK, tk_b, pad_b)
    rel = rel.transpose(0, 2, 1, 3)
    mask_s = np.where(_np_band_mask(pos_s, kpos_b, LEFT_CHUNKS_B), 0.0, NEG).astype(np.float32)

    seg_q = [(0, dq, True)]
    for j in range(n_layers_b):
        layer = n_layers_a + j
        wq = w_q_b[j].astype(BF16)
        gains = jnp.tile(g_q_b[j] * q_scale, N_HEADS)
        wo = w_o_b[j].astype(BF16)
        wup = w_up[layer].astype(BF16)
        wdn = w_down[layer].astype(BF16)
        bias_p = _pair_bias(rel[j])
        bias_s = _stack_heads(rel[j][:, :t_new, :len_b + t_new] + mask_s[None], t_new)

        (qp,) = _project(hp, g_attn[layer], wq[:, perm], gains, seg_q, (BF16,))
        ap = _prompt_attention(qp.reshape(batch, seq, dq), kb_p16, vtb_p, bias_p, None, LEFT_CHUNKS_B)
        (qs,) = _project(hs, g_attn[layer], wq, gains, seg_q, (BF16,))
        as_ = _sample_attention(qs.reshape(dec_batch, t_new, dq), cache_kb, kb_s3, cache_vb, vb_s3, bias_s, None)

        hp = _out_mlp(hp, ap.reshape(batch * seq, dq), wo[perm, :], g_mlp[layer], wup, wdn)
        hs = _out_mlp(hs, as_.reshape(dec_batch * t_new, dq), wo, g_mlp[layer], wup, wdn)

    return (hp.reshape(batch, seq, d), hs.reshape(dec_batch, t_new, d),
            jnp.stack(ka_p), jnp.stack(va_p),
            kv4(kb_p, batch, keep_b), kv4(vb_p, batch, keep_b),
            jnp.stack(ka_s), jnp.stack(va_s),
            kv4(kb_s3, dec_batch, t_new), kv4(vb_s3, dec_batch, t_new))
```

```python
import functools
import math

import numpy as np
import jax
import jax.numpy as jnp
from jax import lax
from jax.experimental import pallas as pl
from jax.experimental.pallas import tpu as pltpu

CHUNK = 64
HEAD_DIM = 64
N_KV_HEADS = 4
GROUP = 4
N_HEADS = N_KV_HEADS * GROUP
LEFT_CHUNKS_A = 2
LEFT_CHUNKS_B = 8
MAX_REL = 128
N_REL = 2 * MAX_REL + 1
PAST_LEN = 1024
EPS = 1e-6
NEG = -1e30
LOG2E = math.log2(math.e)

LANES = 128
KV_WIDTH = N_KV_HEADS * HEAD_DIM
GROUP_WIDTH = GROUP * HEAD_DIM
PAIR = 2 * CHUNK
ONES_ROWS = 16
REL_BIAS_ROWS = 8
ROW_TILE = 512
VMEM_LIMIT = 56 * 1024 * 1024

BF16 = jnp.bfloat16
F32 = jnp.float32


def _const_spec(shape):
    zeros = (0,) * len(shape)
    return pl.BlockSpec(shape, lambda *_: zeros, pipeline_mode=pl.Buffered(1))


def _head_sum_matrix():
    idx = np.arange(GROUP_WIDTH) // HEAD_DIM
    return jnp.asarray(idx[:, None] == idx[None, :], dtype=BF16)


def _replicate_matrices():
    src = np.arange(KV_WIDTH)[:, None]
    dst = np.arange(GROUP_WIDTH)[None, :]
    rep = [(src == h * HEAD_DIM + dst % HEAD_DIM) for h in range(N_KV_HEADS)]
    return jnp.asarray(np.stack(rep), dtype=BF16)


def _group_major_perm():
    g, h, d = np.meshgrid(np.arange(GROUP), np.arange(N_KV_HEADS), np.arange(HEAD_DIM), indexing="ij")
    return ((h * GROUP + g) * HEAD_DIM + d).reshape(-1)


def _rms_rows(x):
    return x * lax.rsqrt(jnp.mean(x * x, axis=-1, keepdims=True) + EPS)


def _proj_kernel(*refs, segments, has_vt):
    if has_vt:
        x_ref, g_ref, w_ref, hsum_ref, gain_ref, wt_ref = refs[:6]
        out_refs = refs[6:]
    else:
        x_ref, g_ref, w_ref, hsum_ref, gain_ref = refs[:5]
        out_refs = refs[5:]
    xn = (_rms_rows(x_ref[...]) * g_ref[...]).astype(BF16)
    y = jnp.dot(xn, w_ref[...], preferred_element_type=F32)
    for (start, width, normed), o_ref in zip(segments, out_refs):
        for t in range(width // GROUP_WIDTH):
            lo = start + t * GROUP_WIDTH
            yt = y[:, lo:lo + GROUP_WIDTH]
            if normed:
                ss = jnp.dot((yt * yt).astype(BF16), hsum_ref[...], preferred_element_type=F32)
                yt = yt * lax.rsqrt(ss * (1.0 / HEAD_DIM) + EPS) * gain_ref[:, lo:lo + GROUP_WIDTH]
            o_ref[:, t * GROUP_WIDTH:(t + 1) * GROUP_WIDTH] = yt.astype(o_ref.dtype)
    if has_vt:
        vt = lax.dot_general(wt_ref[...], xn, (((1,), (1,)), ((), ())), preferred_element_type=F32)
        out_refs[-1][...] = vt.astype(out_refs[-1].dtype)


def _project(x, g, w, gains, segments, out_dtypes, wt=None, seq=None):
    n, d = x.shape
    c = w.shape[1]
    tile = min(ROW_TILE, n)
    assert n % tile == 0
    out_shape = [jax.ShapeDtypeStruct((n, width), dt) for (_, width, _), dt in zip(segments, out_dtypes)]
    out_specs = [pl.BlockSpec((tile, width), lambda i: (i, 0)) for (_, width, _) in segments]
    in_specs = [
        pl.BlockSpec((tile, d), lambda i: (i, 0)),
        _const_spec((1, d)),
        _const_spec((d, c)),
        _const_spec((GROUP_WIDTH, GROUP_WIDTH)),
        _const_spec((1, c)),
    ]
    args = [x, g.reshape(1, d), w, _head_sum_matrix(), gains.reshape(1, c)]
    if wt is not None:
        assert seq % tile == 0 and n % seq == 0
        per_seq = seq // tile
        in_specs.append(_const_spec(wt.shape))
        args.append(wt)
        out_shape.append(jax.ShapeDtypeStruct((n // seq, wt.shape[0], seq), BF16))
        out_specs.append(pl.BlockSpec((None, wt.shape[0], tile), lambda i: (i // per_seq, 0, i % per_seq)))
    return pl.pallas_call(
        functools.partial(_proj_kernel, segments=tuple(segments), has_vt=wt is not None),
        grid=(n // tile,),
        in_specs=in_specs,
        out_specs=out_specs,
        out_shape=out_shape,
        compiler_params=pltpu.CompilerParams(
            dimension_semantics=("arbitrary",), vmem_limit_bytes=VMEM_LIMIT),
        name="project",
    )(*args)


def _band_attn_kernel(*refs, prev, n_pairs, has_sink):
    if has_sink:
        q_ref, kp_ref, kc_ref, vtp_ref, vtc_ref, bias_ref, sink_ref, o_ref, kcat, vt3, s_buf0, s_buf1, ot_prev = refs
    else:
        q_ref, kp_ref, kc_ref, vtp_ref, vtc_ref, bias_ref, o_ref, kcat, vt3, s_buf0, s_buf1, ot_prev = refs
        sink_ref = None
    s_bufs = (s_buf0, s_buf1)
    tile = n_pairs * PAIR
    band = prev + PAIR
    n_cols = 2 * GROUP * CHUNK

    kcat[0:prev, :] = kp_ref[...]
    kcat[prev:prev + tile, :] = kc_ref[...]
    ones = jnp.ones((ONES_ROWS, LANES), BF16)
    blocks = [vtp_ref[:, b * LANES:(b + 1) * LANES] for b in range(prev // LANES)]
    blocks += [vtc_ref[:, b * LANES:(b + 1) * LANES] for b in range(tile // LANES)]
    for b, blk in enumerate(blocks):
        for h in range(N_KV_HEADS):
            vt3[b, h, 0:HEAD_DIM, :] = blk[h * HEAD_DIM:(h + 1) * HEAD_DIM, :]
            vt3[b, h, HEAD_DIM:HEAD_DIM + ONES_ROWS, :] = ones

    lane_head = lax.broadcasted_iota(jnp.int32, (CHUNK, KV_WIDTH), 1) // HEAD_DIM

    def scores(j, h, masked, s_buf):
        r0 = pl.multiple_of(j * PAIR, PAIR)
        qp = q_ref[pl.ds(r0, PAIR), :]
        kb = kcat[pl.ds(r0, band), :]
        qs = jnp.concatenate(
            [jnp.where(lane_head == h, qp[c * CHUNK:(c + 1) * CHUNK, g * KV_WIDTH:(g + 1) * KV_WIDTH], 0)
             for c in range(2) for g in range(GROUP)], axis=0).astype(BF16)
        s = lax.dot_general(kb, qs, (((1,), (1,)), ((), ())), preferred_element_type=F32)
        s = s + bias_ref[h]
        if masked:
            valid = lax.broadcasted_iota(jnp.int32, (band, n_cols), 0) >= prev - r0
            s = jnp.where(valid, s, NEG)
        s_buf[...] = s
        return jnp.max(s, axis=0, keepdims=True)

    def softmax_pv(m, s_buf, j, h):
        if has_sink:
            m = jnp.maximum(m, sink_ref[h])
        ot = jnp.zeros((HEAD_DIM + ONES_ROWS, n_cols), F32)
        for b in range(band // LANES):
            eb = jnp.exp2(s_buf[b * LANES:(b + 1) * LANES, :] - m).astype(BF16)
            ot = ot + jnp.dot(vt3[j + b, h], eb, preferred_element_type=F32)
        denom = ot[HEAD_DIM:HEAD_DIM + 1, :]
        if has_sink:
            denom = denom + jnp.exp2(sink_ref[h] - m)
        return ot[0:HEAD_DIM, :] * (1.0 / denom)

    def run(masked):
        m_first = scores(0, 0, masked, s_bufs[0])
        ot_prev[...] = jnp.zeros(ot_prev.shape, F32)

        def write_out(j):
            r0 = pl.multiple_of(j * PAIR, PAIR)
            for c in range(2):
                oc = ot_prev[:, c * GROUP_WIDTH:(c + 1) * GROUP_WIDTH].T
                for g in range(GROUP):
                    o_ref[pl.ds(r0 + c * CHUNK, CHUNK), g * KV_WIDTH:(g + 1) * KV_WIDTH] = (
                        oc[g * CHUNK:(g + 1) * CHUNK, :].astype(o_ref.dtype))

        def body(j, m):
            write_out(jnp.maximum(j - 1, 0))
            for h in range(N_KV_HEADS):
                if h + 1 < N_KV_HEADS:
                    m_ahead = scores(j, h + 1, masked, s_bufs[(h + 1) % 2])
                else:
                    m_ahead = scores(jnp.minimum(j + 1, n_pairs - 1), 0, masked, s_bufs[0])
                ot_prev[h * HEAD_DIM:(h + 1) * HEAD_DIM, :] = softmax_pv(m, s_bufs[h % 2], j, h)
                m = m_ahead
            return m

        lax.fori_loop(0, n_pairs, body, m_first)
        write_out(n_pairs - 1)

    @pl.when(pl.program_id(1) == 0)
    def _():
        run(True)

    @pl.when(pl.program_id(1) > 0)
    def _():
        run(False)


def _prompt_attention(q, k, vt, bias, sink, left_chunks):
    b, s, dq = q.shape
    prev = left_chunks * CHUNK
    tile = min(ROW_TILE, s)
    band = prev + PAIR
    assert s % tile == 0 and tile % prev == 0 and prev % LANES == 0 and tile % PAIR == 0
    assert bias.shape == (N_KV_HEADS, band, 2 * GROUP * CHUNK)
    ratio = tile // prev
    has_sink = sink is not None
    cur_rows = lambda bi, i: (bi, i, 0)
    prev_rows = lambda bi, i: (bi, jnp.maximum(i * ratio - 1, 0), 0)
    cur_cols = lambda bi, i: (bi, 0, i)
    prev_cols = lambda bi, i: (bi, 0, jnp.maximum(i * ratio - 1, 0))
    in_specs = [
        pl.BlockSpec((None, tile, dq), cur_rows),
        pl.BlockSpec((None, prev, KV_WIDTH), prev_rows),
        pl.BlockSpec((None, tile, KV_WIDTH), cur_rows),
        pl.BlockSpec((None, KV_WIDTH, prev), prev_cols),
        pl.BlockSpec((None, KV_WIDTH, tile), cur_cols),
        _const_spec(bias.shape),
    ]
    args = [q, k, k, vt, vt, bias]
    if has_sink:
        in_specs.append(_const_spec(sink.shape))
        args.append(sink)
    return pl.pallas_call(
        functools.partial(_band_attn_kernel, prev=prev, n_pairs=tile // PAIR, has_sink=has_sink),
        grid=(b, s // tile),
        in_specs=in_specs,
        out_specs=pl.BlockSpec((None, tile, dq), cur_rows),
        out_shape=jax.ShapeDtypeStruct((b, s, dq), BF16),
        scratch_shapes=[
            pltpu.VMEM((prev + tile, KV_WIDTH), BF16),
            pltpu.VMEM(((prev + tile) // LANES, N_KV_HEADS, HEAD_DIM + ONES_ROWS, LANES), BF16),
            pltpu.VMEM((band, 2 * GROUP * CHUNK), F32),
            pltpu.VMEM((band, 2 * GROUP * CHUNK), F32),
            pltpu.VMEM((N_KV_HEADS * HEAD_DIM, 2 * GROUP * CHUNK), F32),
        ],
        compiler_params=pltpu.CompilerParams(
            dimension_semantics=("arbitrary", "arbitrary"), vmem_limit_bytes=VMEM_LIMIT),
        name="band_attention",
    )(*args)


def _pair_bias(per_head):
    h, tq, tk = per_head.shape
    neg = jnp.full((h, tq, CHUNK), NEG, F32)
    both = jnp.stack([jnp.concatenate([per_head, neg], axis=-1),
                      jnp.concatenate([neg, per_head], axis=-1)], axis=1)
    both = both.reshape(N_KV_HEADS, GROUP, 2, tq, tk + CHUNK)
    return both.transpose(0, 4, 2, 1, 3).reshape(N_KV_HEADS, tk + CHUNK, 2 * GROUP * tq)


def _pair_sink(sink):
    s = (sink.astype(F32) * LOG2E).reshape(N_KV_HEADS, 1, GROUP, 1)
    return jnp.broadcast_to(s, (N_KV_HEADS, 2, GROUP, CHUNK)).reshape(N_KV_HEADS, 1, 2 * GROUP * CHUNK)


def _sample_attn_kernel(*refs, tq, prev, has_sink):
    if has_sink:
        q_ref, kp_ref, kc_ref, vp_ref, vc_ref, bias_ref, rep_ref, sink_ref, o_ref, krep, vrep = refs
    else:
        q_ref, kp_ref, kc_ref, vp_ref, vc_ref, bias_ref, rep_ref, o_ref, krep, vrep = refs
        sink_ref = None

    def replicate(dst, src_ref, off, rows):
        xb = src_ref[...].astype(BF16)
        for h in range(N_KV_HEADS):
            dst[off:off + rows, h * GROUP_WIDTH:(h + 1) * GROUP_WIDTH] = jnp.dot(
                xb, rep_ref[h], preferred_element_type=F32).astype(BF16)

    lane_group = lax.broadcasted_iota(jnp.int32, (tq, GROUP_WIDTH), 1) // HEAD_DIM
    in_group = [lane_group == g for g in range(GROUP)]
    qc = q_ref[...]

    replicate(krep, kp_ref, 0, prev)
    replicate(krep, kc_ref, prev, tq)
    scores = []
    for h in range(N_KV_HEADS):
        qg = qc[:, h * GROUP_WIDTH:(h + 1) * GROUP_WIDTH]
        qs = jnp.concatenate(
            [jnp.where(in_group[g], qg, jnp.zeros_like(qg)) for g in range(GROUP)], axis=0)
        kr = krep[:, h * GROUP_WIDTH:(h + 1) * GROUP_WIDTH]
        s = lax.dot_general(qs, kr, (((1,), (1,)), ((), ())), preferred_element_type=F32)
        scores.append(s + bias_ref[h])
    replicate(vrep, vp_ref, 0, prev)
    replicate(vrep, vc_ref, prev, tq)
    probs = []
    for h in range(N_KV_HEADS):
        s = scores[h]
        m = jnp.max(s, axis=-1, keepdims=True)
        if has_sink:
            m = jnp.maximum(m, sink_ref[h])
        e = jnp.exp2(s - m)
        denom = jnp.sum(e, axis=-1, keepdims=True)
        if has_sink:
            denom = denom + jnp.exp2(sink_ref[h] - m)
        probs.append((e * (1.0 / denom)).astype(BF16))
    for h in range(N_KV_HEADS):
        vr = vrep[:, h * GROUP_WIDTH:(h + 1) * GROUP_WIDTH]
        o = jnp.zeros((tq, GROUP_WIDTH), F32)
        for g in range(GROUP):
            og = jnp.dot(probs[h][g * tq:(g + 1) * tq], vr, preferred_element_type=F32)
            o = jnp.where(in_group[g], og, o)
        o_ref[:, h * GROUP_WIDTH:(h + 1) * GROUP_WIDTH] = o.astype(o_ref.dtype)


def _sample_attention(q, k_cache, k_new, v_cache, v_new, bias, sink):
    b, t, dq = q.shape
    prev = k_cache.shape[1]
    assert bias.shape == (N_KV_HEADS, GROUP * t, prev + t)
    has_sink = sink is not None
    row_map = lambda bi: (bi, 0, 0)
    in_specs = [
        pl.BlockSpec((None, t, dq), row_map),
        pl.BlockSpec((None, prev, KV_WIDTH), row_map),
        pl.BlockSpec((None, t, KV_WIDTH), row_map),
        pl.BlockSpec((None, prev, KV_WIDTH), row_map),
        pl.BlockSpec((None, t, KV_WIDTH), row_map),
        _const_spec(bias.shape),
        _const_spec((N_KV_HEADS, KV_WIDTH, GROUP_WIDTH)),
    ]
    args = [q, k_cache, k_new, v_cache, v_new, bias, _replicate_matrices()]
    if has_sink:
        in_specs.append(_const_spec(sink.shape))
        args.append(sink)
    return pl.pallas_call(
        functools.partial(_sample_attn_kernel, tq=t, prev=prev, has_sink=has_sink),
        grid=(b,),
        in_specs=in_specs,
        out_specs=pl.BlockSpec((None, t, dq), row_map),
        out_shape=jax.ShapeDtypeStruct((b, t, dq), BF16),
        scratch_shapes=[
            pltpu.VMEM((prev + t, N_KV_HEADS * GROUP_WIDTH), BF16),
            pltpu.VMEM((prev + t, N_KV_HEADS * GROUP_WIDTH), BF16),
        ],
        compiler_params=pltpu.CompilerParams(
            dimension_semantics=("arbitrary",), vmem_limit_bytes=VMEM_LIMIT),
        name="sample_attention",
    )(*args)


def _out_mlp_kernel(x_ref, a_ref, wo_ref, g_ref, wup_ref, wdn_ref, o_ref, *, ff_chunks):
    h = x_ref[...] + jnp.dot(a_ref[...], wo_ref[...], preferred_element_type=F32)
    hn = (_rms_rows(h) * g_ref[...]).astype(BF16)
    d_ff = wup_ref.shape[1]
    step = d_ff // ff_chunks
    acc = h
    for c in range(ff_chunks):
        u = jnp.dot(hn, wup_ref[:, c * step:(c + 1) * step], preferred_element_type=F32)
        u = jnp.square(jnp.maximum(u, 0.0)).astype(BF16)
        acc = acc + jnp.dot(u, wdn_ref[c * step:(c + 1) * step, :], preferred_element_type=F32)
    o_ref[...] = acc


def _out_mlp(x, a, wo, g, wup, wdn):
    n, d = x.shape
    d_ff = wup.shape[1]
    tile = min(ROW_TILE, n)
    assert n % tile == 0
    row_spec = pl.BlockSpec((tile, d), lambda i: (i, 0))
    return pl.pallas_call(
        functools.partial(_out_mlp_kernel, ff_chunks=4),
        grid=(n // tile,),
        in_specs=[
            row_spec,
            row_spec,
            _const_spec((d, d)),
            _const_spec((1, d)),
            _const_spec((d, d_ff)),
            _const_spec((d_ff, d)),
        ],
        out_specs=row_spec,
        out_shape=jax.ShapeDtypeStruct((n, d), F32),
        compiler_params=pltpu.CompilerParams(
            dimension_semantics=("arbitrary",), vmem_limit_bytes=VMEM_LIMIT),
        name="out_proj_mlp",
    )(x, a, wo, g.reshape(1, d), wup, wdn)


def _rel_bias_kernel(tab_ref, o_ref, *, tk, delta):
    n_pad = tab_ref.shape[-1]
    r = lax.broadcasted_iota(jnp.int32, (n_pad, tk), 0)
    j = lax.broadcasted_iota(jnp.int32, (n_pad, tk), 1)
    for qi in range(o_ref.shape[0]):
        q = pl.program_id(1) * o_ref.shape[0] + qi
        idx = jnp.clip(q + delta - j, -MAX_REL, MAX_REL) + MAX_REL
        onehot = jnp.where(r == idx, 1.0, 0.0).astype(BF16)
        acc = jnp.dot(tab_ref[0], onehot, preferred_element_type=F32)
        acc = acc + jnp.dot(tab_ref[1], onehot, preferred_element_type=F32)
        acc = acc + jnp.dot(tab_ref[2], onehot, preferred_element_type=F32)
        o_ref[qi] = acc


def _rel_bias(table, tq, tk, delta):
    n_layers, h, n_rel = table.shape
    n_pad = -(-n_rel // LANES) * LANES
    t = jnp.pad(table, ((0, 0), (0, 0), (0, n_pad - n_rel)))
    hi = t.astype(BF16)
    mid = (t - hi.astype(F32)).astype(BF16)
    lo = (t - hi.astype(F32) - mid.astype(F32)).astype(BF16)
    parts = jnp.stack([hi, mid, lo], axis=1)
    return pl.pallas_call(
        functools.partial(_rel_bias_kernel, tk=tk, delta=delta),
        grid=(n_layers, tq // REL_BIAS_ROWS),
        in_specs=[pl.BlockSpec((None, 3, h, n_pad), lambda l, q: (l, 0, 0, 0))],
        out_specs=pl.BlockSpec((None, REL_BIAS_ROWS, h, tk), lambda l, q: (l, q, 0, 0)),
        out_shape=jax.ShapeDtypeStruct((n_layers, tq, h, tk), F32),
        compiler_params=pltpu.CompilerParams(dimension_semantics=("arbitrary", "arbitrary")),
        name="rel_bias_table",
    )(parts)


def _stack_heads(per_head, tq):
    return per_head.reshape(N_KV_HEADS, GROUP * tq, per_head.shape[-1])


def _np_band_mask(q_pos, k_pos, left_chunks):
    qc = q_pos[:, None] // CHUNK
    kc = k_pos[None, :] // CHUNK
    return (k_pos[None, :] >= 0) & (kc <= qc) & (kc >= qc - left_chunks)


def _alibi_bias(q_pos, k_pos, static_mask):
    slopes = (2.0 ** (-8.0 * np.arange(1, N_HEADS + 1, dtype=np.float32) / N_HEADS)).astype(np.float32)
    dist = np.abs(q_pos[:, None] - k_pos[None, :]).astype(np.float32)
    bias = -slopes[:, None, None] * dist[None] * np.float32(LOG2E)
    if static_mask:
        bias = np.where(_np_band_mask(q_pos, k_pos, LEFT_CHUNKS_A)[None], bias, np.float32(NEG))
    return jnp.asarray(bias.astype(np.float32))


def _stack_sink(sink, tq):
    return jnp.repeat(sink.astype(F32) * LOG2E, tq).reshape(N_KV_HEADS, GROUP * tq, 1)


def kernel(x_prompt, x_sample, cache_k_a, cache_v_a, cache_k_b, cache_v_b, g_attn, g_mlp, w_qkv_a,
           g_q_a, g_k_a, sink_a, w_o_a, g_kv, w_kv, g_k_b, w_q_b, g_q_b, rel_bias_b, w_o_b, w_up, w_down):
    batch, seq, d = x_prompt.shape
    dec_batch, t_new, _ = x_sample.shape
    n_layers_a = w_qkv_a.shape[0]
    n_layers_b = w_q_b.shape[0]
    len_a = cache_k_a.shape[2]
    len_b = cache_k_b.shape[1]
    keep_a = min(LEFT_CHUNKS_A * CHUNK, seq)
    keep_b = min(LEFT_CHUNKS_B * CHUNK, seq)
    dq = N_HEADS * HEAD_DIM
    q_scale = HEAD_DIM ** -0.5 * LOG2E
    perm = _group_major_perm()

    hp = x_prompt.reshape(batch * seq, d)
    hs = x_sample.reshape(dec_batch * t_new, d)

    pos_s = PAST_LEN + np.arange(t_new)
    kpos_a = np.concatenate([PAST_LEN - len_a + np.arange(len_a), pos_s])
    kpos_b = np.concatenate([PAST_LEN - len_b + np.arange(len_b), pos_s])
    chunk_q = np.arange(CHUNK)
    bias_a_prompt = _pair_bias(_alibi_bias(chunk_q, np.arange(-LEFT_CHUNKS_A * CHUNK, CHUNK), static_mask=False))
    bias_a_sample = _stack_heads(_alibi_bias(pos_s, kpos_a, static_mask=True), t_new)

    ones_kv = jnp.ones((KV_WIDTH,), F32)
    seg_a_p = [(0, dq, True), (dq, KV_WIDTH, True)]
    seg_a_s = seg_a_p + [(dq + KV_WIDTH, KV_WIDTH, False)]
    seg_kv = [(0, KV_WIDTH, True), (KV_WIDTH, KV_WIDTH, False)]
    last_rows = lambda h, keep: h.reshape(batch, seq, d)[:, seq - keep:].reshape(batch * keep, d)
    kv4 = lambda a, n, t: a.reshape(n, t, N_KV_HEADS, HEAD_DIM)

    ka_p, va_p, ka_s, va_s = [], [], [], []
    for i in range(n_layers_a):
        w = w_qkv_a[i].astype(BF16)
        w_p = jnp.concatenate([w[:, :dq][:, perm], w[:, dq:dq + KV_WIDTH]], axis=1)
        wvt = w[:, dq + KV_WIDTH:].T
        gains = jnp.concatenate([jnp.tile(g_q_a[i] * q_scale, N_HEADS), jnp.tile(g_k_a[i], N_KV_HEADS), ones_kv])
        wo = w_o_a[i].astype(BF16)
        wup = w_up[i].astype(BF16)
        wdn = w_down[i].astype(BF16)

        qp, kp16, vtp = _project(hp, g_attn[i], w_p, gains[:dq + KV_WIDTH], seg_a_p, (BF16, BF16),
                                 wt=wvt, seq=seq)
        ap = _prompt_attention(qp.reshape(batch, seq, dq), kp16.reshape(batch, seq, KV_WIDTH), vtp,
                               bias_a_prompt, _pair_sink(sink_a[i]), LEFT_CHUNKS_A)
        kp, vp = _project(last_rows(hp, keep_a), g_attn[i], w[:, dq:], gains[dq:], seg_kv, (F32, F32))
        ka_p.append(kv4(kp, batch, keep_a))
        va_p.append(kv4(vp, batch, keep_a))

        qs, ks, vs = _project(hs, g_attn[i], w, gains, seg_a_s, (BF16, F32, F32))
        ks3 = ks.reshape(dec_batch, t_new, KV_WIDTH)
        vs3 = vs.reshape(dec_batch, t_new, KV_WIDTH)
        as_ = _sample_attention(qs.reshape(dec_batch, t_new, dq),
                                cache_k_a[i].reshape(dec_batch, len_a, KV_WIDTH), ks3,
                                cache_v_a[i].reshape(dec_batch, len_a, KV_WIDTH), vs3,
                                bias_a_sample, _stack_sink(sink_a[i], t_new))
        ka_s.append(ks3.reshape(dec_batch, t_new, N_KV_HEADS, HEAD_DIM))
        va_s.append(vs3.reshape(dec_batch, t_new, N_KV_HEADS, HEAD_DIM))

        hp = _out_mlp(hp, ap.reshape(batch * seq, dq), wo[perm, :], g_mlp[i], wup, wdn)
        hs = _out_mlp(hs, as_.reshape(dec_batch * t_new, dq), wo, g_mlp[i], wup, wdn)

    wkv = w_kv.astype(BF16)
    gains_kv = jnp.concatenate([jnp.tile(g_k_b, N_KV_HEADS), ones_kv])
    kb_p16, vtb_p = _project(hp, g_kv, wkv[:, :KV_WIDTH], gains_kv[:KV_WIDTH], seg_kv[:1], (BF16,),
                             wt=wkv[:, KV_WIDTH:].T, seq=seq)
    kb_p, vb_p = _project(last_rows(hp, keep_b), g_kv, wkv, gains_kv, seg_kv, (F32, F32))
    kb_s, vb_s = _project(hs, g_kv, wkv, gains_kv, seg_kv, (F32, F32))
    kb_p16 = kb_p16.reshape(batch, seq, KV_WIDTH)
    kb_s3 = kb_s.reshape(dec_batch, t_new, KV_WIDTH)
    vb_s3 = vb_s.reshape(dec_batch, t_new, KV_WIDTH)
    cache_kb = cache_k_b.reshape(dec_batch, len_b, KV_WIDTH)
    cache_vb = cache_v_b.reshape(dec_batch, len_b, KV_WIDTH)

    pad_b = LEFT_CHUNKS_B * CHUNK
    tk_b = pad_b + CHUNK
    assert int(pos_s[0] - kpos_b[0]) == pad_b and t_new <= CHUNK and len_b + t_new <= tk_b
    assert np.all(np.diff(kpos_b) == 1)
    rel = _rel_bias(rel_bias_b.astype(F32) * LOG2E, CHUNK, tk_b, pad_b)
    rel = rel.transpose(0, 2, 1, 3)
    mask_s = np.where(_np_band_mask(pos_s, kpos_b, LEFT_CHUNKS_B), 0.0, NEG).astype(np.float32)

    seg_q = [(0, dq, True)]
    for j in range(n_layers_b):
        layer = n_layers_a + j
        wq = w_q_b[j].astype(BF16)
        gains = jnp.tile(g_q_b[j] * q_scale, N_HEADS)
        wo = w_o_b[j].astype(BF16)
        wup = w_up[layer].astype(BF16)
        wdn = w_down[layer].astype(BF16)
        bias_p = _pair_bias(rel[j])
        bias_s = _stack_heads(rel[j][:, :t_new, :len_b + t_new] + mask_s[None], t_new)

        (qp,) = _project(hp, g_attn[layer], wq[:, perm], gains, seg_q, (BF16,))
        ap = _prompt_attention(qp.reshape(batch, seq, dq), kb_p16, vtb_p, bias_p, None, LEFT_CHUNKS_B)
        (qs,) = _project(hs, g_attn[layer], wq, gains, seg_q, (BF16,))
        as_ = _sample_attention(qs.reshape(dec_batch, t_new, dq), cache_kb, kb_s3, cache_vb, vb_s3, bias_s, None)

        hp = _out_mlp(hp, ap.reshape(batch * seq, dq), wo[perm, :], g_mlp[layer], wup, wdn)
        hs = _out_mlp(hs, as_.reshape(dec_batch * t_new, dq), wo, g_mlp[layer], wup, wdn)

    return (hp.reshape(batch, seq, d), hs.reshape(dec_batch, t_new, d),
            jnp.stack(ka_p), jnp.stack(va_p),
            kv4(kb_p, batch, keep_b), kv4(vb_p, batch, keep_b),
            jnp.stack(ka_s), jnp.stack(va_s),
            kv4(kb_s3, dec_batch, t_new), kv4(vb_s3, dec_batch, t_new))
```

```python
import functools
import math

import numpy as np
import jax
import jax.numpy as jnp
from jax import lax
from jax.experimental import pallas as pl
from jax.experimental.pallas import tpu as pltpu

CHUNK = 64
HEAD_DIM = 64
N_KV_HEADS = 4
GROUP = 4
N_HEADS = N_KV_HEADS * GROUP
LEFT_CHUNKS_A = 2
LEFT_CHUNKS_B = 8
MAX_REL = 128
N_REL = 2 * MAX_REL + 1
PAST_LEN = 1024
EPS = 1e-6
NEG = -1e30
LOG2E = math.log2(math.e)

LANES = 128
KV_WIDTH = N_KV_HEADS * HEAD_DIM
GROUP_WIDTH = GROUP * HEAD_DIM
PAIR = 2 * CHUNK
ONES_ROWS = 16
REL_BIAS_ROWS = 8
ROW_TILE = 512
ATTN_TILE = 2048
VMEM_LIMIT = 56 * 1024 * 1024

BF16 = jnp.bfloat16
F32 = jnp.float32


def _const_spec(shape):
    zeros = (0,) * len(shape)
    return pl.BlockSpec(shape, lambda *_: zeros, pipeline_mode=pl.Buffered(1))


def _head_sum_matrix():
    idx = np.arange(GROUP_WIDTH) // HEAD_DIM
    return jnp.asarray(idx[:, None] == idx[None, :], dtype=BF16)


def _replicate_matrices():
    src = np.arange(KV_WIDTH)[:, None]
    dst = np.arange(GROUP_WIDTH)[None, :]
    rep = [(src == h * HEAD_DIM + dst % HEAD_DIM) for h in range(N_KV_HEADS)]
    return jnp.asarray(np.stack(rep), dtype=BF16)


def _group_major_perm():
    g, h, d = np.meshgrid(np.arange(GROUP), np.arange(N_KV_HEADS), np.arange(HEAD_DIM), indexing="ij")
    return ((h * GROUP + g) * HEAD_DIM + d).reshape(-1)


def _rms_rows(x):
    return x * lax.rsqrt(jnp.mean(x * x, axis=-1, keepdims=True) + EPS)


def _proj_kernel(*refs, segments, has_vt):
    if has_vt:
        x_ref, g_ref, w_ref, hsum_ref, gain_ref, wt_ref = refs[:6]
        out_refs = refs[6:]
    else:
        x_ref, g_ref, w_ref, hsum_ref, gain_ref = refs[:5]
        out_refs = refs[5:]
    xn = (_rms_rows(x_ref[...]) * g_ref[...]).astype(BF16)
    y = jnp.dot(xn, w_ref[...], preferred_element_type=F32)
    for (start, width, normed), o_ref in zip(segments, out_refs):
        for t in range(width // GROUP_WIDTH):
            lo = start + t * GROUP_WIDTH
            yt = y[:, lo:lo + GROUP_WIDTH]
            if normed:
                ss = jnp.dot((yt * yt).astype(BF16), hsum_ref[...], preferred_element_type=F32)
                yt = yt * lax.rsqrt(ss * (1.0 / HEAD_DIM) + EPS) * gain_ref[:, lo:lo + GROUP_WIDTH]
            o_ref[:, t * GROUP_WIDTH:(t + 1) * GROUP_WIDTH] = yt.astype(o_ref.dtype)
    if has_vt:
        vt = lax.dot_general(wt_ref[...], xn, (((1,), (1,)), ((), ())), preferred_element_type=F32)
        out_refs[-1][...] = vt.astype(out_refs[-1].dtype)


def _project(x, g, w, gains, segments, out_dtypes, wt=None, seq=None):
    n, d = x.shape
    c = w.shape[1]
    tile = min(ROW_TILE, n)
    assert n % tile == 0
    out_shape = [jax.ShapeDtypeStruct((n, width), dt) for (_, width, _), dt in zip(segments, out_dtypes)]
    out_specs = [pl.BlockSpec((tile, width), lambda i: (i, 0)) for (_, width, _) in segments]
    in_specs = [
        pl.BlockSpec((tile, d), lambda i: (i, 0)),
        _const_spec((1, d)),
        _const_spec((d, c)),
        _const_spec((GROUP_WIDTH, GROUP_WIDTH)),
        _const_spec((1, c)),
    ]
    args = [x, g.reshape(1, d), w, _head_sum_matrix(), gains.reshape(1, c)]
    if wt is not None:
        assert seq % tile == 0 and n % seq == 0
        per_seq = seq // tile
        in_specs.append(_const_spec(wt.shape))
        args.append(wt)
        out_shape.append(jax.ShapeDtypeStruct((n // seq, wt.shape[0], seq), BF16))
        out_specs.append(pl.BlockSpec((None, wt.shape[0], tile), lambda i: (i // per_seq, 0, i % per_seq)))
    return pl.pallas_call(
        functools.partial(_proj_kernel, segments=tuple(segments), has_vt=wt is not None),
        grid=(n // tile,),
        in_specs=in_specs,
        out_specs=out_specs,
        out_shape=out_shape,
        compiler_params=pltpu.CompilerParams(
            dimension_semantics=("arbitrary",), vmem_limit_bytes=VMEM_LIMIT),
        name="project",
    )(*args)


def _band_attn_kernel(*refs, prev, n_pairs, has_sink):
    if has_sink:
        q_ref, kp_ref, kc_ref, vtp_ref, vtc_ref, bias_ref, sink_ref, o_ref, kcat, vt3, s_buf0, s_buf1, ot_prev = refs
    else:
        q_ref, kp_ref, kc_ref, vtp_ref, vtc_ref, bias_ref, o_ref, kcat, vt3, s_buf0, s_buf1, ot_prev = refs
        sink_ref = None
    s_bufs = (s_buf0, s_buf1)
    tile = n_pairs * PAIR
    band = prev + PAIR
    n_cols = 2 * GROUP * CHUNK

    kcat[0:prev, :] = kp_ref[...]
    kcat[prev:prev + tile, :] = kc_ref[...]
    ones = jnp.ones((ONES_ROWS, LANES), BF16)
    blocks = [vtp_ref[:, b * LANES:(b + 1) * LANES] for b in range(prev // LANES)]
    blocks += [vtc_ref[:, b * LANES:(b + 1) * LANES] for b in range(tile // LANES)]
    for b, blk in enumerate(blocks):
        for h in range(N_KV_HEADS):
            vt3[b, h, 0:HEAD_DIM, :] = blk[h * HEAD_DIM:(h + 1) * HEAD_DIM, :]
            vt3[b, h, HEAD_DIM:HEAD_DIM + ONES_ROWS, :] = ones

    lane_head = lax.broadcasted_iota(jnp.int32, (CHUNK, KV_WIDTH), 1) // HEAD_DIM

    def scores(j, h, masked, s_buf):
        r0 = pl.multiple_of(j * PAIR, PAIR)
        qp = q_ref[pl.ds(r0, PAIR), :]
        kb = kcat[pl.ds(r0, band), :]
        qs = jnp.concatenate(
            [jnp.where(lane_head == h, qp[c * CHUNK:(c + 1) * CHUNK, g * KV_WIDTH:(g + 1) * KV_WIDTH], 0)
             for c in range(2) for g in range(GROUP)], axis=0).astype(BF16)
        s = lax.dot_general(kb, qs, (((1,), (1,)), ((), ())), preferred_element_type=F32)
        s = s + bias_ref[h]
        if masked:
            valid = lax.broadcasted_iota(jnp.int32, (band, n_cols), 0) >= prev - r0
            s = jnp.where(valid, s, NEG)
        s_buf[...] = s
        return jnp.max(s, axis=0, keepdims=True)

    def softmax_pv(m, s_buf, j, h):
        if has_sink:
            m = jnp.maximum(m, sink_ref[h])
        ot = jnp.zeros((HEAD_DIM + ONES_ROWS, n_cols), F32)
        for b in range(band // LANES):
            eb = jnp.exp2(s_buf[b * LANES:(b + 1) * LANES, :] - m).astype(BF16)
            ot = ot + jnp.dot(vt3[j + b, h], eb, preferred_element_type=F32)
        denom = ot[HEAD_DIM:HEAD_DIM + 1, :]
        if has_sink:
            denom = denom + jnp.exp2(sink_ref[h] - m)
        return ot[0:HEAD_DIM, :] * (1.0 / denom)

    def run(masked):
        m_first = scores(0, 0, masked, s_bufs[0])
        ot_prev[...] = jnp.zeros(ot_prev.shape, F32)

        def write_out(j):
            r0 = pl.multiple_of(j * PAIR, PAIR)
            for c in range(2):
                oc = ot_prev[:, c * GROUP_WIDTH:(c + 1) * GROUP_WIDTH].T
                for g in range(GROUP):
                    o_ref[pl.ds(r0 + c * CHUNK, CHUNK), g * KV_WIDTH:(g + 1) * KV_WIDTH] = (
                        oc[g * CHUNK:(g + 1) * CHUNK, :].astype(o_ref.dtype))

        def body(j, m):
            write_out(jnp.maximum(j - 1, 0))
            for h in range(N_KV_HEADS):
                if h + 1 < N_KV_HEADS:
                    m_ahead = scores(j, h + 1, masked, s_bufs[(h + 1) % 2])
                else:
                    m_ahead = scores(jnp.minimum(j + 1, n_pairs - 1), 0, masked, s_bufs[0])
                ot_prev[h * HEAD_DIM:(h + 1) * HEAD_DIM, :] = softmax_pv(m, s_bufs[h % 2], j, h)
                m = m_ahead
            return m

        lax.fori_loop(0, n_pairs, body, m_first)
        write_out(n_pairs - 1)

    @pl.when(pl.program_id(1) == 0)
    def _():
        run(True)

    @pl.when(pl.program_id(1) > 0)
    def _():
        run(False)


def _prompt_attention(q, k, vt, bias, sink, left_chunks):
    b, s, dq = q.shape
    prev = left_chunks * CHUNK
    tile = min(ATTN_TILE, s)
    band = prev + PAIR
    assert s % tile == 0 and tile % prev == 0 and prev % LANES == 0 and tile % PAIR == 0
    assert bias.shape == (N_KV_HEADS, band, 2 * GROUP * CHUNK)
    ratio = tile // prev
    has_sink = sink is not None
    cur_rows = lambda bi, i: (bi, i, 0)
    prev_rows = lambda bi, i: (bi, jnp.maximum(i * ratio - 1, 0), 0)
    cur_cols = lambda bi, i: (bi, 0, i)
    prev_cols = lambda bi, i: (bi, 0, jnp.maximum(i * ratio - 1, 0))
    in_specs = [
        pl.BlockSpec((None, tile, dq), cur_rows),
        pl.BlockSpec((None, prev, KV_WIDTH), prev_rows),
        pl.BlockSpec((None, tile, KV_WIDTH), cur_rows),
        pl.BlockSpec((None, KV_WIDTH, prev), prev_cols),
        pl.BlockSpec((None, KV_WIDTH, tile), cur_cols),
        _const_spec(bias.shape),
    ]
    args = [q, k, k, vt, vt, bias]
    if has_sink:
        in_specs.append(_const_spec(sink.shape))
        args.append(sink)
    return pl.pallas_call(
        functools.partial(_band_attn_kernel, prev=prev, n_pairs=tile // PAIR, has_sink=has_sink),
        grid=(b, s // tile),
        in_specs=in_specs,
        out_specs=pl.BlockSpec((None, tile, dq), cur_rows),
        out_shape=jax.ShapeDtypeStruct((b, s, dq), BF16),
        scratch_shapes=[
            pltpu.VMEM((prev + tile, KV_WIDTH), BF16),
            pltpu.VMEM(((prev + tile) // LANES, N_KV_HEADS, HEAD_DIM + ONES_ROWS, LANES), BF16),
            pltpu.VMEM((band, 2 * GROUP * CHUNK), F32),
            pltpu.VMEM((band, 2 * GROUP * CHUNK), F32),
            pltpu.VMEM((N_KV_HEADS * HEAD_DIM, 2 * GROUP * CHUNK), F32),
        ],
        compiler_params=pltpu.CompilerParams(
            dimension_semantics=("arbitrary", "arbitrary"), vmem_limit_bytes=VMEM_LIMIT),
        name="band_attention",
    )(*args)


def _pair_bias(per_head):
    h, tq, tk = per_head.shape
    neg = jnp.full((h, tq, CHUNK), NEG, F32)
    both = jnp.stack([jnp.concatenate([per_head, neg], axis=-1),
                      jnp.concatenate([neg, per_head], axis=-1)], axis=1)
    both = both.reshape(N_KV_HEADS, GROUP, 2, tq, tk + CHUNK)
    return both.transpose(0, 4, 2, 1, 3).reshape(N_KV_HEADS, tk + CHUNK, 2 * GROUP * tq)


def _pair_sink(sink):
    s = (sink.astype(F32) * LOG2E).reshape(N_KV_HEADS, 1, GROUP, 1)
    return jnp.broadcast_to(s, (N_KV_HEADS, 2, GROUP, CHUNK)).reshape(N_KV_HEADS, 1, 2 * GROUP * CHUNK)


def _sample_attn_kernel(*refs, tq, prev, has_sink):
    if has_sink:
        q_ref, kp_ref, kc_ref, vp_ref, vc_ref, bias_ref, rep_ref, sink_ref, o_ref, krep, vrep = refs
    else:
        q_ref, kp_ref, kc_ref, vp_ref, vc_ref, bias_ref, rep_ref, o_ref, krep, vrep = refs
        sink_ref = None

    def replicate(dst, src_ref, off, rows):
        xb = src_ref[...].astype(BF16)
        for h in range(N_KV_HEADS):
            dst[off:off + rows, h * GROUP_WIDTH:(h + 1) * GROUP_WIDTH] = jnp.dot(
                xb, rep_ref[h], preferred_element_type=F32).astype(BF16)

    lane_group = lax.broadcasted_iota(jnp.int32, (tq, GROUP_WIDTH), 1) // HEAD_DIM
    in_group = [lane_group == g for g in range(GROUP)]
    qc = q_ref[...]

    replicate(krep, kp_ref, 0, prev)
    replicate(krep, kc_ref, prev, tq)
    scores = []
    for h in range(N_KV_HEADS):
        qg = qc[:, h * GROUP_WIDTH:(h + 1) * GROUP_WIDTH]
        qs = jnp.concatenate(
            [jnp.where(in_group[g], qg, jnp.zeros_like(qg)) for g in range(GROUP)], axis=0)
        kr = krep[:, h * GROUP_WIDTH:(h + 1) * GROUP_WIDTH]
        s = lax.dot_general(qs, kr, (((1,), (1,)), ((), ())), preferred_element_type=F32)
        scores.append(s + bias_ref[h])
    replicate(vrep, vp_ref, 0, prev)
    replicate(vrep, vc_ref, prev, tq)
    probs = []
    for h in range(N_KV_HEADS):
        s = scores[h]
        m = jnp.max(s, axis=-1, keepdims=True)
        if has_sink:
            m = jnp.maximum(m, sink_ref[h])
        e = jnp.exp2(s - m)
        denom = jnp.sum(e, axis=-1, keepdims=True)
        if has_sink:
            denom = denom + jnp.exp2(sink_ref[h] - m)
        probs.append((e * (1.0 / denom)).astype(BF16))
    for h in range(N_KV_HEADS):
        vr = vrep[:, h * GROUP_WIDTH:(h + 1) * GROUP_WIDTH]
        o = jnp.zeros((tq, GROUP_WIDTH), F32)
        for g in range(GROUP):
            og = jnp.dot(probs[h][g * tq:(g + 1) * tq], vr, preferred_element_type=F32)
            o = jnp.where(in_group[g], og, o)
        o_ref[:, h * GROUP_WIDTH:(h + 1) * GROUP_WIDTH] = o.astype(o_ref.dtype)


def _sample_attention(q, k_cache, k_new, v_cache, v_new, bias, sink):
    b, t, dq = q.shape
    prev = k_cache.shape[1]
    assert bias.shape == (N_KV_HEADS, GROUP * t, prev + t)
    has_sink = sink is not None
    row_map = lambda bi: (bi, 0, 0)
    in_specs = [
        pl.BlockSpec((None, t, dq), row_map),
        pl.BlockSpec((None, prev, KV_WIDTH), row_map),
        pl.BlockSpec((None, t, KV_WIDTH), row_map),
        pl.BlockSpec((None, prev, KV_WIDTH), row_map),
        pl.BlockSpec((None, t, KV_WIDTH), row_map),
        _const_spec(bias.shape),
        _const_spec((N_KV_HEADS, KV_WIDTH, GROUP_WIDTH)),
    ]
    args = [q, k_cache, k_new, v_cache, v_new, bias, _replicate_matrices()]
    if has_sink:
        in_specs.append(_const_spec(sink.shape))
        args.append(sink)
    return pl.pallas_call(
        functools.partial(_sample_attn_kernel, tq=t, prev=prev, has_sink=has_sink),
        grid=(b,),
        in_specs=in_specs,
        out_specs=pl.BlockSpec((None, t, dq), row_map),
        out_shape=jax.ShapeDtypeStruct((b, t, dq), BF16),
        scratch_shapes=[
            pltpu.VMEM((prev + t, N_KV_HEADS * GROUP_WIDTH), BF16),
            pltpu.VMEM((prev + t, N_KV_HEADS * GROUP_WIDTH), BF16),
        ],
        compiler_params=pltpu.CompilerParams(
            dimension_semantics=("arbitrary",), vmem_limit_bytes=VMEM_LIMIT),
        name="sample_attention",
    )(*args)


def _out_mlp_kernel(x_ref, a_ref, wo_ref, g_ref, wup_ref, wdn_ref, o_ref, *, ff_chunks):
    h = x_ref[...] + jnp.dot(a_ref[...], wo_ref[...], preferred_element_type=F32)
    hn = (_rms_rows(h) * g_ref[...]).astype(BF16)
    d_ff = wup_ref.shape[1]
    step = d_ff // ff_chunks
    acc = h
    for c in range(ff_chunks):
        u = jnp.dot(hn, wup_ref[:, c * step:(c + 1) * step], preferred_element_type=F32)
        u = jnp.square(jnp.maximum(u, 0.0)).astype(BF16)
        acc = acc + jnp.dot(u, wdn_ref[c * step:(c + 1) * step, :], preferred_element_type=F32)
    o_ref[...] = acc


def _out_mlp(x, a, wo, g, wup, wdn):
    n, d = x.shape
    d_ff = wup.shape[1]
    tile = min(ROW_TILE, n)
    assert n % tile == 0
    row_spec = pl.BlockSpec((tile, d), lambda i: (i, 0))
    return pl.pallas_call(
        functools.partial(_out_mlp_kernel, ff_chunks=4),
        grid=(n // tile,),
        in_specs=[
            row_spec,
            row_spec,
            _const_spec((d, d)),
            _const_spec((1, d)),
            _const_spec((d, d_ff)),
            _const_spec((d_ff, d)),
        ],
        out_specs=row_spec,
        out_shape=jax.ShapeDtypeStruct((n, d), F32),
        compiler_params=pltpu.CompilerParams(
            dimension_semantics=("arbitrary",), vmem_limit_bytes=VMEM_LIMIT),
        name="out_proj_mlp",
    )(x, a, wo, g.reshape(1, d), wup, wdn)


def _rel_bias_kernel(tab_ref, o_ref, *, tk, delta):
    n_pad = tab_ref.shape[-1]
    r = lax.broadcasted_iota(jnp.int32, (n_pad, tk), 0)
    j = lax.broadcasted_iota(jnp.int32, (n_pad, tk), 1)
    for qi in range(o_ref.shape[0]):
        q = pl.program_id(1) * o_ref.shape[0] + qi
        idx = jnp.clip(q + delta - j, -MAX_REL, MAX_REL) + MAX_REL
        onehot = jnp.where(r == idx, 1.0, 0.0).astype(BF16)
        acc = jnp.dot(tab_ref[0], onehot, preferred_element_type=F32)
        acc = acc + jnp.dot(tab_ref[1], onehot, preferred_element_type=F32)
        acc = acc + jnp.dot(tab_ref[2], onehot, preferred_element_type=F32)
        o_ref[qi] = acc


def _rel_bias(table, tq, tk, delta):
    n_layers, h, n_rel = table.shape
    n_pad = -(-n_rel // LANES) * LANES
    t = jnp.pad(table, ((0, 0), (0, 0), (0, n_pad - n_rel)))
    hi = t.astype(BF16)
    mid = (t - hi.astype(F32)).astype(BF16)
    lo = (t - hi.astype(F32) - mid.astype(F32)).astype(BF16)
    parts = jnp.stack([hi, mid, lo], axis=1)
    return pl.pallas_call(
        functools.partial(_rel_bias_kernel, tk=tk, delta=delta),
        grid=(n_layers, tq // REL_BIAS_ROWS),
        in_specs=[pl.BlockSpec((None, 3, h, n_pad), lambda l, q: (l, 0, 0, 0))],
        out_specs=pl.BlockSpec((None, REL_BIAS_ROWS, h, tk), lambda l, q: (l, q, 0, 0)),
        out_shape=jax.ShapeDtypeStruct((n_layers, tq, h, tk), F32),
        compiler_params=pltpu.CompilerParams(dimension_semantics=("arbitrary", "arbitrary")),
        name="rel_bias_table",
    )(parts)


def _stack_heads(per_head, tq):
    return per_head.reshape(N_KV_HEADS, GROUP * tq, per_head.shape[-1])


def _np_band_mask(q_pos, k_pos, left_chunks):
    qc = q_pos[:, None] // CHUNK
    kc = k_pos[None, :] // CHUNK
    return (k_pos[None, :] >= 0) & (kc <= qc) & (kc >= qc - left_chunks)


def _alibi_bias(q_pos, k_pos, static_mask):
    slopes = (2.0 ** (-8.0 * np.arange(1, N_HEADS + 1, dtype=np.float32) / N_HEADS)).astype(np.float32)
    dist = np.abs(q_pos[:, None] - k_pos[None, :]).astype(np.float32)
    bias = -slopes[:, None, None] * dist[None] * np.float32(LOG2E)
    if static_mask:
        bias = np.where(_np_band_mask(q_pos, k_pos, LEFT_CHUNKS_A)[None], bias, np.float32(NEG))
    return jnp.asarray(bias.astype(np.float32))


def _stack_sink(sink, tq):
    return jnp.repeat(sink.astype(F32) * LOG2E, tq).reshape(N_KV_HEADS, GROUP * tq, 1)


def kernel(x_prompt, x_sample, cache_k_a, cache_v_a, cache_k_b, cache_v_b, g_attn, g_mlp, w_qkv_a,
           g_q_a, g_k_a, sink_a, w_o_a, g_kv, w_kv, g_k_b, w_q_b, g_q_b, rel_bias_b, w_o_b, w_up, w_down):
    batch, seq, d = x_prompt.shape
    dec_batch, t_new, _ = x_sample.shape
    n_layers_a = w_qkv_a.shape[0]
    n_layers_b = w_q_b.shape[0]
    len_a = cache_k_a.shape[2]
    len_b = cache_k_b.shape[1]
    keep_a = min(LEFT_CHUNKS_A * CHUNK, seq)
    keep_b = min(LEFT_CHUNKS_B * CHUNK, seq)
    dq = N_HEADS * HEAD_DIM
    q_scale = HEAD_DIM ** -0.5 * LOG2E
    perm = _group_major_perm()

    hp = x_prompt.reshape(batch * seq, d)
    hs = x_sample.reshape(dec_batch * t_new, d)

    pos_s = PAST_LEN + np.arange(t_new)
    kpos_a = np.concatenate([PAST_LEN - len_a + np.arange(len_a), pos_s])
    kpos_b = np.concatenate([PAST_LEN - len_b + np.arange(len_b), pos_s])
    chunk_q = np.arange(CHUNK)
    bias_a_prompt = _pair_bias(_alibi_bias(chunk_q, np.arange(-LEFT_CHUNKS_A * CHUNK, CHUNK), static_mask=False))
    bias_a_sample = _stack_heads(_alibi_bias(pos_s, kpos_a, static_mask=True), t_new)

    ones_kv = jnp.ones((KV_WIDTH,), F32)
    seg_a_p = [(0, dq, True), (dq, KV_WIDTH, True)]
    seg_a_s = seg_a_p + [(dq + KV_WIDTH, KV_WIDTH, False)]
    seg_kv = [(0, KV_WIDTH, True), (KV_WIDTH, KV_WIDTH, False)]
    last_rows = lambda h, keep: h.reshape(batch, seq, d)[:, seq - keep:].reshape(batch * keep, d)
    kv4 = lambda a, n, t: a.reshape(n, t, N_KV_HEADS, HEAD_DIM)

    ka_p, va_p, ka_s, va_s = [], [], [], []
    for i in range(n_layers_a):
        w = w_qkv_a[i].astype(BF16)
        w_p = jnp.concatenate([w[:, :dq][:, perm], w[:, dq:dq + KV_WIDTH]], axis=1)
        wvt = w[:, dq + KV_WIDTH:].T
        gains = jnp.concatenate([jnp.tile(g_q_a[i] * q_scale, N_HEADS), jnp.tile(g_k_a[i], N_KV_HEADS), ones_kv])
        wo = w_o_a[i].astype(BF16)
        wup = w_up[i].astype(BF16)
        wdn = w_down[i].astype(BF16)

        qp, kp16, vtp = _project(hp, g_attn[i], w_p, gains[:dq + KV_WIDTH], seg_a_p, (BF16, BF16),
                                 wt=wvt, seq=seq)
        ap = _prompt_attention(qp.reshape(batch, seq, dq), kp16.reshape(batch, seq, KV_WIDTH), vtp,
                               bias_a_prompt, _pair_sink(sink_a[i]), LEFT_CHUNKS_A)
        kp, vp = _project(last_rows(hp, keep_a), g_attn[i], w[:, dq:], gains[dq:], seg_kv, (F32, F32))
        ka_p.append(kv4(kp, batch, keep_a))
        va_p.append(kv4(vp, batch, keep_a))

        qs, ks, vs = _project(hs, g_attn[i], w, gains, seg_a_s, (BF16, F32, F32))
        ks3 = ks.reshape(dec_batch, t_new, KV_WIDTH)
        vs3 = vs.reshape(dec_batch, t_new, KV_WIDTH)
        as_ = _sample_attention(qs.reshape(dec_batch, t_new, dq),
                                cache_k_a[i].reshape(dec_batch, len_a, KV_WIDTH), ks3,
                                cache_v_a[i].reshape(dec_batch, len_a, KV_WIDTH), vs3,
                                bias_a_sample, _stack_sink(sink_a[i], t_new))
        ka_s.append(ks3.reshape(dec_batch, t_new, N_KV_HEADS, HEAD_DIM))
        va_s.append(vs3.reshape(dec_batch, t_new, N_KV_HEADS, HEAD_DIM))

        hp = _out_mlp(hp, ap.reshape(batch * seq, dq), wo[perm, :], g_mlp[i], wup, wdn)
        hs = _out_mlp(hs, as_.reshape(dec_batch * t_new, dq), wo, g_mlp[i], wup, wdn)

    wkv = w_kv.astype(BF16)
    gains_kv = jnp.concatenate([jnp.tile(g_k_b, N_KV_HEADS), ones_kv])
    kb_p16, vtb_p = _project(hp, g_kv, wkv[:, :KV_WIDTH], gains_kv[:KV_WIDTH], seg_kv[:1], (BF16,),
                             wt=wkv[:, KV_WIDTH:].T, seq=seq)
    kb_p, vb_p = _project(last_rows(hp, keep_b), g_kv, wkv, gains_kv, seg_kv, (F32, F32))
    kb_s, vb_s = _project(hs, g_kv, wkv, gains_kv, seg_kv, (F32, F32))
    kb_p16 = kb_p16.reshape(batch, seq, KV_WIDTH)
    kb_s3 = kb_s.reshape(dec_batch, t_new, KV_WIDTH)
    vb_s3 = vb_s.reshape(dec_batch, t_new, KV_WIDTH)
    cache_kb = cache_k_b.reshape(dec_batch, len_b, KV_WIDTH)
    cache_vb = cache_v_b.reshape(dec_batch, len_b, KV_WIDTH)

    pad_b = LEFT_CHUNKS_B * CHUNK
    tk_b = pad_b + CHUNK
    assert int(pos_s[0] - kpos_b[0]) == pad_b and t_new <= CHUNK and len_b + t_new <= tk_b
    assert np.all(np.diff(kpos_b) == 1)
    rel = _rel_bias(rel_bias_b.astype(F32) * LOG2E, CHUNK, tk_b, pad_b)
    rel = rel.transpose(0, 2, 1, 3)
    mask_s = np.where(_np_band_mask(pos_s, kpos_b, LEFT_CHUNKS_B), 0.0, NEG).astype(np.float32)

    seg_q = [(0, dq, True)]
    for j in range(n_layers_b):
        layer = n_layers_a + j
        wq = w_q_b[j].astype(BF16)
        gains = jnp.tile(g_q_b[j] * q_scale, N_HEADS)
        wo = w_o_b[j].astype(BF16)
        wup = w_up[layer].astype(BF16)
        wdn = w_down[layer].astype(BF16)
        bias_p = _pair_bias(rel[j])
        bias_s = _stack_heads(rel[j][:, :t_new, :len_b + t_new] + mask_s[None], t_new)

        (qp,) = _project(hp, g_attn[layer], wq[:, perm], gains, seg_q, (BF16,))
        ap = _prompt_attention(qp.reshape(batch, seq, dq), kb_p16, vtb_p, bias_p, None, LEFT_CHUNKS_B)
        (qs,) = _project(hs, g_attn[layer], wq, gains, seg_q, (BF16,))
        as_ = _sample_attention(qs.reshape(dec_batch, t_new, dq), cache_kb, kb_s3, cache_vb, vb_s3, bias_s, None)

        hp = _out_mlp(hp, ap.reshape(batch * seq, dq), wo[perm, :], g_mlp[layer], wup, wdn)
        hs = _out_mlp(hs, as_.reshape(dec_batch * t_new, dq), wo, g_mlp[layer], wup, wdn)

    return (hp.reshape(batch, seq, d), hs.reshape(dec_batch, t_new, d),
            jnp.stack(ka_p), jnp.stack(va_p),
            kv4(kb_p, batch, keep_b), kv4(vb_p, batch, keep_b),
            jnp.stack(ka_s), jnp.stack(va_s),
            kv4(kb_s3, dec_batch, t_new), kv4(vb_s3, dec_batch, t_new))
```

```python
import functools
import math

import numpy as np
import jax
import jax.numpy as jnp
from jax import lax
from jax.experimental import pallas as pl
from jax.experimental.pallas import tpu as pltpu

CHUNK = 64
HEAD_DIM = 64
N_KV_HEADS = 4
GROUP = 4
N_HEADS = N_KV_HEADS * GROUP
LEFT_CHUNKS_A = 2
LEFT_CHUNKS_B = 8
MAX_REL = 128
N_REL = 2 * MAX_REL + 1
PAST_LEN = 1024
EPS = 1e-6
NEG = -1e30
LOG2E = math.log2(math.e)

LANES = 128
KV_WIDTH = N_KV_HEADS * HEAD_DIM
GROUP_WIDTH = GROUP * HEAD_DIM
PAIR = 2 * CHUNK
ONES_ROWS = 16
REL_BIAS_ROWS = 8
ROW_TILE = 512
ATTN_TILE = 2048
PAIR_UNROLL = 4
VMEM_LIMIT = 56 * 1024 * 1024

BF16 = jnp.bfloat16
F32 = jnp.float32


def _const_spec(shape):
    zeros = (0,) * len(shape)
    return pl.BlockSpec(shape, lambda *_: zeros, pipeline_mode=pl.Buffered(1))


def _head_sum_matrix():
    idx = np.arange(GROUP_WIDTH) // HEAD_DIM
    return jnp.asarray(idx[:, None] == idx[None, :], dtype=BF16)


def _replicate_matrices():
    src = np.arange(KV_WIDTH)[:, None]
    dst = np.arange(GROUP_WIDTH)[None, :]
    rep = [(src == h * HEAD_DIM + dst % HEAD_DIM) for h in range(N_KV_HEADS)]
    return jnp.asarray(np.stack(rep), dtype=BF16)


def _group_major_perm():
    g, h, d = np.meshgrid(np.arange(GROUP), np.arange(N_KV_HEADS), np.arange(HEAD_DIM), indexing="ij")
    return ((h * GROUP + g) * HEAD_DIM + d).reshape(-1)


def _rms_rows(x):
    return x * lax.rsqrt(jnp.mean(x * x, axis=-1, keepdims=True) + EPS)


def _proj_kernel(*refs, segments, has_vt):
    if has_vt:
        x_ref, g_ref, w_ref, hsum_ref, gain_ref, wt_ref = refs[:6]
        out_refs = refs[6:]
    else:
        x_ref, g_ref, w_ref, hsum_ref, gain_ref = refs[:5]
        out_refs = refs[5:]
    xn = (_rms_rows(x_ref[...]) * g_ref[...]).astype(BF16)
    y = jnp.dot(xn, w_ref[...], preferred_element_type=F32)
    for (start, width, normed), o_ref in zip(segments, out_refs):
        for t in range(width // GROUP_WIDTH):
            lo = start + t * GROUP_WIDTH
            yt = y[:, lo:lo + GROUP_WIDTH]
            if normed:
                ss = jnp.dot((yt * yt).astype(BF16), hsum_ref[...], preferred_element_type=F32)
                yt = yt * lax.rsqrt(ss * (1.0 / HEAD_DIM) + EPS) * gain_ref[:, lo:lo + GROUP_WIDTH]
            o_ref[:, t * GROUP_WIDTH:(t + 1) * GROUP_WIDTH] = yt.astype(o_ref.dtype)
    if has_vt:
        vt = lax.dot_general(wt_ref[...], xn, (((1,), (1,)), ((), ())), preferred_element_type=F32)
        out_refs[-1][...] = vt.astype(out_refs[-1].dtype)


def _project(x, g, w, gains, segments, out_dtypes, wt=None, seq=None):
    n, d = x.shape
    c = w.shape[1]
    tile = min(ROW_TILE, n)
    assert n % tile == 0
    out_shape = [jax.ShapeDtypeStruct((n, width), dt) for (_, width, _), dt in zip(segments, out_dtypes)]
    out_specs = [pl.BlockSpec((tile, width), lambda i: (i, 0)) for (_, width, _) in segments]
    in_specs = [
        pl.BlockSpec((tile, d), lambda i: (i, 0)),
        _const_spec((1, d)),
        _const_spec((d, c)),
        _const_spec((GROUP_WIDTH, GROUP_WIDTH)),
        _const_spec((1, c)),
    ]
    args = [x, g.reshape(1, d), w, _head_sum_matrix(), gains.reshape(1, c)]
    if wt is not None:
        assert seq % tile == 0 and n % seq == 0
        per_seq = seq // tile
        in_specs.append(_const_spec(wt.shape))
        args.append(wt)
        out_shape.append(jax.ShapeDtypeStruct((n // seq, wt.shape[0], seq), BF16))
        out_specs.append(pl.BlockSpec((None, wt.shape[0], tile), lambda i: (i // per_seq, 0, i % per_seq)))
    return pl.pallas_call(
        functools.partial(_proj_kernel, segments=tuple(segments), has_vt=wt is not None),
        grid=(n // tile,),
        in_specs=in_specs,
        out_specs=out_specs,
        out_shape=out_shape,
        compiler_params=pltpu.CompilerParams(
            dimension_semantics=("arbitrary",), vmem_limit_bytes=VMEM_LIMIT),
        name="project",
    )(*args)


def _band_attn_kernel(*refs, prev, n_pairs, has_sink):
    if has_sink:
        q_ref, kp_ref, kc_ref, vtp_ref, vtc_ref, bias_ref, sink_ref, o_ref, kcat, vt3, s_buf0, s_buf1, ot_prev = refs
    else:
        q_ref, kp_ref, kc_ref, vtp_ref, vtc_ref, bias_ref, o_ref, kcat, vt3, s_buf0, s_buf1, ot_prev = refs
        sink_ref = None
    s_bufs = (s_buf0, s_buf1)
    tile = n_pairs * PAIR
    band = prev + PAIR
    n_cols = 2 * GROUP * CHUNK

    kcat[0:prev, :] = kp_ref[...]
    kcat[prev:prev + tile, :] = kc_ref[...]
    ones = jnp.ones((ONES_ROWS, LANES), BF16)
    blocks = [vtp_ref[:, b * LANES:(b + 1) * LANES] for b in range(prev // LANES)]
    blocks += [vtc_ref[:, b * LANES:(b + 1) * LANES] for b in range(tile // LANES)]
    for b, blk in enumerate(blocks):
        for h in range(N_KV_HEADS):
            vt3[b, h, 0:HEAD_DIM, :] = blk[h * HEAD_DIM:(h + 1) * HEAD_DIM, :]
            vt3[b, h, HEAD_DIM:HEAD_DIM + ONES_ROWS, :] = ones

    lane_head = lax.broadcasted_iota(jnp.int32, (CHUNK, KV_WIDTH), 1) // HEAD_DIM

    def scores(j, h, masked, s_buf):
        r0 = pl.multiple_of(j * PAIR, PAIR)
        qp = q_ref[pl.ds(r0, PAIR), :]
        kb = kcat[pl.ds(r0, band), :]
        qs = jnp.concatenate(
            [jnp.where(lane_head == h, qp[c * CHUNK:(c + 1) * CHUNK, g * KV_WIDTH:(g + 1) * KV_WIDTH], 0)
             for c in range(2) for g in range(GROUP)], axis=0).astype(BF16)
        s = lax.dot_general(kb, qs, (((1,), (1,)), ((), ())), preferred_element_type=F32)
        s = s + bias_ref[h]
        if masked:
            valid = lax.broadcasted_iota(jnp.int32, (band, n_cols), 0) >= prev - r0
            s = jnp.where(valid, s, NEG)
        s_buf[...] = s
        return jnp.max(s, axis=0, keepdims=True)

    def softmax_pv(m, s_buf, j, h):
        if has_sink:
            m = jnp.maximum(m, sink_ref[h])
        ot = jnp.zeros((HEAD_DIM + ONES_ROWS, n_cols), F32)
        n_blocks = band // LANES
        for b in range(0, n_blocks, 2):
            width = min(2, n_blocks - b) * LANES
            eb = jnp.exp2(s_buf[b * LANES:b * LANES + width, :] - m).astype(BF16)
            vt = jnp.concatenate([vt3[j + b + i, h] for i in range(width // LANES)], axis=1)
            ot = ot + jnp.dot(vt, eb, preferred_element_type=F32)
        denom = ot[HEAD_DIM:HEAD_DIM + 1, :]
        if has_sink:
            denom = denom + jnp.exp2(sink_ref[h] - m)
        return ot[0:HEAD_DIM, :] * (1.0 / denom)

    def run(masked):
        m_first = scores(0, 0, masked, s_bufs[0])
        ot_prev[...] = jnp.zeros(ot_prev.shape, F32)

        def write_out(j):
            r0 = pl.multiple_of(j * PAIR, PAIR)
            for c in range(2):
                oc = ot_prev[:, c * GROUP_WIDTH:(c + 1) * GROUP_WIDTH].T
                for g in range(GROUP):
                    o_ref[pl.ds(r0 + c * CHUNK, CHUNK), g * KV_WIDTH:(g + 1) * KV_WIDTH] = (
                        oc[g * CHUNK:(g + 1) * CHUNK, :].astype(o_ref.dtype))

        def body(j, m):
            write_out(jnp.maximum(j - 1, 0))
            for h in range(N_KV_HEADS):
                if h + 1 < N_KV_HEADS:
                    m_ahead = scores(j, h + 1, masked, s_bufs[(h + 1) % 2])
                else:
                    m_ahead = scores(jnp.minimum(j + 1, n_pairs - 1), 0, masked, s_bufs[0])
                ot_prev[h * HEAD_DIM:(h + 1) * HEAD_DIM, :] = softmax_pv(m, s_bufs[h % 2], j, h)
                m = m_ahead
            return m

        lax.fori_loop(0, n_pairs, body, m_first, unroll=PAIR_UNROLL)
        write_out(n_pairs - 1)

    @pl.when(pl.program_id(1) == 0)
    def _():
        run(True)

    @pl.when(pl.program_id(1) > 0)
    def _():
        run(False)


def _prompt_attention(q, k, vt, bias, sink, left_chunks):
    b, s, dq = q.shape
    prev = left_chunks * CHUNK
    tile = min(ATTN_TILE, s)
    band = prev + PAIR
    assert s % tile == 0 and tile % prev == 0 and prev % LANES == 0 and tile % PAIR == 0
    assert bias.shape == (N_KV_HEADS, band, 2 * GROUP * CHUNK)
    ratio = tile // prev
    has_sink = sink is not None
    cur_rows = lambda bi, i: (bi, i, 0)
    prev_rows = lambda bi, i: (bi, jnp.maximum(i * ratio - 1, 0), 0)
    cur_cols = lambda bi, i: (bi, 0, i)
    prev_cols = lambda bi, i: (bi, 0, jnp.maximum(i * ratio - 1, 0))
    in_specs = [
        pl.BlockSpec((None, tile, dq), cur_rows),
        pl.BlockSpec((None, prev, KV_WIDTH), prev_rows),
        pl.BlockSpec((None, tile, KV_WIDTH), cur_rows),
        pl.BlockSpec((None, KV_WIDTH, prev), prev_cols),
        pl.BlockSpec((None, KV_WIDTH, tile), cur_cols),
        _const_spec(bias.shape),
    ]
    args = [q, k, k, vt, vt, bias]
    if has_sink:
        in_specs.append(_const_spec(sink.shape))
        args.append(sink)
    return pl.pallas_call(
        functools.partial(_band_attn_kernel, prev=prev, n_pairs=tile // PAIR, has_sink=has_sink),
        grid=(b, s // tile),
        in_specs=in_specs,
        out_specs=pl.BlockSpec((None, tile, dq), cur_rows),
        out_shape=jax.ShapeDtypeStruct((b, s, dq), BF16),
        scratch_shapes=[
            pltpu.VMEM((prev + tile, KV_WIDTH), BF16),
            pltpu.VMEM(((prev + tile) // LANES, N_KV_HEADS, HEAD_DIM + ONES_ROWS, LANES), BF16),
            pltpu.VMEM((band, 2 * GROUP * CHUNK), F32),
            pltpu.VMEM((band, 2 * GROUP * CHUNK), F32),
            pltpu.VMEM((N_KV_HEADS * HEAD_DIM, 2 * GROUP * CHUNK), F32),
        ],
        compiler_params=pltpu.CompilerParams(
            dimension_semantics=("arbitrary", "arbitrary"), vmem_limit_bytes=VMEM_LIMIT),
        name="band_attention",
    )(*args)


def _pair_bias(per_head):
    h, tq, tk = per_head.shape
    neg = jnp.full((h, tq, CHUNK), NEG, F32)
    both = jnp.stack([jnp.concatenate([per_head, neg], axis=-1),
                      jnp.concatenate([neg, per_head], axis=-1)], axis=1)
    both = both.reshape(N_KV_HEADS, GROUP, 2, tq, tk + CHUNK)
    return both.transpose(0, 4, 2, 1, 3).reshape(N_KV_HEADS, tk + CHUNK, 2 * GROUP * tq)


def _pair_sink(sink):
    s = (sink.astype(F32) * LOG2E).reshape(N_KV_HEADS, 1, GROUP, 1)
    return jnp.broadcast_to(s, (N_KV_HEADS, 2, GROUP, CHUNK)).reshape(N_KV_HEADS, 1, 2 * GROUP * CHUNK)


def _sample_attn_kernel(*refs, tq, prev, has_sink):
    if has_sink:
        q_ref, kp_ref, kc_ref, vp_ref, vc_ref, bias_ref, rep_ref, sink_ref, o_ref, krep, vrep = refs
    else:
        q_ref, kp_ref, kc_ref, vp_ref, vc_ref, bias_ref, rep_ref, o_ref, krep, vrep = refs
        sink_ref = None

    def replicate(dst, src_ref, off, rows):
        xb = src_ref[...].astype(BF16)
        for h in range(N_KV_HEADS):
            dst[off:off + rows, h * GROUP_WIDTH:(h + 1) * GROUP_WIDTH] = jnp.dot(
                xb, rep_ref[h], preferred_element_type=F32).astype(BF16)

    lane_group = lax.broadcasted_iota(jnp.int32, (tq, GROUP_WIDTH), 1) // HEAD_DIM
    in_group = [lane_group == g for g in range(GROUP)]
    qc = q_ref[...]

    replicate(krep, kp_ref, 0, prev)
    replicate(krep, kc_ref, prev, tq)
    scores = []
    for h in range(N_KV_HEADS):
        qg = qc[:, h * GROUP_WIDTH:(h + 1) * GROUP_WIDTH]
        qs = jnp.concatenate(
            [jnp.where(in_group[g], qg, jnp.zeros_like(qg)) for g in range(GROUP)], axis=0)
        kr = krep[:, h * GROUP_WIDTH:(h + 1) * GROUP_WIDTH]
        s = lax.dot_general(qs, kr, (((1,), (1,)), ((), ())), preferred_element_type=F32)
        scores.append(s + bias_ref[h])
    replicate(vrep, vp_ref, 0, prev)
    replicate(vrep, vc_ref, prev, tq)
    probs = []
    for h in range(N_KV_HEADS):
        s = scores[h]
        m = jnp.max(s, axis=-1, keepdims=True)
        if has_sink:
            m = jnp.maximum(m, sink_ref[h])
        e = jnp.exp2(s - m)
        denom = jnp.sum(e, axis=-1, keepdims=True)
        if has_sink:
            denom = denom + jnp.exp2(sink_ref[h] - m)
        probs.append((e * (1.0 / denom)).astype(BF16))
    for h in range(N_KV_HEADS):
        vr = vrep[:, h * GROUP_WIDTH:(h + 1) * GROUP_WIDTH]
        o = jnp.zeros((tq, GROUP_WIDTH), F32)
        for g in range(GROUP):
            og = jnp.dot(probs[h][g * tq:(g + 1) * tq], vr, preferred_element_type=F32)
            o = jnp.where(in_group[g], og, o)
        o_ref[:, h * GROUP_WIDTH:(h + 1) * GROUP_WIDTH] = o.astype(o_ref.dtype)


def _sample_attention(q, k_cache, k_new, v_cache, v_new, bias, sink):
    b, t, dq = q.shape
    prev = k_cache.shape[1]
    assert bias.shape == (N_KV_HEADS, GROUP * t, prev + t)
    has_sink = sink is not None
    row_map = lambda bi: (bi, 0, 0)
    in_specs = [
        pl.BlockSpec((None, t, dq), row_map),
        pl.BlockSpec((None, prev, KV_WIDTH), row_map),
        pl.BlockSpec((None, t, KV_WIDTH), row_map),
        pl.BlockSpec((None, prev, KV_WIDTH), row_map),
        pl.BlockSpec((None, t, KV_WIDTH), row_map),
        _const_spec(bias.shape),
        _const_spec((N_KV_HEADS, KV_WIDTH, GROUP_WIDTH)),
    ]
    args = [q, k_cache, k_new, v_cache, v_new, bias, _replicate_matrices()]
    if has_sink:
        in_specs.append(_const_spec(sink.shape))
        args.append(sink)
    return pl.pallas_call(
        functools.partial(_sample_attn_kernel, tq=t, prev=prev, has_sink=has_sink),
        grid=(b,),
        in_specs=in_specs,
        out_specs=pl.BlockSpec((None, t, dq), row_map),
        out_shape=jax.ShapeDtypeStruct((b, t, dq), BF16),
        scratch_shapes=[
            pltpu.VMEM((prev + t, N_KV_HEADS * GROUP_WIDTH), BF16),
            pltpu.VMEM((prev + t, N_KV_HEADS * GROUP_WIDTH), BF16),
        ],
        compiler_params=pltpu.CompilerParams(
            dimension_semantics=("arbitrary",), vmem_limit_bytes=VMEM_LIMIT),
        name="sample_attention",
    )(*args)


def _out_mlp_kernel(x_ref, a_ref, wo_ref, g_ref, wup_ref, wdn_ref, o_ref, *, ff_chunks):
    h = x_ref[...] + jnp.dot(a_ref[...], wo_ref[...], preferred_element_type=F32)
    hn = (_rms_rows(h) * g_ref[...]).astype(BF16)
    d_ff = wup_ref.shape[1]
    step = d_ff // ff_chunks
    acc = h
    for c in range(ff_chunks):
        u = jnp.dot(hn, wup_ref[:, c * step:(c + 1) * step], preferred_element_type=F32)
        u = jnp.square(jnp.maximum(u, 0.0)).astype(BF16)
        acc = acc + jnp.dot(u, wdn_ref[c * step:(c + 1) * step, :], preferred_element_type=F32)
    o_ref[...] = acc


def _out_mlp(x, a, wo, g, wup, wdn):
    n, d = x.shape
    d_ff = wup.shape[1]
    tile = min(ROW_TILE, n)
    assert n % tile == 0
    row_spec = pl.BlockSpec((tile, d), lambda i: (i, 0))
    return pl.pallas_call(
        functools.partial(_out_mlp_kernel, ff_chunks=4),
        grid=(n // tile,),
        in_specs=[
            row_spec,
            row_spec,
            _const_spec((d, d)),
            _const_spec((1, d)),
            _const_spec((d, d_ff)),
            _const_spec((d_ff, d)),
        ],
        out_specs=row_spec,
        out_shape=jax.ShapeDtypeStruct((n, d), F32),
        compiler_params=pltpu.CompilerParams(
            dimension_semantics=("arbitrary",), vmem_limit_bytes=VMEM_LIMIT),
        name="out_proj_mlp",
    )(x, a, wo, g.reshape(1, d), wup, wdn)


def _rel_bias_kernel(tab_ref, o_ref, *, tk, delta):
    n_pad = tab_ref.shape[-1]
    r = lax.broadcasted_iota(jnp.int32, (n_pad, tk), 0)
    j = lax.broadcasted_iota(jnp.int32, (n_pad, tk), 1)
    for qi in range(o_ref.shape[0]):
        q = pl.program_id(1) * o_ref.shape[0] + qi
        idx = jnp.clip(q + delta - j, -MAX_REL, MAX_REL) + MAX_REL
        onehot = jnp.where(r == idx, 1.0, 0.0).astype(BF16)
        acc = jnp.dot(tab_ref[0], onehot, preferred_element_type=F32)
        acc = acc + jnp.dot(tab_ref[1], onehot, preferred_element_type=F32)
        acc = acc + jnp.dot(tab_ref[2], onehot, preferred_element_type=F32)
        o_ref[qi] = acc


def _rel_bias(table, tq, tk, delta):
    n_layers, h, n_rel = table.shape
    n_pad = -(-n_rel // LANES) * LANES
    t = jnp.pad(table, ((0, 0), (0, 0), (0, n_pad - n_rel)))
    hi = t.astype(BF16)
    mid = (t - hi.astype(F32)).astype(BF16)
    lo = (t - hi.astype(F32) - mid.astype(F32)).astype(BF16)
    parts = jnp.stack([hi, mid, lo], axis=1)
    return pl.pallas_call(
        functools.partial(_rel_bias_kernel, tk=tk, delta=delta),
        grid=(n_layers, tq // REL_BIAS_ROWS),
        in_specs=[pl.BlockSpec((None, 3, h, n_pad), lambda l, q: (l, 0, 0, 0))],
        out_specs=pl.BlockSpec((None, REL_BIAS_ROWS, h, tk), lambda l, q: (l, q, 0, 0)),
        out_shape=jax.ShapeDtypeStruct((n_layers, tq, h, tk), F32),
        compiler_params=pltpu.CompilerParams(dimension_semantics=("arbitrary", "arbitrary")),
        name="rel_bias_table",
    )(parts)


def _stack_heads(per_head, tq):
    return per_head.reshape(N_KV_HEADS, GROUP * tq, per_head.shape[-1])


def _np_band_mask(q_pos, k_pos, left_chunks):
    qc = q_pos[:, None] // CHUNK
    kc = k_pos[None, :] // CHUNK
    return (k_pos[None, :] >= 0) & (kc <= qc) & (kc >= qc - left_chunks)


def _alibi_bias(q_pos, k_pos, static_mask):
    slopes = (2.0 ** (-8.0 * np.arange(1, N_HEADS + 1, dtype=np.float32) / N_HEADS)).astype(np.float32)
    dist = np.abs(q_pos[:, None] - k_pos[None, :]).astype(np.float32)
    bias = -slopes[:, None, None] * dist[None] * np.float32(LOG2E)
    if static_mask:
        bias = np.where(_np_band_mask(q_pos, k_pos, LEFT_CHUNKS_A)[None], bias, np.float32(NEG))
    return jnp.asarray(bias.astype(np.float32))


def _stack_sink(sink, tq):
    return jnp.repeat(sink.astype(F32) * LOG2E, tq).reshape(N_KV_HEADS, GROUP * tq, 1)


def kernel(x_prompt, x_sample, cache_k_a, cache_v_a, cache_k_b, cache_v_b, g_attn, g_mlp, w_qkv_a,
           g_q_a, g_k_a, sink_a, w_o_a, g_kv, w_kv, g_k_b, w_q_b, g_q_b, rel_bias_b, w_o_b, w_up, w_down):
    batch, seq, d = x_prompt.shape
    dec_batch, t_new, _ = x_sample.shape
    n_layers_a = w_qkv_a.shape[0]
    n_layers_b = w_q_b.shape[0]
    len_a = cache_k_a.shape[2]
    len_b = cache_k_b.shape[1]
    keep_a = min(LEFT_CHUNKS_A * CHUNK, seq)
    keep_b = min(LEFT_CHUNKS_B * CHUNK, seq)
    dq = N_HEADS * HEAD_DIM
    q_scale = HEAD_DIM ** -0.5 * LOG2E
    perm = _group_major_perm()

    hp = x_prompt.reshape(batch * seq, d)
    hs = x_sample.reshape(dec_batch * t_new, d)

    pos_s = PAST_LEN + np.arange(t_new)
    kpos_a = np.concatenate([PAST_LEN - len_a + np.arange(len_a), pos_s])
    kpos_b = np.concatenate([PAST_LEN - len_b + np.arange(len_b), pos_s])
    chunk_q = np.arange(CHUNK)
    bias_a_prompt = _pair_bias(_alibi_bias(chunk_q, np.arange(-LEFT_CHUNKS_A * CHUNK, CHUNK), static_mask=False))
    bias_a_sample = _stack_heads(_alibi_bias(pos_s, kpos_a, static_mask=True), t_new)

    ones_kv = jnp.ones((KV_WIDTH,), F32)
    seg_a_p = [(0, dq, True), (dq, KV_WIDTH, True)]
    seg_a_s = seg_a_p + [(dq + KV_WIDTH, KV_WIDTH, False)]
    seg_kv = [(0, KV_WIDTH, True), (KV_WIDTH, KV_WIDTH, False)]
    last_rows = lambda h, keep: h.reshape(batch, seq, d)[:, seq - keep:].reshape(batch * keep, d)
    kv4 = lambda a, n, t: a.reshape(n, t, N_KV_HEADS, HEAD_DIM)

    ka_p, va_p, ka_s, va_s = [], [], [], []
    for i in range(n_layers_a):
        w = w_qkv_a[i].astype(BF16)
        w_p = jnp.concatenate([w[:, :dq][:, perm], w[:, dq:dq + KV_WIDTH]], axis=1)
        wvt = w[:, dq + KV_WIDTH:].T
        gains = jnp.concatenate([jnp.tile(g_q_a[i] * q_scale, N_HEADS), jnp.tile(g_k_a[i], N_KV_HEADS), ones_kv])
        wo = w_o_a[i].astype(BF16)
        wup = w_up[i].astype(BF16)
        wdn = w_down[i].astype(BF16)

        qp, kp16, vtp = _project(hp, g_attn[i], w_p, gains[:dq + KV_WIDTH], seg_a_p, (BF16, BF16),
                                 wt=wvt, seq=seq)
        ap = _prompt_attention(qp.reshape(batch, seq, dq), kp16.reshape(batch, seq, KV_WIDTH), vtp,
                               bias_a_prompt, _pair_sink(sink_a[i]), LEFT_CHUNKS_A)
        kp, vp = _project(last_rows(hp, keep_a), g_attn[i], w[:, dq:], gains[dq:], seg_kv, (F32, F32))
        ka_p.append(kv4(kp, batch, keep_a))
        va_p.append(kv4(vp, batch, keep_a))

        qs, ks, vs = _project(hs, g_attn[i], w, gains, seg_a_s, (BF16, F32, F32))
        ks3 = ks.reshape(dec_batch, t_new, KV_WIDTH)
        vs3 = vs.reshape(dec_batch, t_new, KV_WIDTH)
        as_ = _sample_attention(qs.reshape(dec_batch, t_new, dq),
                                cache_k_a[i].reshape(dec_batch, len_a, KV_WIDTH), ks3,
                                cache_v_a[i].reshape(dec_batch, len_a, KV_WIDTH), vs3,
                                bias_a_sample, _stack_sink(sink_a[i], t_new))
        ka_s.append(ks3.reshape(dec_batch, t_new, N_KV_HEADS, HEAD_DIM))
        va_s.append(vs3.reshape(dec_batch, t_new, N_KV_HEADS, HEAD_DIM))

        hp = _out_mlp(hp, ap.reshape(batch * seq, dq), wo[perm, :], g_mlp[i], wup, wdn)
        hs = _out_mlp(hs, as_.reshape(dec_batch * t_new, dq), wo, g_mlp[i], wup, wdn)

    wkv = w_kv.astype(BF16)
    gains_kv = jnp.concatenate([jnp.tile(g_k_b, N_KV_HEADS), ones_kv])
    kb_p16, vtb_p = _project(hp, g_kv, wkv[:, :KV_WIDTH], gains_kv[:KV_WIDTH], seg_kv[:1], (BF16,),
                             wt=wkv[:, KV_WIDTH:].T, seq=seq)
    kb_p, vb_p = _project(last_rows(hp, keep_b), g_kv, wkv, gains_kv, seg_kv, (F32, F32))
    kb_s, vb_s = _project(hs, g_kv, wkv, gains_kv, seg_kv, (F32, F32))
    kb_p16 = kb_p16.reshape(batch, seq, KV_WIDTH)
    kb_s3 = kb_s.reshape(dec_batch, t_new, KV_WIDTH)
    vb_s3 = vb_s.reshape(dec_batch, t_new, KV_WIDTH)
    cache_kb = cache_k_b.reshape(dec_batch, len_b, KV_WIDTH)
    cache_vb = cache_v_b.reshape(dec_batch, len_b, KV_WIDTH)

    pad_b = LEFT_CHUNKS_B * CHUNK
    tk_b = pad_b + CHUNK
    assert int(pos_s[0] - kpos_b[0]) == pad_b and t_new <= CHUNK and len_b + t_new <= tk_b
    assert np.all(np.diff(kpos_b) == 1)
    rel = _rel_bias(rel_bias_b.astype(F32) * LOG2E, CHUNK, tk_b, pad_b)
    rel = rel.transpose(0, 2, 1, 3)
    mask_s = np.where(_np_band_mask(pos_s, kpos_b, LEFT_CHUNKS_B), 0.0, NEG).astype(np.float32)

    seg_q = [(0, dq, True)]
    for j in range(n_layers_b):
        layer = n_layers_a + j
        wq = w_q_b[j].astype(BF16)
        gains = jnp.tile(g_q_b[j] * q_scale, N_HEADS)
        wo = w_o_b[j].astype(BF16)
        wup = w_up[layer].astype(BF16)
        wdn = w_down[layer].astype(BF16)
        bias_p = _pair_bias(rel[j])
        bias_s = _stack_heads(rel[j][:, :t_new, :len_b + t_new] + mask_s[None], t_new)

        (qp,) = _project(hp, g_attn[layer], wq[:, perm], gains, seg_q, (BF16,))
        ap = _prompt_attention(qp.reshape(batch, seq, dq), kb_p16, vtb_p, bias_p, None, LEFT_CHUNKS_B)
        (qs,) = _project(hs, g_attn[layer], wq, gains, seg_q, (BF16,))
        as_ = _sample_attention(qs.reshape(dec_batch, t_new, dq), cache_kb, kb_s3, cache_vb, vb_s3, bias_s, None)

        hp = _out_mlp(hp, ap.reshape(batch * seq, dq), wo[perm, :], g_mlp[layer], wup, wdn)
        hs = _out_mlp(hs, as_.reshape(dec_batch * t_new, dq), wo, g_mlp[layer], wup, wdn)

    return (hp.reshape(batch, seq, d), hs.reshape(dec_batch, t_new, d),
            jnp.stack(ka_p), jnp.stack(va_p),
            kv4(kb_p, batch, keep_b), kv4(vb_p, batch, keep_b),
            jnp.stack(ka_s), jnp.stack(va_s),
            kv4(kb_s3, dec_batch, t_new), kv4(vb_s3, dec_batch, t_new))
```

```python
import functools
import math

import numpy as np
import jax
import jax.numpy as jnp
from jax import lax
from jax.experimental import pallas as pl
from jax.experimental.pallas import tpu as pltpu

CHUNK = 64
HEAD_DIM = 64
N_KV_HEADS = 4
GROUP = 4
N_HEADS = N_KV_HEADS * GROUP
LEFT_CHUNKS_A = 2
LEFT_CHUNKS_B = 8
MAX_REL = 128
N_REL = 2 * MAX_REL + 1
PAST_LEN = 1024
EPS = 1e-6
NEG = -1e30
LOG2E = math.log2(math.e)

LANES = 128
KV_WIDTH = N_KV_HEADS * HEAD_DIM
GROUP_WIDTH = GROUP * HEAD_DIM
PAIR = 2 * CHUNK
ONES_ROWS = 16
REL_BIAS_ROWS = 8
ROW_TILE = 512
MLP_TILE = 1024
MLP_FF_CHUNKS = 8
ATTN_TILE = 2048
PAIR_UNROLL = 4
VMEM_LIMIT = 56 * 1024 * 1024

BF16 = jnp.bfloat16
F32 = jnp.float32


def _const_spec(shape):
    zeros = (0,) * len(shape)
    return pl.BlockSpec(shape, lambda *_: zeros, pipeline_mode=pl.Buffered(1))


def _head_sum_matrix():
    idx = np.arange(GROUP_WIDTH) // HEAD_DIM
    return jnp.asarray(idx[:, None] == idx[None, :], dtype=BF16)


def _group_major_perm():
    g, h, d = np.meshgrid(np.arange(GROUP), np.arange(N_KV_HEADS), np.arange(HEAD_DIM), indexing="ij")
    return ((h * GROUP + g) * HEAD_DIM + d).reshape(-1)


def _rms_rows(x):
    return x * lax.rsqrt(jnp.mean(x * x, axis=-1, keepdims=True) + EPS)


def _proj_kernel(*refs, segments, has_vt):
    if has_vt:
        x_ref, g_ref, w_ref, hsum_ref, gain_ref, wt_ref = refs[:6]
        out_refs = refs[6:]
    else:
        x_ref, g_ref, w_ref, hsum_ref, gain_ref = refs[:5]
        out_refs = refs[5:]
    xn = (_rms_rows(x_ref[...]) * g_ref[...]).astype(BF16)
    y = jnp.dot(xn, w_ref[...], preferred_element_type=F32)
    for (start, width, normed), o_ref in zip(segments, out_refs):
        for t in range(width // GROUP_WIDTH):
            lo = start + t * GROUP_WIDTH
            yt = y[:, lo:lo + GROUP_WIDTH]
            if normed:
                ss = jnp.dot((yt * yt).astype(BF16), hsum_ref[...], preferred_element_type=F32)
                yt = yt * lax.rsqrt(ss * (1.0 / HEAD_DIM) + EPS) * gain_ref[:, lo:lo + GROUP_WIDTH]
            o_ref[:, t * GROUP_WIDTH:(t + 1) * GROUP_WIDTH] = yt.astype(o_ref.dtype)
    if has_vt:
        vt = lax.dot_general(wt_ref[...], xn, (((1,), (1,)), ((), ())), preferred_element_type=F32)
        out_refs[-1][...] = vt.astype(out_refs[-1].dtype)


def _project(x, g, w, gains, segments, out_dtypes, wt=None, seq=None):
    n, d = x.shape
    c = w.shape[1]
    tile = min(ROW_TILE, n)
    assert n % tile == 0
    out_shape = [jax.ShapeDtypeStruct((n, width), dt) for (_, width, _), dt in zip(segments, out_dtypes)]
    out_specs = [pl.BlockSpec((tile, width), lambda i: (i, 0)) for (_, width, _) in segments]
    in_specs = [
        pl.BlockSpec((tile, d), lambda i: (i, 0)),
        _const_spec((1, d)),
        _const_spec((d, c)),
        _const_spec((GROUP_WIDTH, GROUP_WIDTH)),
        _const_spec((1, c)),
    ]
    args = [x, g.reshape(1, d), w, _head_sum_matrix(), gains.reshape(1, c)]
    if wt is not None:
        assert seq % tile == 0 and n % seq == 0
        per_seq = seq // tile
        in_specs.append(_const_spec(wt.shape))
        args.append(wt)
        out_shape.append(jax.ShapeDtypeStruct((n // seq, wt.shape[0], seq), BF16))
        out_specs.append(pl.BlockSpec((None, wt.shape[0], tile), lambda i: (i // per_seq, 0, i % per_seq)))
    return pl.pallas_call(
        functools.partial(_proj_kernel, segments=tuple(segments), has_vt=wt is not None),
        grid=(n // tile,),
        in_specs=in_specs,
        out_specs=out_specs,
        out_shape=out_shape,
        compiler_params=pltpu.CompilerParams(
            dimension_semantics=("arbitrary",), vmem_limit_bytes=VMEM_LIMIT),
        name="project",
    )(*args)


def _band_attn_kernel(*refs, prev, n_pairs, has_sink):
    if has_sink:
        q_ref, kp_ref, kc_ref, vtp_ref, vtc_ref, bias_ref, sink_ref, o_ref, kcat, vt3, s_buf0, s_buf1, ot_prev = refs
    else:
        q_ref, kp_ref, kc_ref, vtp_ref, vtc_ref, bias_ref, o_ref, kcat, vt3, s_buf0, s_buf1, ot_prev = refs
        sink_ref = None
    s_bufs = (s_buf0, s_buf1)
    tile = n_pairs * PAIR
    band = prev + PAIR
    n_cols = 2 * GROUP * CHUNK

    kcat[0:prev, :] = kp_ref[...]
    kcat[prev:prev + tile, :] = kc_ref[...]
    ones = jnp.ones((ONES_ROWS, LANES), BF16)
    blocks = [vtp_ref[:, b * LANES:(b + 1) * LANES] for b in range(prev // LANES)]
    blocks += [vtc_ref[:, b * LANES:(b + 1) * LANES] for b in range(tile // LANES)]
    for b, blk in enumerate(blocks):
        for h in range(N_KV_HEADS):
            vt3[b, h, 0:HEAD_DIM, :] = blk[h * HEAD_DIM:(h + 1) * HEAD_DIM, :]
            vt3[b, h, HEAD_DIM:HEAD_DIM + ONES_ROWS, :] = ones

    lane_head = lax.broadcasted_iota(jnp.int32, (CHUNK, KV_WIDTH), 1) // HEAD_DIM
    head_lanes = [jnp.where(lane_head == h, 1.0, 0.0).astype(BF16) for h in range(N_KV_HEADS)]

    def scores(j, h, masked, s_buf):
        r0 = pl.multiple_of(j * PAIR, PAIR)
        qp = q_ref[pl.ds(r0, PAIR), :]
        kb = kcat[pl.ds(r0, band), :]
        qs = jnp.concatenate(
            [qp[c * CHUNK:(c + 1) * CHUNK, g * KV_WIDTH:(g + 1) * KV_WIDTH] * head_lanes[h]
             for c in range(2) for g in range(GROUP)], axis=0)
        s = lax.dot_general(kb, qs, (((1,), (1,)), ((), ())), preferred_element_type=F32)
        s = s + bias_ref[h]
        if masked:
            valid = lax.broadcasted_iota(jnp.int32, (band, n_cols), 0) >= prev - r0
            s = jnp.where(valid, s, NEG)
        s_buf[...] = s
        return jnp.max(s, axis=0, keepdims=True)

    def softmax_pv(m, s_buf, j, h):
        if has_sink:
            m = jnp.maximum(m, sink_ref[h])
        ot = jnp.zeros((HEAD_DIM + ONES_ROWS, n_cols), F32)
        n_blocks = band // LANES
        for b in range(0, n_blocks, 2):
            width = min(2, n_blocks - b) * LANES
            eb = jnp.exp2(s_buf[b * LANES:b * LANES + width, :] - m).astype(BF16)
            vt = jnp.concatenate([vt3[j + b + i, h] for i in range(width // LANES)], axis=1)
            ot = ot + jnp.dot(vt, eb, preferred_element_type=F32)
        denom = ot[HEAD_DIM:HEAD_DIM + 1, :]
        if has_sink:
            denom = denom + jnp.exp2(sink_ref[h] - m)
        return ot[0:HEAD_DIM, :] * (1.0 / denom)

    def run(masked):
        m_first = scores(0, 0, masked, s_bufs[0])
        ot_prev[...] = jnp.zeros(ot_prev.shape, F32)

        def write_out(j):
            r0 = pl.multiple_of(j * PAIR, PAIR)
            for c in range(2):
                oc = ot_prev[:, c * GROUP_WIDTH:(c + 1) * GROUP_WIDTH].T
                for g in range(GROUP):
                    o_ref[pl.ds(r0 + c * CHUNK, CHUNK), g * KV_WIDTH:(g + 1) * KV_WIDTH] = (
                        oc[g * CHUNK:(g + 1) * CHUNK, :].astype(o_ref.dtype))

        def body(j, m):
            write_out(jnp.maximum(j - 1, 0))
            for h in range(N_KV_HEADS):
                if h + 1 < N_KV_HEADS:
                    m_ahead = scores(j, h + 1, masked, s_bufs[(h + 1) % 2])
                else:
                    m_ahead = scores(jnp.minimum(j + 1, n_pairs - 1), 0, masked, s_bufs[0])
                ot_prev[h * HEAD_DIM:(h + 1) * HEAD_DIM, :] = softmax_pv(m, s_bufs[h % 2], j, h)
                m = m_ahead
            return m

        lax.fori_loop(0, n_pairs, body, m_first, unroll=PAIR_UNROLL)
        write_out(n_pairs - 1)

    @pl.when(pl.program_id(1) == 0)
    def _():
        run(True)

    @pl.when(pl.program_id(1) > 0)
    def _():
        run(False)


def _prompt_attention(q, k, vt, bias, sink, left_chunks):
    b, s, dq = q.shape
    prev = left_chunks * CHUNK
    tile = min(ATTN_TILE, s)
    band = prev + PAIR
    assert s % tile == 0 and tile % prev == 0 and prev % LANES == 0 and tile % PAIR == 0
    assert bias.shape == (N_KV_HEADS, band, 2 * GROUP * CHUNK)
    ratio = tile // prev
    has_sink = sink is not None
    cur_rows = lambda bi, i: (bi, i, 0)
    prev_rows = lambda bi, i: (bi, jnp.maximum(i * ratio - 1, 0), 0)
    cur_cols = lambda bi, i: (bi, 0, i)
    prev_cols = lambda bi, i: (bi, 0, jnp.maximum(i * ratio - 1, 0))
    in_specs = [
        pl.BlockSpec((None, tile, dq), cur_rows),
        pl.BlockSpec((None, prev, KV_WIDTH), prev_rows),
        pl.BlockSpec((None, tile, KV_WIDTH), cur_rows),
        pl.BlockSpec((None, KV_WIDTH, prev), prev_cols),
        pl.BlockSpec((None, KV_WIDTH, tile), cur_cols),
        _const_spec(bias.shape),
    ]
    args = [q, k, k, vt, vt, bias]
    if has_sink:
        in_specs.append(_const_spec(sink.shape))
        args.append(sink)
    return pl.pallas_call(
        functools.partial(_band_attn_kernel, prev=prev, n_pairs=tile // PAIR, has_sink=has_sink),
        grid=(b, s // tile),
        in_specs=in_specs,
        out_specs=pl.BlockSpec((None, tile, dq), cur_rows),
        out_shape=jax.ShapeDtypeStruct((b, s, dq), BF16),
        scratch_shapes=[
            pltpu.VMEM((prev + tile, KV_WIDTH), BF16),
            pltpu.VMEM(((prev + tile) // LANES, N_KV_HEADS, HEAD_DIM + ONES_ROWS, LANES), BF16),
            pltpu.VMEM((band, 2 * GROUP * CHUNK), F32),
            pltpu.VMEM((band, 2 * GROUP * CHUNK), F32),
            pltpu.VMEM((N_KV_HEADS * HEAD_DIM, 2 * GROUP * CHUNK), F32),
        ],
        compiler_params=pltpu.CompilerParams(
            dimension_semantics=("arbitrary", "arbitrary"), vmem_limit_bytes=VMEM_LIMIT),
        name="band_attention",
    )(*args)


def _pair_bias(per_head):
    h, tq, tk = per_head.shape
    neg = jnp.full((h, tq, CHUNK), NEG, F32)
    both = jnp.stack([jnp.concatenate([per_head, neg], axis=-1),
                      jnp.concatenate([neg, per_head], axis=-1)], axis=1)
    both = both.reshape(N_KV_HEADS, GROUP, 2, tq, tk + CHUNK)
    return both.transpose(0, 4, 2, 1, 3).reshape(N_KV_HEADS, tk + CHUNK, 2 * GROUP * tq)


def _pair_sink(sink):
    s = (sink.astype(F32) * LOG2E).reshape(N_KV_HEADS, 1, GROUP, 1)
    return jnp.broadcast_to(s, (N_KV_HEADS, 2, GROUP, CHUNK)).reshape(N_KV_HEADS, 1, 2 * GROUP * CHUNK)


def _sample_attn_kernel(*refs, tq, has_sink):
    if has_sink:
        q_ref, kp_ref, kc_ref, vp_ref, vc_ref, bias_ref, sink_ref, o_ref = refs
    else:
        q_ref, kp_ref, kc_ref, vp_ref, vc_ref, bias_ref, o_ref = refs
        sink_ref = None
    kcat = jnp.concatenate([kp_ref[...], kc_ref[...]], axis=0).astype(BF16)
    vcat = jnp.concatenate([vp_ref[...], vc_ref[...]], axis=0).astype(BF16)
    lane_head = lax.broadcasted_iota(jnp.int32, (tq, KV_WIDTH), 1) // HEAD_DIM
    qc = q_ref[...]
    qs = jnp.concatenate(
        [jnp.where(lane_head == h, qc[:, g * KV_WIDTH:(g + 1) * KV_WIDTH], 0)
         for h in range(N_KV_HEADS) for g in range(GROUP)], axis=0).astype(BF16)
    s = lax.dot_general(qs, kcat, (((1,), (1,)), ((), ())), preferred_element_type=F32)
    s = s + bias_ref[...]
    m = jnp.max(s, axis=-1, keepdims=True)
    if has_sink:
        m = jnp.maximum(m, sink_ref[...])
    e = jnp.exp2(s - m)
    denom = jnp.sum(e, axis=-1, keepdims=True)
    if has_sink:
        denom = denom + jnp.exp2(sink_ref[...] - m)
    p = (e * (1.0 / denom)).astype(BF16)
    o_all = jnp.dot(p, vcat, preferred_element_type=F32)
    for g in range(GROUP):
        o = jnp.zeros((tq, KV_WIDTH), F32)
        for h in range(N_KV_HEADS):
            r0 = (h * GROUP + g) * tq
            o = jnp.where(lane_head == h, o_all[r0:r0 + tq, :], o)
        o_ref[:, g * KV_WIDTH:(g + 1) * KV_WIDTH] = o.astype(o_ref.dtype)


def _sample_attention(q, k_cache, k_new, v_cache, v_new, bias, sink):
    b, t, dq = q.shape
    prev = k_cache.shape[1]
    assert bias.shape == (N_HEADS * t, prev + t)
    has_sink = sink is not None
    row_map = lambda bi: (bi, 0, 0)
    in_specs = [
        pl.BlockSpec((None, t, dq), row_map),
        pl.BlockSpec((None, prev, KV_WIDTH), row_map),
        pl.BlockSpec((None, t, KV_WIDTH), row_map),
        pl.BlockSpec((None, prev, KV_WIDTH), row_map),
        pl.BlockSpec((None, t, KV_WIDTH), row_map),
        _const_spec(bias.shape),
    ]
    args = [q, k_cache, k_new, v_cache, v_new, bias]
    if has_sink:
        in_specs.append(_const_spec(sink.shape))
        args.append(sink)
    return pl.pallas_call(
        functools.partial(_sample_attn_kernel, tq=t, has_sink=has_sink),
        grid=(b,),
        in_specs=in_specs,
        out_specs=pl.BlockSpec((None, t, dq), row_map),
        out_shape=jax.ShapeDtypeStruct((b, t, dq), BF16),
        compiler_params=pltpu.CompilerParams(
            dimension_semantics=("arbitrary",), vmem_limit_bytes=VMEM_LIMIT),
        name="sample_attention",
    )(*args)


def _out_mlp_kernel(x_ref, a_ref, wo_ref, g_ref, wup_ref, wdn_ref, o_ref, *, ff_chunks):
    h = x_ref[...] + jnp.dot(a_ref[...], wo_ref[...], preferred_element_type=F32)
    hn = (_rms_rows(h) * g_ref[...]).astype(BF16)
    d_ff = wup_ref.shape[1]
    step = d_ff // ff_chunks
    acc = h
    for c in range(ff_chunks):
        u = jnp.dot(hn, wup_ref[:, c * step:(c + 1) * step], preferred_element_type=F32)
        u = jnp.square(jnp.maximum(u, 0.0)).astype(BF16)
        acc = acc + jnp.dot(u, wdn_ref[c * step:(c + 1) * step, :], preferred_element_type=F32)
    o_ref[...] = acc


def _out_mlp(x, a, wo, g, wup, wdn):
    n, d = x.shape
    d_ff = wup.shape[1]
    tile = min(MLP_TILE, n)
    assert n % tile == 0
    row_spec = pl.BlockSpec((tile, d), lambda i: (i, 0))
    return pl.pallas_call(
        functools.partial(_out_mlp_kernel, ff_chunks=MLP_FF_CHUNKS),
        grid=(n // tile,),
        in_specs=[
            row_spec,
            row_spec,
            _const_spec((d, d)),
            _const_spec((1, d)),
            _const_spec((d, d_ff)),
            _const_spec((d_ff, d)),
        ],
        out_specs=row_spec,
        out_shape=jax.ShapeDtypeStruct((n, d), F32),
        compiler_params=pltpu.CompilerParams(
            dimension_semantics=("arbitrary",), vmem_limit_bytes=VMEM_LIMIT),
        name="out_proj_mlp",
    )(x, a, wo, g.reshape(1, d), wup, wdn)


def _rel_bias_kernel(tab_ref, o_ref, *, tk, delta):
    n_pad = tab_ref.shape[-1]
    r = lax.broadcasted_iota(jnp.int32, (n_pad, tk), 0)
    j = lax.broadcasted_iota(jnp.int32, (n_pad, tk), 1)
    for qi in range(o_ref.shape[0]):
        q = pl.program_id(1) * o_ref.shape[0] + qi
        idx = jnp.clip(q + delta - j, -MAX_REL, MAX_REL) + MAX_REL
        onehot = jnp.where(r == idx, 1.0, 0.0).astype(BF16)
        acc = jnp.dot(tab_ref[0], onehot, preferred_element_type=F32)
        acc = acc + jnp.dot(tab_ref[1], onehot, preferred_element_type=F32)
        acc = acc + jnp.dot(tab_ref[2], onehot, preferred_element_type=F32)
        o_ref[qi] = acc


def _rel_bias(table, tq, tk, delta):
    n_layers, h, n_rel = table.shape
    n_pad = -(-n_rel // LANES) * LANES
    t = jnp.pad(table, ((0, 0), (0, 0), (0, n_pad - n_rel)))
    hi = t.astype(BF16)
    mid = (t - hi.astype(F32)).astype(BF16)
    lo = (t - hi.astype(F32) - mid.astype(F32)).astype(BF16)
    parts = jnp.stack([hi, mid, lo], axis=1)
    return pl.pallas_call(
        functools.partial(_rel_bias_kernel, tk=tk, delta=delta),
        grid=(n_layers, tq // REL_BIAS_ROWS),
        in_specs=[pl.BlockSpec((None, 3, h, n_pad), lambda l, q: (l, 0, 0, 0))],
        out_specs=pl.BlockSpec((None, REL_BIAS_ROWS, h, tk), lambda l, q: (l, q, 0, 0)),
        out_shape=jax.ShapeDtypeStruct((n_layers, tq, h, tk), F32),
        compiler_params=pltpu.CompilerParams(dimension_semantics=("arbitrary", "arbitrary")),
        name="rel_bias_table",
    )(parts)


def _np_band_mask(q_pos, k_pos, left_chunks):
    qc = q_pos[:, None] // CHUNK
    kc = k_pos[None, :] // CHUNK
    return (k_pos[None, :] >= 0) & (kc <= qc) & (kc >= qc - left_chunks)


def _alibi_bias(q_pos, k_pos, static_mask):
    slopes = (2.0 ** (-8.0 * np.arange(1, N_HEADS + 1, dtype=np.float32) / N_HEADS)).astype(np.float32)
    dist = np.abs(q_pos[:, None] - k_pos[None, :]).astype(np.float32)
    bias = -slopes[:, None, None] * dist[None] * np.float32(LOG2E)
    if static_mask:
        bias = np.where(_np_band_mask(q_pos, k_pos, LEFT_CHUNKS_A)[None], bias, np.float32(NEG))
    return jnp.asarray(bias.astype(np.float32))


def _row_sink(sink, tq):
    return jnp.repeat(sink.astype(F32) * LOG2E, tq).reshape(N_HEADS * tq, 1)


def kernel(x_prompt, x_sample, cache_k_a, cache_v_a, cache_k_b, cache_v_b, g_attn, g_mlp, w_qkv_a,
           g_q_a, g_k_a, sink_a, w_o_a, g_kv, w_kv, g_k_b, w_q_b, g_q_b, rel_bias_b, w_o_b, w_up, w_down):
    batch, seq, d = x_prompt.shape
    dec_batch, t_new, _ = x_sample.shape
    n_layers_a = w_qkv_a.shape[0]
    n_layers_b = w_q_b.shape[0]
    len_a = cache_k_a.shape[2]
    len_b = cache_k_b.shape[1]
    keep_a = min(LEFT_CHUNKS_A * CHUNK, seq)
    keep_b = min(LEFT_CHUNKS_B * CHUNK, seq)
    dq = N_HEADS * HEAD_DIM
    q_scale = HEAD_DIM ** -0.5 * LOG2E
    perm = _group_major_perm()
    qkv_cols = np.concatenate([perm, np.arange(dq, dq + 2 * KV_WIDTH)])

    hp = x_prompt.reshape(batch * seq, d)
    hs = x_sample.reshape(dec_batch * t_new, d)

    pos_s = PAST_LEN + np.arange(t_new)
    kpos_a = np.concatenate([PAST_LEN - len_a + np.arange(len_a), pos_s])
    kpos_b = np.concatenate([PAST_LEN - len_b + np.arange(len_b), pos_s])
    chunk_q = np.arange(CHUNK)
    bias_a_prompt = _pair_bias(_alibi_bias(chunk_q, np.arange(-LEFT_CHUNKS_A * CHUNK, CHUNK), static_mask=False))
    bias_a_sample = _alibi_bias(pos_s, kpos_a, static_mask=True).reshape(N_HEADS * t_new, -1)

    ones_kv = jnp.ones((KV_WIDTH,), F32)
    seg_a_p = [(0, dq, True), (dq, KV_WIDTH, True)]
    seg_a_s = seg_a_p + [(dq + KV_WIDTH, KV_WIDTH, False)]
    seg_kv = [(0, KV_WIDTH, True), (KV_WIDTH, KV_WIDTH, False)]
    last_rows = lambda h, keep: h.reshape(batch, seq, d)[:, seq - keep:].reshape(batch * keep, d)
    kv4 = lambda a, n, t: a.reshape(n, t, N_KV_HEADS, HEAD_DIM)

    ka_p, va_p, ka_s, va_s = [], [], [], []
    for i in range(n_layers_a):
        w = w_qkv_a[i][:, qkv_cols].astype(BF16)
        wvt = w_qkv_a[i][:, dq + KV_WIDTH:].T.astype(BF16)
        gains = jnp.concatenate([jnp.tile(g_q_a[i] * q_scale, N_HEADS), jnp.tile(g_k_a[i], N_KV_HEADS), ones_kv])
        wo = w_o_a[i][perm, :].astype(BF16)
        wup = w_up[i].astype(BF16)
        wdn = w_down[i].astype(BF16)

        qp, kp16, vtp = _project(hp, g_attn[i], w[:, :dq + KV_WIDTH], gains[:dq + KV_WIDTH], seg_a_p,
                                 (BF16, BF16), wt=wvt, seq=seq)
        ap = _prompt_attention(qp.reshape(batch, seq, dq), kp16.reshape(batch, seq, KV_WIDTH), vtp,
                               bias_a_prompt, _pair_sink(sink_a[i]), LEFT_CHUNKS_A)
        kp, vp = _project(last_rows(hp, keep_a), g_attn[i], w[:, dq:], gains[dq:], seg_kv, (F32, F32))
        ka_p.append(kv4(kp, batch, keep_a))
        va_p.append(kv4(vp, batch, keep_a))

        qs, ks, vs = _project(hs, g_attn[i], w, gains, seg_a_s, (BF16, F32, F32))
        ks3 = ks.reshape(dec_batch, t_new, KV_WIDTH)
        vs3 = vs.reshape(dec_batch, t_new, KV_WIDTH)
        as_ = _sample_attention(qs.reshape(dec_batch, t_new, dq),
                                cache_k_a[i].reshape(dec_batch, len_a, KV_WIDTH), ks3,
                                cache_v_a[i].reshape(dec_batch, len_a, KV_WIDTH), vs3,
                                bias_a_sample, _row_sink(sink_a[i], t_new))
        ka_s.append(ks3.reshape(dec_batch, t_new, N_KV_HEADS, HEAD_DIM))
        va_s.append(vs3.reshape(dec_batch, t_new, N_KV_HEADS, HEAD_DIM))

        hp = _out_mlp(hp, ap.reshape(batch * seq, dq), wo, g_mlp[i], wup, wdn)
        hs = _out_mlp(hs, as_.reshape(dec_batch * t_new, dq), wo, g_mlp[i], wup, wdn)

    wkv = w_kv.astype(BF16)
    gains_kv = jnp.concatenate([jnp.tile(g_k_b, N_KV_HEADS), ones_kv])
    kb_p16, vtb_p = _project(hp, g_kv, wkv[:, :KV_WIDTH], gains_kv[:KV_WIDTH], seg_kv[:1], (BF16,),
                             wt=wkv[:, KV_WIDTH:].T, seq=seq)
    kb_p, vb_p = _project(last_rows(hp, keep_b), g_kv, wkv, gains_kv, seg_kv, (F32, F32))
    kb_s, vb_s = _project(hs, g_kv, wkv, gains_kv, seg_kv, (F32, F32))
    kb_p16 = kb_p16.reshape(batch, seq, KV_WIDTH)
    kb_s3 = kb_s.reshape(dec_batch, t_new, KV_WIDTH)
    vb_s3 = vb_s.reshape(dec_batch, t_new, KV_WIDTH)
    cache_kb = cache_k_b.reshape(dec_batch, len_b, KV_WIDTH)
    cache_vb = cache_v_b.reshape(dec_batch, len_b, KV_WIDTH)

    pad_b = LEFT_CHUNKS_B * CHUNK
    tk_b = pad_b + CHUNK
    assert int(pos_s[0] - kpos_b[0]) == pad_b and t_new <= CHUNK and len_b + t_new <= tk_b
    assert np.all(np.diff(kpos_b) == 1)
    rel = _rel_bias(rel_bias_b.astype(F32) * LOG2E, CHUNK, tk_b, pad_b)
    rel = rel.transpose(0, 2, 1, 3)
    mask_s = np.where(_np_band_mask(pos_s, kpos_b, LEFT_CHUNKS_B), 0.0, NEG).astype(np.float32)

    seg_q = [(0, dq, True)]
    for j in range(n_layers_b):
        layer = n_layers_a + j
        wq = w_q_b[j][:, perm].astype(BF16)
        gains = jnp.tile(g_q_b[j] * q_scale, N_HEADS)
        wo = w_o_b[j][perm, :].astype(BF16)
        wup = w_up[layer].astype(BF16)
        wdn = w_down[layer].astype(BF16)
        bias_p = _pair_bias(rel[j])
        bias_s = (rel[j][:, :t_new, :len_b + t_new] + mask_s[None]).reshape(N_HEADS * t_new, -1)

        (qp,) = _project(hp, g_attn[layer], wq, gains, seg_q, (BF16,))
        ap = _prompt_attention(qp.reshape(batch, seq, dq), kb_p16, vtb_p, bias_p, None, LEFT_CHUNKS_B)
        (qs,) = _project(hs, g_attn[layer], wq, gains, seg_q, (BF16,))
        as_ = _sample_attention(qs.reshape(dec_batch, t_new, dq), cache_kb, kb_s3, cache_vb, vb_s3, bias_s, None)

        hp = _out_mlp(hp, ap.reshape(batch * seq, dq), wo, g_mlp[layer], wup, wdn)
        hs = _out_mlp(hs, as_.reshape(dec_batch * t_new, dq), wo, g_mlp[layer], wup, wdn)

    return (hp.reshape(batch, seq, d), hs.reshape(dec_batch, t_new, d),
            jnp.stack(ka_p), jnp.stack(va_p),
            kv4(kb_p, batch, keep_b), kv4(vb_p, batch, keep_b),
            jnp.stack(ka_s), jnp.stack(va_s),
            kv4(kb_s3, dec_batch, t_new), kv4(vb_s3, dec_batch, t_new))
```

```python
import functools
import math

import numpy as np
import jax
import jax.numpy as jnp
from jax import lax
from jax.experimental import pallas as pl
from jax.experimental.pallas import tpu as pltpu

CHUNK = 64
HEAD_DIM = 64
N_KV_HEADS = 4
GROUP = 4
N_HEADS = N_KV_HEADS * GROUP
LEFT_CHUNKS_A = 2
LEFT_CHUNKS_B = 8
MAX_REL = 128
N_REL = 2 * MAX_REL + 1
PAST_LEN = 1024
EPS = 1e-6
NEG = -1e30
LOG2E = math.log2(math.e)

LANES = 128
KV_WIDTH = N_KV_HEADS * HEAD_DIM
GROUP_WIDTH = GROUP * HEAD_DIM
PAIR = 2 * CHUNK
ONES_ROWS = 16
REL_BIAS_ROWS = 8
ROW_TILE = 1024
MLP_TILE = 1024
MLP_FF_CHUNKS = 8
ATTN_TILE = 2048
PAIR_UNROLL = 4
VMEM_LIMIT = 56 * 1024 * 1024

BF16 = jnp.bfloat16
F32 = jnp.float32


def _const_spec(shape):
    zeros = (0,) * len(shape)
    return pl.BlockSpec(shape, lambda *_: zeros, pipeline_mode=pl.Buffered(1))


def _head_sum_matrix():
    idx = np.arange(GROUP_WIDTH) // HEAD_DIM
    return jnp.asarray(idx[:, None] == idx[None, :], dtype=BF16)


def _group_major_perm():
    g, h, d = np.meshgrid(np.arange(GROUP), np.arange(N_KV_HEADS), np.arange(HEAD_DIM), indexing="ij")
    return ((h * GROUP + g) * HEAD_DIM + d).reshape(-1)


def _rms_rows(x):
    return x * lax.rsqrt(jnp.mean(x * x, axis=-1, keepdims=True) + EPS)


def _proj_kernel(*refs, segments, has_vt):
    if has_vt:
        x_ref, g_ref, w_ref, hsum_ref, gain_ref, wt_ref = refs[:6]
        out_refs = refs[6:]
    else:
        x_ref, g_ref, w_ref, hsum_ref, gain_ref = refs[:5]
        out_refs = refs[5:]
    xn = (_rms_rows(x_ref[...]) * g_ref[...]).astype(BF16)
    y = jnp.dot(xn, w_ref[...], preferred_element_type=F32)
    for (start, width, normed), o_ref in zip(segments, out_refs):
        for t in range(width // GROUP_WIDTH):
            lo = start + t * GROUP_WIDTH
            yt = y[:, lo:lo + GROUP_WIDTH]
            if normed:
                ss = jnp.dot((yt * yt).astype(BF16), hsum_ref[...], preferred_element_type=F32)
                yt = yt * lax.rsqrt(ss * (1.0 / HEAD_DIM) + EPS) * gain_ref[:, lo:lo + GROUP_WIDTH]
            o_ref[:, t * GROUP_WIDTH:(t + 1) * GROUP_WIDTH] = yt.astype(o_ref.dtype)
    if has_vt:
        vt = lax.dot_general(wt_ref[...], xn, (((1,), (1,)), ((), ())), preferred_element_type=F32)
        out_refs[-1][...] = vt.astype(out_refs[-1].dtype)


def _project(x, g, w, gains, segments, out_dtypes, wt=None, seq=None):
    n, d = x.shape
    c = w.shape[1]
    tile = min(ROW_TILE, n)
    assert n % tile == 0
    out_shape = [jax.ShapeDtypeStruct((n, width), dt) for (_, width, _), dt in zip(segments, out_dtypes)]
    out_specs = [pl.BlockSpec((tile, width), lambda i: (i, 0)) for (_, width, _) in segments]
    in_specs = [
        pl.BlockSpec((tile, d), lambda i: (i, 0)),
        _const_spec((1, d)),
        _const_spec((d, c)),
        _const_spec((GROUP_WIDTH, GROUP_WIDTH)),
        _const_spec((1, c)),
    ]
    args = [x, g.reshape(1, d), w, _head_sum_matrix(), gains.reshape(1, c)]
    if wt is not None:
        assert seq % tile == 0 and n % seq == 0
        per_seq = seq // tile
        in_specs.append(_const_spec(wt.shape))
        args.append(wt)
        out_shape.append(jax.ShapeDtypeStruct((n // seq, wt.shape[0], seq), BF16))
        out_specs.append(pl.BlockSpec((None, wt.shape[0], tile), lambda i: (i // per_seq, 0, i % per_seq)))
    return pl.pallas_call(
        functools.partial(_proj_kernel, segments=tuple(segments), has_vt=wt is not None),
        grid=(n // tile,),
        in_specs=in_specs,
        out_specs=out_specs,
        out_shape=out_shape,
        compiler_params=pltpu.CompilerParams(
            dimension_semantics=("arbitrary",), vmem_limit_bytes=VMEM_LIMIT),
        name="project",
    )(*args)


def _band_attn_kernel(*refs, prev, n_pairs, has_sink):
    if has_sink:
        q_ref, kp_ref, kc_ref, vtp_ref, vtc_ref, bias_ref, sink_ref, o_ref, kcat, vt3, s_buf0, s_buf1, ot_prev = refs
    else:
        q_ref, kp_ref, kc_ref, vtp_ref, vtc_ref, bias_ref, o_ref, kcat, vt3, s_buf0, s_buf1, ot_prev = refs
        sink_ref = None
    s_bufs = (s_buf0, s_buf1)
    tile = n_pairs * PAIR
    band = prev + PAIR
    n_cols = 2 * GROUP * CHUNK

    kcat[0:prev, :] = kp_ref[...]
    kcat[prev:prev + tile, :] = kc_ref[...]
    ones = jnp.ones((ONES_ROWS, LANES), BF16)
    blocks = [vtp_ref[:, b * LANES:(b + 1) * LANES] for b in range(prev // LANES)]
    blocks += [vtc_ref[:, b * LANES:(b + 1) * LANES] for b in range(tile // LANES)]
    for b, blk in enumerate(blocks):
        for h in range(N_KV_HEADS):
            vt3[b, h, 0:HEAD_DIM, :] = blk[h * HEAD_DIM:(h + 1) * HEAD_DIM, :]
            vt3[b, h, HEAD_DIM:HEAD_DIM + ONES_ROWS, :] = ones

    lane_head = lax.broadcasted_iota(jnp.int32, (CHUNK, KV_WIDTH), 1) // HEAD_DIM
    head_lanes = [jnp.where(lane_head == h, 1.0, 0.0).astype(BF16) for h in range(N_KV_HEADS)]

    def scores(j, h, masked, s_buf):
        r0 = pl.multiple_of(j * PAIR, PAIR)
        qp = q_ref[pl.ds(r0, PAIR), :]
        kb = kcat[pl.ds(r0, band), :]
        qs = jnp.concatenate(
            [qp[c * CHUNK:(c + 1) * CHUNK, g * KV_WIDTH:(g + 1) * KV_WIDTH] * head_lanes[h]
             for c in range(2) for g in range(GROUP)], axis=0)
        s = lax.dot_general(kb, qs, (((1,), (1,)), ((), ())), preferred_element_type=F32)
        s = s + bias_ref[h]
        if masked:
            valid = lax.broadcasted_iota(jnp.int32, (band, n_cols), 0) >= prev - r0
            s = jnp.where(valid, s, NEG)
        s_buf[...] = s
        return jnp.max(s, axis=0, keepdims=True)

    def softmax_pv(m, s_buf, j, h):
        if has_sink:
            m = jnp.maximum(m, sink_ref[h])
        ot = jnp.zeros((HEAD_DIM + ONES_ROWS, n_cols), F32)
        n_blocks = band // LANES
        for b in range(0, n_blocks, 2):
            width = min(2, n_blocks - b) * LANES
            eb = jnp.exp2(s_buf[b * LANES:b * LANES + width, :] - m).astype(BF16)
            vt = jnp.concatenate([vt3[j + b + i, h] for i in range(width // LANES)], axis=1)
            ot = ot + jnp.dot(vt, eb, preferred_element_type=F32)
        denom = ot[HEAD_DIM:HEAD_DIM + 1, :]
        if has_sink:
            denom = denom + jnp.exp2(sink_ref[h] - m)
        return ot[0:HEAD_DIM, :] * (1.0 / denom)

    def run(masked):
        m_first = scores(0, 0, masked, s_bufs[0])
        ot_prev[...] = jnp.zeros(ot_prev.shape, F32)

        def write_out(j):
            r0 = pl.multiple_of(j * PAIR, PAIR)
            for c in range(2):
                oc = ot_prev[:, c * GROUP_WIDTH:(c + 1) * GROUP_WIDTH].T
                for g in range(GROUP):
                    o_ref[pl.ds(r0 + c * CHUNK, CHUNK), g * KV_WIDTH:(g + 1) * KV_WIDTH] = (
                        oc[g * CHUNK:(g + 1) * CHUNK, :].astype(o_ref.dtype))

        def body(j, m):
            write_out(jnp.maximum(j - 1, 0))
            for h in range(N_KV_HEADS):
                if h + 1 < N_KV_HEADS:
                    m_ahead = scores(j, h + 1, masked, s_bufs[(h + 1) % 2])
                else:
                    m_ahead = scores(jnp.minimum(j + 1, n_pairs - 1), 0, masked, s_bufs[0])
                ot_prev[h * HEAD_DIM:(h + 1) * HEAD_DIM, :] = softmax_pv(m, s_bufs[h % 2], j, h)
                m = m_ahead
            return m

        lax.fori_loop(0, n_pairs, body, m_first, unroll=PAIR_UNROLL)
        write_out(n_pairs - 1)

    @pl.when(pl.program_id(1) == 0)
    def _():
        run(True)

    @pl.when(pl.program_id(1) > 0)
    def _():
        run(False)


def _prompt_attention(q, k, vt, bias, sink, left_chunks):
    b, s, dq = q.shape
    prev = left_chunks * CHUNK
    tile = min(ATTN_TILE, s)
    band = prev + PAIR
    assert s % tile == 0 and tile % prev == 0 and prev % LANES == 0 and tile % PAIR == 0
    assert bias.shape == (N_KV_HEADS, band, 2 * GROUP * CHUNK)
    ratio = tile // prev
    has_sink = sink is not None
    cur_rows = lambda bi, i: (bi, i, 0)
    prev_rows = lambda bi, i: (bi, jnp.maximum(i * ratio - 1, 0), 0)
    cur_cols = lambda bi, i: (bi, 0, i)
    prev_cols = lambda bi, i: (bi, 0, jnp.maximum(i * ratio - 1, 0))
    in_specs = [
        pl.BlockSpec((None, tile, dq), cur_rows),
        pl.BlockSpec((None, prev, KV_WIDTH), prev_rows),
        pl.BlockSpec((None, tile, KV_WIDTH), cur_rows),
        pl.BlockSpec((None, KV_WIDTH, prev), prev_cols),
        pl.BlockSpec((None, KV_WIDTH, tile), cur_cols),
        _const_spec(bias.shape),
    ]
    args = [q, k, k, vt, vt, bias]
    if has_sink:
        in_specs.append(_const_spec(sink.shape))
        args.append(sink)
    return pl.pallas_call(
        functools.partial(_band_attn_kernel, prev=prev, n_pairs=tile // PAIR, has_sink=has_sink),
        grid=(b, s // tile),
        in_specs=in_specs,
        out_specs=pl.BlockSpec((None, tile, dq), cur_rows),
        out_shape=jax.ShapeDtypeStruct((b, s, dq), BF16),
        scratch_shapes=[
            pltpu.VMEM((prev + tile, KV_WIDTH), BF16),
            pltpu.VMEM(((prev + tile) // LANES, N_KV_HEADS, HEAD_DIM + ONES_ROWS, LANES), BF16),
            pltpu.VMEM((band, 2 * GROUP * CHUNK), F32),
            pltpu.VMEM((band, 2 * GROUP * CHUNK), F32),
            pltpu.VMEM((N_KV_HEADS * HEAD_DIM, 2 * GROUP * CHUNK), F32),
        ],
        compiler_params=pltpu.CompilerParams(
            dimension_semantics=("arbitrary", "arbitrary"), vmem_limit_bytes=VMEM_LIMIT),
        name="band_attention",
    )(*args)


def _pair_bias(per_head):
    h, tq, tk = per_head.shape
    neg = jnp.full((h, tq, CHUNK), NEG, F32)
    both = jnp.stack([jnp.concatenate([per_head, neg], axis=-1),
                      jnp.concatenate([neg, per_head], axis=-1)], axis=1)
    both = both.reshape(N_KV_HEADS, GROUP, 2, tq, tk + CHUNK)
    return both.transpose(0, 4, 2, 1, 3).reshape(N_KV_HEADS, tk + CHUNK, 2 * GROUP * tq)


def _pair_sink(sink):
    s = (sink.astype(F32) * LOG2E).reshape(N_KV_HEADS, 1, GROUP, 1)
    return jnp.broadcast_to(s, (N_KV_HEADS, 2, GROUP, CHUNK)).reshape(N_KV_HEADS, 1, 2 * GROUP * CHUNK)


def _sample_attn_kernel(*refs, tq, has_sink):
    if has_sink:
        q_ref, kp_ref, kc_ref, vp_ref, vc_ref, bias_ref, sink_ref, o_ref = refs
    else:
        q_ref, kp_ref, kc_ref, vp_ref, vc_ref, bias_ref, o_ref = refs
        sink_ref = None
    kcat = jnp.concatenate([kp_ref[...], kc_ref[...]], axis=0).astype(BF16)
    vcat = jnp.concatenate([vp_ref[...], vc_ref[...]], axis=0).astype(BF16)
    lane_head = lax.broadcasted_iota(jnp.int32, (tq, KV_WIDTH), 1) // HEAD_DIM
    qc = q_ref[...]
    qs = jnp.concatenate(
        [jnp.where(lane_head == h, qc[:, g * KV_WIDTH:(g + 1) * KV_WIDTH], 0)
         for h in range(N_KV_HEADS) for g in range(GROUP)], axis=0).astype(BF16)
    s = lax.dot_general(qs, kcat, (((1,), (1,)), ((), ())), preferred_element_type=F32)
    s = s + bias_ref[...]
    m = jnp.max(s, axis=-1, keepdims=True)
    if has_sink:
        m = jnp.maximum(m, sink_ref[...])
    e = jnp.exp2(s - m)
    denom = jnp.sum(e, axis=-1, keepdims=True)
    if has_sink:
        denom = denom + jnp.exp2(sink_ref[...] - m)
    p = (e * (1.0 / denom)).astype(BF16)
    o_all = jnp.dot(p, vcat, preferred_element_type=F32)
    for g in range(GROUP):
        o = jnp.zeros((tq, KV_WIDTH), F32)
        for h in range(N_KV_HEADS):
            r0 = (h * GROUP + g) * tq
            o = jnp.where(lane_head == h, o_all[r0:r0 + tq, :], o)
        o_ref[:, g * KV_WIDTH:(g + 1) * KV_WIDTH] = o.astype(o_ref.dtype)


def _sample_attention(q, k_cache, k_new, v_cache, v_new, bias, sink):
    b, t, dq = q.shape
    prev = k_cache.shape[1]
    assert bias.shape == (N_HEADS * t, prev + t)
    has_sink = sink is not None
    row_map = lambda bi: (bi, 0, 0)
    in_specs = [
        pl.BlockSpec((None, t, dq), row_map),
        pl.BlockSpec((None, prev, KV_WIDTH), row_map),
        pl.BlockSpec((None, t, KV_WIDTH), row_map),
        pl.BlockSpec((None, prev, KV_WIDTH), row_map),
        pl.BlockSpec((None, t, KV_WIDTH), row_map),
        _const_spec(bias.shape),
    ]
    args = [q, k_cache, k_new, v_cache, v_new, bias]
    if has_sink:
        in_specs.append(_const_spec(sink.shape))
        args.append(sink)
    return pl.pallas_call(
        functools.partial(_sample_attn_kernel, tq=t, has_sink=has_sink),
        grid=(b,),
        in_specs=in_specs,
        out_specs=pl.BlockSpec((None, t, dq), row_map),
        out_shape=jax.ShapeDtypeStruct((b, t, dq), BF16),
        compiler_params=pltpu.CompilerParams(
            dimension_semantics=("arbitrary",), vmem_limit_bytes=VMEM_LIMIT),
        name="sample_attention",
    )(*args)


def _out_mlp_kernel(x_ref, a_ref, wo_ref, g_ref, wup_ref, wdn_ref, o_ref, *, ff_chunks):
    h = x_ref[...] + jnp.dot(a_ref[...], wo_ref[...], preferred_element_type=F32)
    hn = (_rms_rows(h) * g_ref[...]).astype(BF16)
    d_ff = wup_ref.shape[1]
    step = d_ff // ff_chunks
    acc = h
    for c in range(ff_chunks):
        u = jnp.dot(hn, wup_ref[:, c * step:(c + 1) * step], preferred_element_type=F32)
        u = jnp.square(jnp.maximum(u, 0.0)).astype(BF16)
        acc = acc + jnp.dot(u, wdn_ref[c * step:(c + 1) * step, :], preferred_element_type=F32)
    o_ref[...] = acc


def _out_mlp(x, a, wo, g, wup, wdn):
    n, d = x.shape
    d_ff = wup.shape[1]
    tile = min(MLP_TILE, n)
    assert n % tile == 0
    row_spec = pl.BlockSpec((tile, d), lambda i: (i, 0))
    return pl.pallas_call(
        functools.partial(_out_mlp_kernel, ff_chunks=MLP_FF_CHUNKS),
        grid=(n // tile,),
        in_specs=[
            row_spec,
            row_spec,
            _const_spec((d, d)),
            _const_spec((1, d)),
            _const_spec((d, d_ff)),
            _const_spec((d_ff, d)),
        ],
        out_specs=row_spec,
        out_shape=jax.ShapeDtypeStruct((n, d), F32),
        compiler_params=pltpu.CompilerParams(
            dimension_semantics=("arbitrary",), vmem_limit_bytes=VMEM_LIMIT),
        name="out_proj_mlp",
    )(x, a, wo, g.reshape(1, d), wup, wdn)


def _rel_bias_kernel(tab_ref, o_ref, *, tk, delta):
    n_pad = tab_ref.shape[-1]
    r = lax.broadcasted_iota(jnp.int32, (n_pad, tk), 0)
    j = lax.broadcasted_iota(jnp.int32, (n_pad, tk), 1)
    for qi in range(o_ref.shape[0]):
        q = pl.program_id(1) * o_ref.shape[0] + qi
        idx = jnp.clip(q + delta - j, -MAX_REL, MAX_REL) + MAX_REL
        onehot = jnp.where(r == idx, 1.0, 0.0).astype(BF16)
        acc = jnp.dot(tab_ref[0], onehot, preferred_element_type=F32)
        acc = acc + jnp.dot(tab_ref[1], onehot, preferred_element_type=F32)
        acc = acc + jnp.dot(tab_ref[2], onehot, preferred_element_type=F32)
        o_ref[qi] = acc


def _rel_bias(table, tq, tk, delta):
    n_layers, h, n_rel = table.shape
    n_pad = -(-n_rel // LANES) * LANES
    t = jnp.pad(table, ((0, 0), (0, 0), (0, n_pad - n_rel)))
    hi = t.astype(BF16)
    mid = (t - hi.astype(F32)).astype(BF16)
    lo = (t - hi.astype(F32) - mid.astype(F32)).astype(BF16)
    parts = jnp.stack([hi, mid, lo], axis=1)
    return pl.pallas_call(
        functools.partial(_rel_bias_kernel, tk=tk, delta=delta),
        grid=(n_layers, tq // REL_BIAS_ROWS),
        in_specs=[pl.BlockSpec((None, 3, h, n_pad), lambda l, q: (l, 0, 0, 0))],
        out_specs=pl.BlockSpec((None, REL_BIAS_ROWS, h, tk), lambda l, q: (l, q, 0, 0)),
        out_shape=jax.ShapeDtypeStruct((n_layers, tq, h, tk), F32),
        compiler_params=pltpu.CompilerParams(dimension_semantics=("arbitrary", "arbitrary")),
        name="rel_bias_table",
    )(parts)


def _np_band_mask(q_pos, k_pos, left_chunks):
    qc = q_pos[:, None] // CHUNK
    kc = k_pos[None, :] // CHUNK
    return (k_pos[None, :] >= 0) & (kc <= qc) & (kc >= qc - left_chunks)


def _alibi_bias(q_pos, k_pos, static_mask):
    slopes = (2.0 ** (-8.0 * np.arange(1, N_HEADS + 1, dtype=np.float32) / N_HEADS)).astype(np.float32)
    dist = np.abs(q_pos[:, None] - k_pos[None, :]).astype(np.float32)
    bias = -slopes[:, None, None] * dist[None] * np.float32(LOG2E)
    if static_mask:
        bias = np.where(_np_band_mask(q_pos, k_pos, LEFT_CHUNKS_A)[None], bias, np.float32(NEG))
    return jnp.asarray(bias.astype(np.float32))


def _row_sink(sink, tq):
    return jnp.repeat(sink.astype(F32) * LOG2E, tq).reshape(N_HEADS * tq, 1)


def kernel(x_prompt, x_sample, cache_k_a, cache_v_a, cache_k_b, cache_v_b, g_attn, g_mlp, w_qkv_a,
           g_q_a, g_k_a, sink_a, w_o_a, g_kv, w_kv, g_k_b, w_q_b, g_q_b, rel_bias_b, w_o_b, w_up, w_down):
    batch, seq, d = x_prompt.shape
    dec_batch, t_new, _ = x_sample.shape
    n_layers_a = w_qkv_a.shape[0]
    n_layers_b = w_q_b.shape[0]
    len_a = cache_k_a.shape[2]
    len_b = cache_k_b.shape[1]
    keep_a = min(LEFT_CHUNKS_A * CHUNK, seq)
    keep_b = min(LEFT_CHUNKS_B * CHUNK, seq)
    dq = N_HEADS * HEAD_DIM
    q_scale = HEAD_DIM ** -0.5 * LOG2E
    perm = _group_major_perm()
    qkv_cols = np.concatenate([perm, np.arange(dq, dq + 2 * KV_WIDTH)])

    hp = x_prompt.reshape(batch * seq, d)
    hs = x_sample.reshape(dec_batch * t_new, d)

    pos_s = PAST_LEN + np.arange(t_new)
    kpos_a = np.concatenate([PAST_LEN - len_a + np.arange(len_a), pos_s])
    kpos_b = np.concatenate([PAST_LEN - len_b + np.arange(len_b), pos_s])
    chunk_q = np.arange(CHUNK)
    bias_a_prompt = _pair_bias(_alibi_bias(chunk_q, np.arange(-LEFT_CHUNKS_A * CHUNK, CHUNK), static_mask=False))
    bias_a_sample = _alibi_bias(pos_s, kpos_a, static_mask=True).reshape(N_HEADS * t_new, -1)

    ones_kv = jnp.ones((KV_WIDTH,), F32)
    seg_a_p = [(0, dq, True), (dq, KV_WIDTH, True)]
    seg_a_s = seg_a_p + [(dq + KV_WIDTH, KV_WIDTH, False)]
    seg_kv = [(0, KV_WIDTH, True), (KV_WIDTH, KV_WIDTH, False)]
    last_rows = lambda h, keep: h.reshape(batch, seq, d)[:, seq - keep:].reshape(batch * keep, d)
    kv4 = lambda a, n, t: a.reshape(n, t, N_KV_HEADS, HEAD_DIM)

    ka_p, va_p, ka_s, va_s = [], [], [], []
    for i in range(n_layers_a):
        w = w_qkv_a[i][:, qkv_cols].astype(BF16)
        wvt = w_qkv_a[i][:, dq + KV_WIDTH:].T.astype(BF16)
        gains = jnp.concatenate([jnp.tile(g_q_a[i] * q_scale, N_HEADS), jnp.tile(g_k_a[i], N_KV_HEADS), ones_kv])
        wo = w_o_a[i][perm, :].astype(BF16)
        wup = w_up[i].astype(BF16)
        wdn = w_down[i].astype(BF16)

        qp, kp16, vtp = _project(hp, g_attn[i], w[:, :dq + KV_WIDTH], gains[:dq + KV_WIDTH], seg_a_p,
                                 (BF16, BF16), wt=wvt, seq=seq)
        ap = _prompt_attention(qp.reshape(batch, seq, dq), kp16.reshape(batch, seq, KV_WIDTH), vtp,
                               bias_a_prompt, _pair_sink(sink_a[i]), LEFT_CHUNKS_A)
        kp, vp = _project(last_rows(hp, keep_a), g_attn[i], w[:, dq:], gains[dq:], seg_kv, (F32, F32))
        ka_p.append(kv4(kp, batch, keep_a))
        va_p.append(kv4(vp, batch, keep_a))

        qs, ks, vs = _project(hs, g_attn[i], w, gains, seg_a_s, (BF16, F32, F32))
        ks3 = ks.reshape(dec_batch, t_new, KV_WIDTH)
        vs3 = vs.reshape(dec_batch, t_new, KV_WIDTH)
        as_ = _sample_attention(qs.reshape(dec_batch, t_new, dq),
                                cache_k_a[i].reshape(dec_batch, len_a, KV_WIDTH), ks3,
                                cache_v_a[i].reshape(dec_batch, len_a, KV_WIDTH), vs3,
                                bias_a_sample, _row_sink(sink_a[i], t_new))
        ka_s.append(ks3.reshape(dec_batch, t_new, N_KV_HEADS, HEAD_DIM))
        va_s.append(vs3.reshape(dec_batch, t_new, N_KV_HEADS, HEAD_DIM))

        hp = _out_mlp(hp, ap.reshape(batch * seq, dq), wo, g_mlp[i], wup, wdn)
        hs = _out_mlp(hs, as_.reshape(dec_batch * t_new, dq), wo, g_mlp[i], wup, wdn)

    wkv = w_kv.astype(BF16)
    gains_kv = jnp.concatenate([jnp.tile(g_k_b, N_KV_HEADS), ones_kv])
    kb_p16, vtb_p = _project(hp, g_kv, wkv[:, :KV_WIDTH], gains_kv[:KV_WIDTH], seg_kv[:1], (BF16,),
                             wt=wkv[:, KV_WIDTH:].T, seq=seq)
    kb_p, vb_p = _project(last_rows(hp, keep_b), g_kv, wkv, gains_kv, seg_kv, (F32, F32))
    kb_s, vb_s = _project(hs, g_kv, wkv, gains_kv, seg_kv, (F32, F32))
    kb_p16 = kb_p16.reshape(batch, seq, KV_WIDTH)
    kb_s3 = kb_s.reshape(dec_batch, t_new, KV_WIDTH)
    vb_s3 = vb_s.reshape(dec_batch, t_new, KV_WIDTH)
    cache_kb = cache_k_b.reshape(dec_batch, len_b, KV_WIDTH)
    cache_vb = cache_v_b.reshape(dec_batch, len_b, KV_WIDTH)

    pad_b = LEFT_CHUNKS_B * CHUNK
    tk_b = pad_b + CHUNK
    assert int(pos_s[0] - kpos_b[0]) == pad_b and t_new <= CHUNK and len_b + t_new <= tk_b
    assert np.all(np.diff(kpos_b) == 1)
    rel = _rel_bias(rel_bias_b.astype(F32) * LOG2E, CHUNK, tk_b, pad_b)
    rel = rel.transpose(0, 2, 1, 3)
    mask_s = np.where(_np_band_mask(pos_s, kpos_b, LEFT_CHUNKS_B), 0.0, NEG).astype(np.float32)

    seg_q = [(0, dq, True)]
    for j in range(n_layers_b):
        layer = n_layers_a + j
        wq = w_q_b[j][:, perm].astype(BF16)
        gains = jnp.tile(g_q_b[j] * q_scale, N_HEADS)
        wo = w_o_b[j][perm, :].astype(BF16)
        wup = w_up[layer].astype(BF16)
        wdn = w_down[layer].astype(BF16)
        bias_p = _pair_bias(rel[j])
        bias_s = (rel[j][:, :t_new, :len_b + t_new] + mask_s[None]).reshape(N_HEADS * t_new, -1)

        (qp,) = _project(hp, g_attn[layer], wq, gains, seg_q, (BF16,))
        ap = _prompt_attention(qp.reshape(batch, seq, dq), kb_p16, vtb_p, bias_p, None, LEFT_CHUNKS_B)
        (qs,) = _project(hs, g_attn[layer], wq, gains, seg_q, (BF16,))
        as_ = _sample_attention(qs.reshape(dec_batch, t_new, dq), cache_kb, kb_s3, cache_vb, vb_s3, bias_s, None)

        hp = _out_mlp(hp, ap.reshape(batch * seq, dq), wo, g_mlp[layer], wup, wdn)
        hs = _out_mlp(hs, as_.reshape(dec_batch * t_new, dq), wo, g_mlp[layer], wup, wdn)

    return (hp.reshape(batch, seq, d), hs.reshape(dec_batch, t_new, d),
            jnp.stack(ka_p), jnp.stack(va_p),
            kv4(kb_p, batch, keep_b), kv4(vb_p, batch, keep_b),
            jnp.stack(ka_s), jnp.stack(va_s),
            kv4(kb_s3, dec_batch, t_new), kv4(vb_s3, dec_batch, t_new))
```

```python
import functools
import math

import numpy as np
import jax
import jax.numpy as jnp
from jax import lax
from jax.experimental import pallas as pl
from jax.experimental.pallas import tpu as pltpu

CHUNK = 64
HEAD_DIM = 64
N_KV_HEADS = 4
GROUP = 4
N_HEADS = N_KV_HEADS * GROUP
LEFT_CHUNKS_A = 2
LEFT_CHUNKS_B = 8
MAX_REL = 128
N_REL = 2 * MAX_REL + 1
PAST_LEN = 1024
EPS = 1e-6
NEG = -1e30
LOG2E = math.log2(math.e)

LANES = 128
KV_WIDTH = N_KV_HEADS * HEAD_DIM
GROUP_WIDTH = GROUP * HEAD_DIM
PAIR = 2 * CHUNK
ONES_ROWS = 16
REL_BIAS_ROWS = 8
ROW_TILE = 1024
MLP_TILE = 1024
MLP_FF_CHUNKS = 8
ATTN_TILE = 2048
PAIR_UNROLL = 4
VMEM_LIMIT = 56 * 1024 * 1024

BF16 = jnp.bfloat16
F32 = jnp.float32


def _const_spec(shape):
    zeros = (0,) * len(shape)
    return pl.BlockSpec(shape, lambda *_: zeros, pipeline_mode=pl.Buffered(1))


def _head_sum_matrix():
    idx = np.arange(GROUP_WIDTH) // HEAD_DIM
    return jnp.asarray(idx[:, None] == idx[None, :], dtype=BF16)


def _group_major_perm():
    g, h, d = np.meshgrid(np.arange(GROUP), np.arange(N_KV_HEADS), np.arange(HEAD_DIM), indexing="ij")
    return ((h * GROUP + g) * HEAD_DIM + d).reshape(-1)


def _rms_rows(x):
    return x * lax.rsqrt(jnp.mean(x * x, axis=-1, keepdims=True) + EPS)


def _proj_kernel(*refs, segments, qt_rows, vt_rows):
    refs = list(refs)
    x_ref, g_ref, hsum_ref = refs[:3]
    del refs[:3]
    if segments:
        w_ref, gain_ref = refs[:2]
        del refs[:2]
    if qt_rows or vt_rows:
        wt_ref = refs.pop(0)
    if qt_rows:
        gain_t_ref = refs.pop(0)
    out_refs = refs
    xn = (_rms_rows(x_ref[...]) * g_ref[...]).astype(BF16)
    if segments:
        if w_ref.shape[1] == GROUP_WIDTH:
            half = xn.shape[0] // 2
            y = jnp.concatenate([jnp.dot(xn[:half], w_ref[...], preferred_element_type=F32),
                                 jnp.dot(xn[half:], w_ref[...], preferred_element_type=F32)], axis=0)
        else:
            y = jnp.dot(xn, w_ref[...], preferred_element_type=F32)
        for (start, width, normed), o_ref in zip(segments, out_refs):
            for t in range(width // GROUP_WIDTH):
                lo = start + t * GROUP_WIDTH
                yt = y[:, lo:lo + GROUP_WIDTH]
                if normed:
                    ss = jnp.dot((yt * yt).astype(BF16), hsum_ref[...], preferred_element_type=F32)
                    yt = yt * lax.rsqrt(ss * (1.0 / HEAD_DIM) + EPS) * gain_ref[:, lo:lo + GROUP_WIDTH]
                o_ref[:, t * GROUP_WIDTH:(t + 1) * GROUP_WIDTH] = yt.astype(o_ref.dtype)
    if qt_rows or vt_rows:
        yt = lax.dot_general(wt_ref[...], xn, (((1,), (1,)), ((), ())), preferred_element_type=F32)
        if qt_rows:
            qt_ref = out_refs[len(segments)]
            for grp in range(qt_rows // GROUP_WIDTH):
                r0 = grp * GROUP_WIDTH
                blk = yt[r0:r0 + GROUP_WIDTH, :]
                ss = jnp.dot(hsum_ref[...], (blk * blk).astype(BF16), preferred_element_type=F32)
                scale = lax.rsqrt(ss * (1.0 / HEAD_DIM) + EPS)
                for lb in range(blk.shape[1] // LANES):
                    piece = (blk[:, lb * LANES:(lb + 1) * LANES] * scale[:, lb * LANES:(lb + 1) * LANES]
                             * gain_t_ref[r0:r0 + GROUP_WIDTH, :])
                    qt_ref[lb, r0:r0 + GROUP_WIDTH, :] = piece.astype(qt_ref.dtype)
        if vt_rows:
            out_refs[-1][...] = yt[qt_rows:qt_rows + vt_rows, :].astype(out_refs[-1].dtype)


def _project(x, g, w, gains, segments, out_dtypes, wt=None, gains_t=None, qt_rows=0, seq=None):
    n, d = x.shape
    tile = min(ROW_TILE, n)
    assert n % tile == 0
    out_shape = [jax.ShapeDtypeStruct((n, width), dt) for (_, width, _), dt in zip(segments, out_dtypes)]
    out_specs = [pl.BlockSpec((tile, width), lambda i: (i, 0)) for (_, width, _) in segments]
    in_specs = [
        pl.BlockSpec((tile, d), lambda i: (i, 0)),
        _const_spec((1, d)),
        _const_spec((GROUP_WIDTH, GROUP_WIDTH)),
    ]
    args = [x, g.reshape(1, d), _head_sum_matrix()]
    if segments:
        c = w.shape[1]
        in_specs += [_const_spec((d, c)), _const_spec((1, c))]
        args += [w, gains.reshape(1, c)]
    vt_rows = 0
    if wt is not None:
        vt_rows = wt.shape[0] - qt_rows
        in_specs.append(_const_spec(wt.shape))
        args.append(wt)
    if qt_rows:
        assert tile % LANES == 0
        in_specs.append(_const_spec((qt_rows, LANES)))
        args.append(jnp.broadcast_to(gains_t.astype(F32)[:, None], (qt_rows, LANES)))
        out_shape.append(jax.ShapeDtypeStruct((n // LANES, qt_rows, LANES), BF16))
        out_specs.append(pl.BlockSpec((tile // LANES, qt_rows, LANES), lambda i: (i, 0, 0)))
    if vt_rows:
        assert seq % tile == 0 and n % seq == 0
        per_seq = seq // tile
        out_shape.append(jax.ShapeDtypeStruct((n // seq, vt_rows, seq), BF16))
        out_specs.append(pl.BlockSpec((None, vt_rows, tile), lambda i: (i // per_seq, 0, i % per_seq)))
    return pl.pallas_call(
        functools.partial(_proj_kernel, segments=tuple(segments), qt_rows=qt_rows, vt_rows=vt_rows),
        grid=(n // tile,),
        in_specs=in_specs,
        out_specs=out_specs,
        out_shape=out_shape,
        compiler_params=pltpu.CompilerParams(
            dimension_semantics=("arbitrary",), vmem_limit_bytes=VMEM_LIMIT),
        name="project",
    )(*args)


def _band_attn_kernel(*refs, prev, n_pairs, has_sink):
    if has_sink:
        q_ref, kp_ref, kc_ref, vtp_ref, vtc_ref, bias_ref, sink_ref, o_ref, kcat, vt3, s_buf0, s_buf1, ot_prev = refs
    else:
        q_ref, kp_ref, kc_ref, vtp_ref, vtc_ref, bias_ref, o_ref, kcat, vt3, s_buf0, s_buf1, ot_prev = refs
        sink_ref = None
    s_bufs = (s_buf0, s_buf1)
    tile = n_pairs * PAIR
    band = prev + PAIR
    n_cols = 2 * GROUP * CHUNK

    kcat[0:prev, :] = kp_ref[...]
    kcat[prev:prev + tile, :] = kc_ref[...]
    ones = jnp.ones((ONES_ROWS, LANES), BF16)
    blocks = [vtp_ref[:, b * LANES:(b + 1) * LANES] for b in range(prev // LANES)]
    blocks += [vtc_ref[:, b * LANES:(b + 1) * LANES] for b in range(tile // LANES)]
    for b, blk in enumerate(blocks):
        for h in range(N_KV_HEADS):
            vt3[b, h, 0:HEAD_DIM, :] = blk[h * HEAD_DIM:(h + 1) * HEAD_DIM, :]
            vt3[b, h, HEAD_DIM:HEAD_DIM + ONES_ROWS, :] = ones

    def scores(j, h, masked, s_buf):
        r0 = pl.multiple_of(j * PAIR, PAIR)
        kb = kcat[pl.ds(r0, band), :]
        cols = []
        for g in range(GROUP):
            qt = q_ref[j, g * GROUP_WIDTH + h * HEAD_DIM:g * GROUP_WIDTH + (h + 1) * HEAD_DIM, :]
            parts = []
            if h > 0:
                parts.append(jnp.zeros((h * HEAD_DIM, PAIR), BF16))
            parts.append(qt)
            if h + 1 < N_KV_HEADS:
                parts.append(jnp.zeros(((N_KV_HEADS - 1 - h) * HEAD_DIM, PAIR), BF16))
            cols.append(jnp.concatenate(parts, axis=0))
        qs = jnp.concatenate(cols, axis=1)
        s = jnp.dot(kb, qs, preferred_element_type=F32)
        s = s + bias_ref[h]
        if masked:
            valid = lax.broadcasted_iota(jnp.int32, (band, n_cols), 0) >= prev - r0
            s = jnp.where(valid, s, NEG)
        s_buf[...] = s
        return jnp.max(s, axis=0, keepdims=True)

    def softmax_pv(m, s_buf, j, h):
        if has_sink:
            m = jnp.maximum(m, sink_ref[h])
        ot = jnp.zeros((HEAD_DIM + ONES_ROWS, n_cols), F32)
        n_blocks = band // LANES
        for b in range(0, n_blocks, 2):
            width = min(2, n_blocks - b) * LANES
            eb = jnp.exp2(s_buf[b * LANES:b * LANES + width, :] - m).astype(BF16)
            vt = jnp.concatenate([vt3[j + b + i, h] for i in range(width // LANES)], axis=1)
            ot = ot + jnp.dot(vt, eb, preferred_element_type=F32)
        denom = ot[HEAD_DIM:HEAD_DIM + 1, :]
        if has_sink:
            denom = denom + jnp.exp2(sink_ref[h] - m)
        return ot[0:HEAD_DIM, :] * (1.0 / denom)

    def run(masked):
        m_first = scores(0, 0, masked, s_bufs[0])
        ot_prev[...] = jnp.zeros(ot_prev.shape, F32)

        def write_out(j):
            r0 = pl.multiple_of(j * PAIR, PAIR)
            for g in range(GROUP):
                og = ot_prev[:, g * PAIR:(g + 1) * PAIR].T
                o_ref[pl.ds(r0, PAIR), g * KV_WIDTH:(g + 1) * KV_WIDTH] = og.astype(o_ref.dtype)

        def body(j, m):
            write_out(jnp.maximum(j - 1, 0))
            for h in range(N_KV_HEADS):
                if h + 1 < N_KV_HEADS:
                    m_ahead = scores(j, h + 1, masked, s_bufs[(h + 1) % 2])
                else:
                    m_ahead = scores(jnp.minimum(j + 1, n_pairs - 1), 0, masked, s_bufs[0])
                ot_prev[h * HEAD_DIM:(h + 1) * HEAD_DIM, :] = softmax_pv(m, s_bufs[h % 2], j, h)
                m = m_ahead
            return m

        lax.fori_loop(0, n_pairs, body, m_first, unroll=PAIR_UNROLL)
        write_out(n_pairs - 1)

    @pl.when(pl.program_id(1) == 0)
    def _():
        run(True)

    @pl.when(pl.program_id(1) > 0)
    def _():
        run(False)


def _prompt_attention(q, k, vt, bias, sink, left_chunks):
    b, n_blocks, dq, _ = q.shape
    s = n_blocks * LANES
    prev = left_chunks * CHUNK
    tile = min(ATTN_TILE, s)
    band = prev + PAIR
    assert s % tile == 0 and tile % prev == 0 and prev % LANES == 0 and tile % PAIR == 0
    assert bias.shape == (N_KV_HEADS, band, 2 * GROUP * CHUNK)
    ratio = tile // prev
    has_sink = sink is not None
    cur_rows = lambda bi, i: (bi, i, 0)
    prev_rows = lambda bi, i: (bi, jnp.maximum(i * ratio - 1, 0), 0)
    cur_cols = lambda bi, i: (bi, 0, i)
    prev_cols = lambda bi, i: (bi, 0, jnp.maximum(i * ratio - 1, 0))
    in_specs = [
        pl.BlockSpec((None, tile // LANES, dq, LANES), lambda bi, i: (bi, i, 0, 0)),
        pl.BlockSpec((None, prev, KV_WIDTH), prev_rows),
        pl.BlockSpec((None, tile, KV_WIDTH), cur_rows),
        pl.BlockSpec((None, KV_WIDTH, prev), prev_cols),
        pl.BlockSpec((None, KV_WIDTH, tile), cur_cols),
        _const_spec(bias.shape),
    ]
    args = [q, k, k, vt, vt, bias]
    if has_sink:
        in_specs.append(_const_spec(sink.shape))
        args.append(sink)
    return pl.pallas_call(
        functools.partial(_band_attn_kernel, prev=prev, n_pairs=tile // PAIR, has_sink=has_sink),
        grid=(b, s // tile),
        in_specs=in_specs,
        out_specs=pl.BlockSpec((None, tile, dq), cur_rows),
        out_shape=jax.ShapeDtypeStruct((b, s, dq), BF16),
        scratch_shapes=[
            pltpu.VMEM((prev + tile, KV_WIDTH), BF16),
            pltpu.VMEM(((prev + tile) // LANES, N_KV_HEADS, HEAD_DIM + ONES_ROWS, LANES), BF16),
            pltpu.VMEM((band, 2 * GROUP * CHUNK), F32),
            pltpu.VMEM((band, 2 * GROUP * CHUNK), F32),
            pltpu.VMEM((N_KV_HEADS * HEAD_DIM, 2 * GROUP * CHUNK), F32),
        ],
        compiler_params=pltpu.CompilerParams(
            dimension_semantics=("arbitrary", "arbitrary"), vmem_limit_bytes=VMEM_LIMIT),
        name="band_attention",
    )(*args)


def _pair_bias(per_head):
    h, tq, tk = per_head.shape
    neg = jnp.full((h, tq, CHUNK), NEG, F32)
    both = jnp.stack([jnp.concatenate([per_head, neg], axis=-1),
                      jnp.concatenate([neg, per_head], axis=-1)], axis=1)
    both = both.reshape(N_KV_HEADS, GROUP, 2, tq, tk + CHUNK)
    return both.transpose(0, 4, 1, 2, 3).reshape(N_KV_HEADS, tk + CHUNK, 2 * GROUP * tq)


def _pair_sink(sink):
    s = (sink.astype(F32) * LOG2E).reshape(N_KV_HEADS, GROUP, 1, 1)
    return jnp.broadcast_to(s, (N_KV_HEADS, GROUP, 2, CHUNK)).reshape(N_KV_HEADS, 1, 2 * GROUP * CHUNK)


def _sample_attn_kernel(*refs, tq, has_sink):
    if has_sink:
        q_ref, kp_ref, kc_ref, vp_ref, vc_ref, bias_ref, sink_ref, o_ref = refs
    else:
        q_ref, kp_ref, kc_ref, vp_ref, vc_ref, bias_ref, o_ref = refs
        sink_ref = None
    kcat = jnp.concatenate([kp_ref[...], kc_ref[...]], axis=0).astype(BF16)
    vcat = jnp.concatenate([vp_ref[...], vc_ref[...]], axis=0).astype(BF16)
    lane_head = lax.broadcasted_iota(jnp.int32, (tq, KV_WIDTH), 1) // HEAD_DIM
    qc = q_ref[...]
    qs = jnp.concatenate(
        [jnp.where(lane_head == h, qc[:, g * KV_WIDTH:(g + 1) * KV_WIDTH], 0)
         for h in range(N_KV_HEADS) for g in range(GROUP)], axis=0).astype(BF16)
    s = lax.dot_general(qs, kcat, (((1,), (1,)), ((), ())), preferred_element_type=F32)
    s = s + bias_ref[...]
    m = jnp.max(s, axis=-1, keepdims=True)
    if has_sink:
        m = jnp.maximum(m, sink_ref[...])
    e = jnp.exp2(s - m)
    denom = jnp.sum(e, axis=-1, keepdims=True)
    if has_sink:
        denom = denom + jnp.exp2(sink_ref[...] - m)
    p = (e * (1.0 / denom)).astype(BF16)
    o_all = jnp.dot(p, vcat, preferred_element_type=F32)
    for g in range(GROUP):
        o = jnp.zeros((tq, KV_WIDTH), F32)
        for h in range(N_KV_HEADS):
            r0 = (h * GROUP + g) * tq
            o = jnp.where(lane_head == h, o_all[r0:r0 + tq, :], o)
        o_ref[:, g * KV_WIDTH:(g + 1) * KV_WIDTH] = o.astype(o_ref.dtype)


def _sample_attention(q, k_cache, k_new, v_cache, v_new, bias, sink):
    b, t, dq = q.shape
    prev = k_cache.shape[1]
    assert bias.shape == (N_HEADS * t, prev + t)
    has_sink = sink is not None
    row_map = lambda bi: (bi, 0, 0)
    in_specs = [
        pl.BlockSpec((None, t, dq), row_map),
        pl.BlockSpec((None, prev, KV_WIDTH), row_map),
        pl.BlockSpec((None, t, KV_WIDTH), row_map),
        pl.BlockSpec((None, prev, KV_WIDTH), row_map),
        pl.BlockSpec((None, t, KV_WIDTH), row_map),
        _const_spec(bias.shape),
    ]
    args = [q, k_cache, k_new, v_cache, v_new, bias]
    if has_sink:
        in_specs.append(_const_spec(sink.shape))
        args.append(sink)
    return pl.pallas_call(
        functools.partial(_sample_attn_kernel, tq=t, has_sink=has_sink),
        grid=(b,),
        in_specs=in_specs,
        out_specs=pl.BlockSpec((None, t, dq), row_map),
        out_shape=jax.ShapeDtypeStruct((b, t, dq), BF16),
        compiler_params=pltpu.CompilerParams(
            dimension_semantics=("arbitrary",), vmem_limit_bytes=VMEM_LIMIT),
        name="sample_attention",
    )(*args)


def _out_mlp_kernel(x_ref, a_ref, wo_ref, g_ref, wup_ref, wdn_ref, o_ref, *, ff_chunks):
    h = x_ref[...] + jnp.dot(a_ref[...], wo_ref[...], preferred_element_type=F32)
    hn = (_rms_rows(h) * g_ref[...]).astype(BF16)
    d_ff = wup_ref.shape[1]
    step = d_ff // ff_chunks
    acc = h
    for c in range(ff_chunks):
        u = jnp.dot(hn, wup_ref[:, c * step:(c + 1) * step], preferred_element_type=F32)
        u = jnp.square(jnp.maximum(u, 0.0)).astype(BF16)
        acc = acc + jnp.dot(u, wdn_ref[c * step:(c + 1) * step, :], preferred_element_type=F32)
    o_ref[...] = acc


def _out_mlp(x, a, wo, g, wup, wdn):
    n, d = x.shape
    d_ff = wup.shape[1]
    tile = min(MLP_TILE, n)
    assert n % tile == 0
    row_spec = pl.BlockSpec((tile, d), lambda i: (i, 0))
    return pl.pallas_call(
        functools.partial(_out_mlp_kernel, ff_chunks=MLP_FF_CHUNKS),
        grid=(n // tile,),
        in_specs=[
            row_spec,
            row_spec,
            _const_spec((d, d)),
            _const_spec((1, d)),
            _const_spec((d, d_ff)),
            _const_spec((d_ff, d)),
        ],
        out_specs=row_spec,
        out_shape=jax.ShapeDtypeStruct((n, d), F32),
        compiler_params=pltpu.CompilerParams(
            dimension_semantics=("arbitrary",), vmem_limit_bytes=VMEM_LIMIT),
        name="out_proj_mlp",
    )(x, a, wo, g.reshape(1, d), wup, wdn)


def _rel_bias_kernel(tab_ref, o_ref, *, tk, delta):
    n_pad = tab_ref.shape[-1]
    r = lax.broadcasted_iota(jnp.int32, (n_pad, tk), 0)
    j = lax.broadcasted_iota(jnp.int32, (n_pad, tk), 1)
    for qi in range(o_ref.shape[0]):
        q = pl.program_id(1) * o_ref.shape[0] + qi
        idx = jnp.clip(q + delta - j, -MAX_REL, MAX_REL) + MAX_REL
        onehot = jnp.where(r == idx, 1.0, 0.0).astype(BF16)
        acc = jnp.dot(tab_ref[0], onehot, preferred_element_type=F32)
        acc = acc + jnp.dot(tab_ref[1], onehot, preferred_element_type=F32)
        acc = acc + jnp.dot(tab_ref[2], onehot, preferred_element_type=F32)
        o_ref[qi] = acc


def _rel_bias(table, tq, tk, delta):
    n_layers, h, n_rel = table.shape
    n_pad = -(-n_rel // LANES) * LANES
    t = jnp.pad(table, ((0, 0), (0, 0), (0, n_pad - n_rel)))
    hi = t.astype(BF16)
    mid = (t - hi.astype(F32)).astype(BF16)
    lo = (t - hi.astype(F32) - mid.astype(F32)).astype(BF16)
    parts = jnp.stack([hi, mid, lo], axis=1)
    return pl.pallas_call(
        functools.partial(_rel_bias_kernel, tk=tk, delta=delta),
        grid=(n_layers, tq // REL_BIAS_ROWS),
        in_specs=[pl.BlockSpec((None, 3, h, n_pad), lambda l, q: (l, 0, 0, 0))],
        out_specs=pl.BlockSpec((None, REL_BIAS_ROWS, h, tk), lambda l, q: (l, q, 0, 0)),
        out_shape=jax.ShapeDtypeStruct((n_layers, tq, h, tk), F32),
        compiler_params=pltpu.CompilerParams(dimension_semantics=("arbitrary", "arbitrary")),
        name="rel_bias_table",
    )(parts)


def _np_band_mask(q_pos, k_pos, left_chunks):
    qc = q_pos[:, None] // CHUNK
    kc = k_pos[None, :] // CHUNK
    return (k_pos[None, :] >= 0) & (kc <= qc) & (kc >= qc - left_chunks)


def _alibi_bias(q_pos, k_pos, static_mask):
    slopes = (2.0 ** (-8.0 * np.arange(1, N_HEADS + 1, dtype=np.float32) / N_HEADS)).astype(np.float32)
    dist = np.abs(q_pos[:, None] - k_pos[None, :]).astype(np.float32)
    bias = -slopes[:, None, None] * dist[None] * np.float32(LOG2E)
    if static_mask:
        bias = np.where(_np_band_mask(q_pos, k_pos, LEFT_CHUNKS_A)[None], bias, np.float32(NEG))
    return jnp.asarray(bias.astype(np.float32))


def _row_sink(sink, tq):
    return jnp.repeat(sink.astype(F32) * LOG2E, tq).reshape(N_HEADS * tq, 1)


def kernel(x_prompt, x_sample, cache_k_a, cache_v_a, cache_k_b, cache_v_b, g_attn, g_mlp, w_qkv_a,
           g_q_a, g_k_a, sink_a, w_o_a, g_kv, w_kv, g_k_b, w_q_b, g_q_b, rel_bias_b, w_o_b, w_up, w_down):
    batch, seq, d = x_prompt.shape
    dec_batch, t_new, _ = x_sample.shape
    n_layers_a = w_qkv_a.shape[0]
    n_layers_b = w_q_b.shape[0]
    len_a = cache_k_a.shape[2]
    len_b = cache_k_b.shape[1]
    keep_a = min(LEFT_CHUNKS_A * CHUNK, seq)
    keep_b = min(LEFT_CHUNKS_B * CHUNK, seq)
    dq = N_HEADS * HEAD_DIM
    q_scale = HEAD_DIM ** -0.5 * LOG2E
    perm = _group_major_perm()
    qkv_cols = np.concatenate([perm, np.arange(dq, dq + 2 * KV_WIDTH)])
    qv_cols = np.concatenate([perm, np.arange(dq + KV_WIDTH, dq + 2 * KV_WIDTH)])

    hp = x_prompt.reshape(batch * seq, d)
    hs = x_sample.reshape(dec_batch * t_new, d)

    pos_s = PAST_LEN + np.arange(t_new)
    kpos_a = np.concatenate([PAST_LEN - len_a + np.arange(len_a), pos_s])
    kpos_b = np.concatenate([PAST_LEN - len_b + np.arange(len_b), pos_s])
    chunk_q = np.arange(CHUNK)
    bias_a_prompt = _pair_bias(_alibi_bias(chunk_q, np.arange(-LEFT_CHUNKS_A * CHUNK, CHUNK), static_mask=False))
    bias_a_sample = _alibi_bias(pos_s, kpos_a, static_mask=True).reshape(N_HEADS * t_new, -1)

    ones_kv = jnp.ones((KV_WIDTH,), F32)
    seg_a_p = [(0, dq, True), (dq, KV_WIDTH, True)]
    seg_a_s = seg_a_p + [(dq + KV_WIDTH, KV_WIDTH, False)]
    seg_kv = [(0, KV_WIDTH, True), (KV_WIDTH, KV_WIDTH, False)]
    last_rows = lambda h, keep: h.reshape(batch, seq, d)[:, seq - keep:].reshape(batch * keep, d)
    kv4 = lambda a, n, t: a.reshape(n, t, N_KV_HEADS, HEAD_DIM)

    ka_p, va_p, ka_s, va_s = [], [], [], []
    for i in range(n_layers_a):
        w = w_qkv_a[i][:, qkv_cols].astype(BF16)
        wqv_t = w_qkv_a[i][:, qv_cols].T.astype(BF16)
        gains = jnp.concatenate([jnp.tile(g_q_a[i] * q_scale, N_HEADS), jnp.tile(g_k_a[i], N_KV_HEADS), ones_kv])
        wo = w_o_a[i][perm, :].astype(BF16)
        wup = w_up[i].astype(BF16)
        wdn = w_down[i].astype(BF16)

        kp16, qtp, vtp = _project(hp, g_attn[i], w[:, dq:dq + KV_WIDTH], gains[dq:dq + KV_WIDTH], seg_kv[:1],
                                  (BF16,), wt=wqv_t, gains_t=gains[:dq], qt_rows=dq, seq=seq)
        ap = _prompt_attention(qtp.reshape(batch, seq // LANES, dq, LANES), kp16.reshape(batch, seq, KV_WIDTH),
                               vtp, bias_a_prompt, _pair_sink(sink_a[i]), LEFT_CHUNKS_A)
        kp, vp = _project(last_rows(hp, keep_a), g_attn[i], w[:, dq:], gains[dq:], seg_kv, (F32, F32))
        ka_p.append(kv4(kp, batch, keep_a))
        va_p.append(kv4(vp, batch, keep_a))

        qs, ks, vs = _project(hs, g_attn[i], w, gains, seg_a_s, (BF16, F32, F32))
        ks3 = ks.reshape(dec_batch, t_new, KV_WIDTH)
        vs3 = vs.reshape(dec_batch, t_new, KV_WIDTH)
        as_ = _sample_attention(qs.reshape(dec_batch, t_new, dq),
                                cache_k_a[i].reshape(dec_batch, len_a, KV_WIDTH), ks3,
                                cache_v_a[i].reshape(dec_batch, len_a, KV_WIDTH), vs3,
                                bias_a_sample, _row_sink(sink_a[i], t_new))
        ka_s.append(ks3.reshape(dec_batch, t_new, N_KV_HEADS, HEAD_DIM))
        va_s.append(vs3.reshape(dec_batch, t_new, N_KV_HEADS, HEAD_DIM))

        hp = _out_mlp(hp, ap.reshape(batch * seq, dq), wo, g_mlp[i], wup, wdn)
        hs = _out_mlp(hs, as_.reshape(dec_batch * t_new, dq), wo, g_mlp[i], wup, wdn)

    wkv = w_kv.astype(BF16)
    gains_kv = jnp.concatenate([jnp.tile(g_k_b, N_KV_HEADS), ones_kv])
    kb_p16, vtb_p = _project(hp, g_kv, wkv[:, :KV_WIDTH], gains_kv[:KV_WIDTH], seg_kv[:1], (BF16,),
                             wt=wkv[:, KV_WIDTH:].T, seq=seq)
    kb_p, vb_p = _project(last_rows(hp, keep_b), g_kv, wkv, gains_kv, seg_kv, (F32, F32))
    kb_s, vb_s = _project(hs, g_kv, wkv, gains_kv, seg_kv, (F32, F32))
    kb_p16 = kb_p16.reshape(batch, seq, KV_WIDTH)
    kb_s3 = kb_s.reshape(dec_batch, t_new, KV_WIDTH)
    vb_s3 = vb_s.reshape(dec_batch, t_new, KV_WIDTH)
    cache_kb = cache_k_b.reshape(dec_batch, len_b, KV_WIDTH)
    cache_vb = cache_v_b.reshape(dec_batch, len_b, KV_WIDTH)

    pad_b = LEFT_CHUNKS_B * CHUNK
    tk_b = pad_b + CHUNK
    assert int(pos_s[0] - kpos_b[0]) == pad_b and t_new <= CHUNK and len_b + t_new <= tk_b
    assert np.all(np.diff(kpos_b) == 1)
    rel = _rel_bias(rel_bias_b.astype(F32) * LOG2E, CHUNK, tk_b, pad_b)
    rel = rel.transpose(0, 2, 1, 3)
    mask_s = np.where(_np_band_mask(pos_s, kpos_b, LEFT_CHUNKS_B), 0.0, NEG).astype(np.float32)

    seg_q = [(0, dq, True)]
    for j in range(n_layers_b):
        layer = n_layers_a + j
        wq = w_q_b[j][:, perm].astype(BF16)
        gains = jnp.tile(g_q_b[j] * q_scale, N_HEADS)
        wo = w_o_b[j][perm, :].astype(BF16)
        wup = w_up[layer].astype(BF16)
        wdn = w_down[layer].astype(BF16)
        bias_p = _pair_bias(rel[j])
        bias_s = (rel[j][:, :t_new, :len_b + t_new] + mask_s[None]).reshape(N_HEADS * t_new, -1)

        (qtp,) = _project(hp, g_attn[layer], None, None, [], (), wt=wq.T, gains_t=gains, qt_rows=dq)
        ap = _prompt_attention(qtp.reshape(batch, seq // LANES, dq, LANES), kb_p16, vtb_p, bias_p, None,
                               LEFT_CHUNKS_B)
        (qs,) = _project(hs, g_attn[layer], wq, gains, seg_q, (BF16,))
        as_ = _sample_attention(qs.reshape(dec_batch, t_new, dq), cache_kb, kb_s3, cache_vb, vb_s3, bias_s, None)

        hp = _out_mlp(hp, ap.reshape(batch * seq, dq), wo, g_mlp[layer], wup, wdn)
        hs = _out_mlp(hs, as_.reshape(dec_batch * t_new, dq), wo, g_mlp[layer], wup, wdn)

    return (hp.reshape(batch, seq, d), hs.reshape(dec_batch, t_new, d),
            jnp.stack(ka_p), jnp.stack(va_p),
            kv4(kb_p, batch, keep_b), kv4(vb_p, batch, keep_b),
            jnp.stack(ka_s), jnp.stack(va_s),
            kv4(kb_s3, dec_batch, t_new), kv4(vb_s3, dec_batch, t_new))
```

```python
import functools
import math

import numpy as np
import jax
import jax.numpy as jnp
from jax import lax
from jax.experimental import pallas as pl
from jax.experimental.pallas import tpu as pltpu

CHUNK = 64
HEAD_DIM = 64
N_KV_HEADS = 4
GROUP = 4
N_HEADS = N_KV_HEADS * GROUP
LEFT_CHUNKS_A = 2
LEFT_CHUNKS_B = 8
MAX_REL = 128
N_REL = 2 * MAX_REL + 1
PAST_LEN = 1024
EPS = 1e-6
NEG = -1e30
LOG2E = math.log2(math.e)

LANES = 128
KV_WIDTH = N_KV_HEADS * HEAD_DIM
GROUP_WIDTH = GROUP * HEAD_DIM
PAIR = 2 * CHUNK
ONES_ROWS = 16
REL_BIAS_ROWS = 8
ROW_TILE = 1024
MLP_TILE = 1024
MLP_FF_CHUNKS = 8
ATTN_TILE = 2048
PAIR_UNROLL = 4
VMEM_LIMIT = 56 * 1024 * 1024

BF16 = jnp.bfloat16
F32 = jnp.float32


def _const_spec(shape):
    zeros = (0,) * len(shape)
    return pl.BlockSpec(shape, lambda *_: zeros, pipeline_mode=pl.Buffered(1))


def _head_sum_matrix():
    idx = np.arange(GROUP_WIDTH) // HEAD_DIM
    return jnp.asarray(idx[:, None] == idx[None, :], dtype=BF16)


def _group_major_perm():
    g, h, d = np.meshgrid(np.arange(GROUP), np.arange(N_KV_HEADS), np.arange(HEAD_DIM), indexing="ij")
    return ((h * GROUP + g) * HEAD_DIM + d).reshape(-1)


def _rms_rows(x):
    return x * lax.rsqrt(jnp.mean(x * x, axis=-1, keepdims=True) + EPS)


def _proj_kernel(*refs, segments, qt_rows, vt_rows):
    refs = list(refs)
    x_ref, g_ref, hsum_ref = refs[:3]
    del refs[:3]
    if segments:
        w_ref, gain_ref = refs[:2]
        del refs[:2]
    if qt_rows or vt_rows:
        wt_ref = refs.pop(0)
    if qt_rows:
        gain_t_ref = refs.pop(0)
    out_refs = refs
    xn = (_rms_rows(x_ref[...]) * g_ref[...]).astype(BF16)
    if segments:
        if w_ref.shape[1] == GROUP_WIDTH:
            half = xn.shape[0] // 2
            y = jnp.concatenate([jnp.dot(xn[:half], w_ref[...], preferred_element_type=F32),
                                 jnp.dot(xn[half:], w_ref[...], preferred_element_type=F32)], axis=0)
        else:
            y = jnp.dot(xn, w_ref[...], preferred_element_type=F32)
        for (start, width, normed), o_ref in zip(segments, out_refs):
            for t in range(width // GROUP_WIDTH):
                lo = start + t * GROUP_WIDTH
                yt = y[:, lo:lo + GROUP_WIDTH]
                if normed:
                    ss = jnp.dot((yt * yt).astype(BF16), hsum_ref[...], preferred_element_type=F32)
                    yt = yt * lax.rsqrt(ss * (1.0 / HEAD_DIM) + EPS) * gain_ref[:, lo:lo + GROUP_WIDTH]
                o_ref[:, t * GROUP_WIDTH:(t + 1) * GROUP_WIDTH] = yt.astype(o_ref.dtype)
    if qt_rows or vt_rows:
        yt = lax.dot_general(wt_ref[...], xn, (((1,), (1,)), ((), ())), preferred_element_type=F32)
        if qt_rows:
            qt_ref = out_refs[len(segments)]
            n_tok = yt.shape[1]
            for grp in range(qt_rows // GROUP_WIDTH):
                r0 = grp * GROUP_WIDTH
                blk = yt[r0:r0 + GROUP_WIDTH, :].reshape(GROUP, HEAD_DIM, n_tok)
                scale = lax.rsqrt(jnp.mean(blk * blk, axis=1, keepdims=True) + EPS)
                gain = gain_t_ref[r0:r0 + GROUP_WIDTH, :].reshape(GROUP, HEAD_DIM, LANES)
                for lb in range(n_tok // LANES):
                    piece = blk[:, :, lb * LANES:(lb + 1) * LANES] * scale[:, :, lb * LANES:(lb + 1) * LANES] * gain
                    qt_ref[lb, r0:r0 + GROUP_WIDTH, :] = piece.reshape(GROUP_WIDTH, LANES).astype(qt_ref.dtype)
        if vt_rows:
            out_refs[-1][...] = yt[qt_rows:qt_rows + vt_rows, :].astype(out_refs[-1].dtype)


def _project(x, g, w, gains, segments, out_dtypes, wt=None, gains_t=None, qt_rows=0, seq=None):
    n, d = x.shape
    tile = min(ROW_TILE, n)
    assert n % tile == 0
    out_shape = [jax.ShapeDtypeStruct((n, width), dt) for (_, width, _), dt in zip(segments, out_dtypes)]
    out_specs = [pl.BlockSpec((tile, width), lambda i: (i, 0)) for (_, width, _) in segments]
    in_specs = [
        pl.BlockSpec((tile, d), lambda i: (i, 0)),
        _const_spec((1, d)),
        _const_spec((GROUP_WIDTH, GROUP_WIDTH)),
    ]
    args = [x, g.reshape(1, d), _head_sum_matrix()]
    if segments:
        c = w.shape[1]
        in_specs += [_const_spec((d, c)), _const_spec((1, c))]
        args += [w, gains.reshape(1, c)]
    vt_rows = 0
    if wt is not None:
        vt_rows = wt.shape[0] - qt_rows
        in_specs.append(_const_spec(wt.shape))
        args.append(wt)
    if qt_rows:
        assert tile % LANES == 0
        in_specs.append(_const_spec((qt_rows, LANES)))
        args.append(jnp.broadcast_to(gains_t.astype(F32)[:, None], (qt_rows, LANES)))
        out_shape.append(jax.ShapeDtypeStruct((n // LANES, qt_rows, LANES), BF16))
        out_specs.append(pl.BlockSpec((tile // LANES, qt_rows, LANES), lambda i: (i, 0, 0)))
    if vt_rows:
        assert seq % tile == 0 and n % seq == 0
        per_seq = seq // tile
        out_shape.append(jax.ShapeDtypeStruct((n // seq, vt_rows, seq), BF16))
        out_specs.append(pl.BlockSpec((None, vt_rows, tile), lambda i: (i // per_seq, 0, i % per_seq)))
    return pl.pallas_call(
        functools.partial(_proj_kernel, segments=tuple(segments), qt_rows=qt_rows, vt_rows=vt_rows),
        grid=(n // tile,),
        in_specs=in_specs,
        out_specs=out_specs,
        out_shape=out_shape,
        compiler_params=pltpu.CompilerParams(
            dimension_semantics=("arbitrary",), vmem_limit_bytes=VMEM_LIMIT),
        name="project",
    )(*args)


def _band_attn_kernel(*refs, prev, n_pairs, has_sink):
    if has_sink:
        q_ref, kp_ref, kc_ref, vtp_ref, vtc_ref, bias_ref, sink_ref, o_ref, kcat, vt3, s_buf0, s_buf1, ot_prev = refs
    else:
        q_ref, kp_ref, kc_ref, vtp_ref, vtc_ref, bias_ref, o_ref, kcat, vt3, s_buf0, s_buf1, ot_prev = refs
        sink_ref = None
    s_bufs = (s_buf0, s_buf1)
    tile = n_pairs * PAIR
    band = prev + PAIR
    n_cols = 2 * GROUP * CHUNK

    kcat[0:prev, :] = kp_ref[...]
    kcat[prev:prev + tile, :] = kc_ref[...]
    ones = jnp.ones((ONES_ROWS, LANES), BF16)
    blocks = [vtp_ref[:, b * LANES:(b + 1) * LANES] for b in range(prev // LANES)]
    blocks += [vtc_ref[:, b * LANES:(b + 1) * LANES] for b in range(tile // LANES)]
    for b, blk in enumerate(blocks):
        for h in range(N_KV_HEADS):
            vt3[b, h, 0:HEAD_DIM, :] = blk[h * HEAD_DIM:(h + 1) * HEAD_DIM, :]
            vt3[b, h, HEAD_DIM:HEAD_DIM + ONES_ROWS, :] = ones

    def scores(j, h, masked, s_buf):
        r0 = pl.multiple_of(j * PAIR, PAIR)
        kb = kcat[pl.ds(r0, band), :]
        cols = []
        for g in range(GROUP):
            qt = q_ref[j, g * GROUP_WIDTH + h * HEAD_DIM:g * GROUP_WIDTH + (h + 1) * HEAD_DIM, :]
            parts = []
            if h > 0:
                parts.append(jnp.zeros((h * HEAD_DIM, PAIR), BF16))
            parts.append(qt)
            if h + 1 < N_KV_HEADS:
                parts.append(jnp.zeros(((N_KV_HEADS - 1 - h) * HEAD_DIM, PAIR), BF16))
            cols.append(jnp.concatenate(parts, axis=0))
        qs = jnp.concatenate(cols, axis=1)
        s = jnp.dot(kb, qs, preferred_element_type=F32)
        s = s + bias_ref[h]
        if masked:
            valid = lax.broadcasted_iota(jnp.int32, (band, n_cols), 0) >= prev - r0
            s = jnp.where(valid, s, NEG)
        s_buf[...] = s
        return jnp.max(s, axis=0, keepdims=True)

    def softmax_pv(m, s_buf, j, h):
        if has_sink:
            m = jnp.maximum(m, sink_ref[h])
        ot = jnp.zeros((HEAD_DIM + ONES_ROWS, n_cols), F32)
        n_blocks = band // LANES
        for b in range(0, n_blocks, 2):
            width = min(2, n_blocks - b) * LANES
            eb = jnp.exp2(s_buf[b * LANES:b * LANES + width, :] - m).astype(BF16)
            vt = jnp.concatenate([vt3[j + b + i, h] for i in range(width // LANES)], axis=1)
            ot = ot + jnp.dot(vt, eb, preferred_element_type=F32)
        denom = ot[HEAD_DIM:HEAD_DIM + 1, :]
        if has_sink:
            denom = denom + jnp.exp2(sink_ref[h] - m)
        return ot[0:HEAD_DIM, :] * (1.0 / denom)

    def run(masked):
        m_first = scores(0, 0, masked, s_bufs[0])
        ot_prev[...] = jnp.zeros(ot_prev.shape, F32)

        def write_out(j):
            r0 = pl.multiple_of(j * PAIR, PAIR)
            for g in range(GROUP):
                og = ot_prev[:, g * PAIR:(g + 1) * PAIR].T
                o_ref[pl.ds(r0, PAIR), g * KV_WIDTH:(g + 1) * KV_WIDTH] = og.astype(o_ref.dtype)

        def body(j, m):
            write_out(jnp.maximum(j - 1, 0))
            for h in range(N_KV_HEADS):
                if h + 1 < N_KV_HEADS:
                    m_ahead = scores(j, h + 1, masked, s_bufs[(h + 1) % 2])
                else:
                    m_ahead = scores(jnp.minimum(j + 1, n_pairs - 1), 0, masked, s_bufs[0])
                ot_prev[h * HEAD_DIM:(h + 1) * HEAD_DIM, :] = softmax_pv(m, s_bufs[h % 2], j, h)
                m = m_ahead
            return m

        lax.fori_loop(0, n_pairs, body, m_first, unroll=PAIR_UNROLL)
        write_out(n_pairs - 1)

    @pl.when(pl.program_id(1) == 0)
    def _():
        run(True)

    @pl.when(pl.program_id(1) > 0)
    def _():
        run(False)


def _prompt_attention(q, k, vt, bias, sink, left_chunks):
    b, n_blocks, dq, _ = q.shape
    s = n_blocks * LANES
    prev = left_chunks * CHUNK
    tile = min(ATTN_TILE, s)
    band = prev + PAIR
    assert s % tile == 0 and tile % prev == 0 and prev % LANES == 0 and tile % PAIR == 0
    assert bias.shape == (N_KV_HEADS, band, 2 * GROUP * CHUNK)
    ratio = tile // prev
    has_sink = sink is not None
    cur_rows = lambda bi, i: (bi, i, 0)
    prev_rows = lambda bi, i: (bi, jnp.maximum(i * ratio - 1, 0), 0)
    cur_cols = lambda bi, i: (bi, 0, i)
    prev_cols = lambda bi, i: (bi, 0, jnp.maximum(i * ratio - 1, 0))
    in_specs = [
        pl.BlockSpec((None, tile // LANES, dq, LANES), lambda bi, i: (bi, i, 0, 0)),
        pl.BlockSpec((None, prev, KV_WIDTH), prev_rows),
        pl.BlockSpec((None, tile, KV_WIDTH), cur_rows),
        pl.BlockSpec((None, KV_WIDTH, prev), prev_cols),
        pl.BlockSpec((None, KV_WIDTH, tile), cur_cols),
        _const_spec(bias.shape),
    ]
    args = [q, k, k, vt, vt, bias]
    if has_sink:
        in_specs.append(_const_spec(sink.shape))
        args.append(sink)
    return pl.pallas_call(
        functools.partial(_band_attn_kernel, prev=prev, n_pairs=tile // PAIR, has_sink=has_sink),
        grid=(b, s // tile),
        in_specs=in_specs,
        out_specs=pl.BlockSpec((None, tile, dq), cur_rows),
        out_shape=jax.ShapeDtypeStruct((b, s, dq), BF16),
        scratch_shapes=[
            pltpu.VMEM((prev + tile, KV_WIDTH), BF16),
            pltpu.VMEM(((prev + tile) // LANES, N_KV_HEADS, HEAD_DIM + ONES_ROWS, LANES), BF16),
            pltpu.VMEM((band, 2 * GROUP * CHUNK), F32),
            pltpu.VMEM((band, 2 * GROUP * CHUNK), F32),
            pltpu.VMEM((N_KV_HEADS * HEAD_DIM, 2 * GROUP * CHUNK), F32),
        ],
        compiler_params=pltpu.CompilerParams(
            dimension_semantics=("arbitrary", "arbitrary"), vmem_limit_bytes=VMEM_LIMIT),
        name="band_attention",
    )(*args)


def _pair_bias(per_head):
    h, tq, tk = per_head.shape
    neg = jnp.full((h, tq, CHUNK), NEG, F32)
    both = jnp.stack([jnp.concatenate([per_head, neg], axis=-1),
                      jnp.concatenate([neg, per_head], axis=-1)], axis=1)
    both = both.reshape(N_KV_HEADS, GROUP, 2, tq, tk + CHUNK)
    return both.transpose(0, 4, 1, 2, 3).reshape(N_KV_HEADS, tk + CHUNK, 2 * GROUP * tq)


def _pair_sink(sink):
    s = (sink.astype(F32) * LOG2E).reshape(N_KV_HEADS, GROUP, 1, 1)
    return jnp.broadcast_to(s, (N_KV_HEADS, GROUP, 2, CHUNK)).reshape(N_KV_HEADS, 1, 2 * GROUP * CHUNK)


def _sample_attn_kernel(*refs, tq, has_sink):
    if has_sink:
        q_ref, kp_ref, kc_ref, vp_ref, vc_ref, bias_ref, sink_ref, o_ref = refs
    else:
        q_ref, kp_ref, kc_ref, vp_ref, vc_ref, bias_ref, o_ref = refs
        sink_ref = None
    kcat = jnp.concatenate([kp_ref[...], kc_ref[...]], axis=0).astype(BF16)
    vcat = jnp.concatenate([vp_ref[...], vc_ref[...]], axis=0).astype(BF16)
    lane_head = lax.broadcasted_iota(jnp.int32, (tq, KV_WIDTH), 1) // HEAD_DIM
    qc = q_ref[...]
    qs = jnp.concatenate(
        [jnp.where(lane_head == h, qc[:, g * KV_WIDTH:(g + 1) * KV_WIDTH], 0)
         for h in range(N_KV_HEADS) for g in range(GROUP)], axis=0).astype(BF16)
    s = lax.dot_general(qs, kcat, (((1,), (1,)), ((), ())), preferred_element_type=F32)
    s = s + bias_ref[...]
    m = jnp.max(s, axis=-1, keepdims=True)
    if has_sink:
        m = jnp.maximum(m, sink_ref[...])
    e = jnp.exp2(s - m)
    denom = jnp.sum(e, axis=-1, keepdims=True)
    if has_sink:
        denom = denom + jnp.exp2(sink_ref[...] - m)
    p = (e * (1.0 / denom)).astype(BF16)
    o_all = jnp.dot(p, vcat, preferred_element_type=F32)
    for g in range(GROUP):
        o = jnp.zeros((tq, KV_WIDTH), F32)
        for h in range(N_KV_HEADS):
            r0 = (h * GROUP + g) * tq
            o = jnp.where(lane_head == h, o_all[r0:r0 + tq, :], o)
        o_ref[:, g * KV_WIDTH:(g + 1) * KV_WIDTH] = o.astype(o_ref.dtype)


def _sample_attention(q, k_cache, k_new, v_cache, v_new, bias, sink):
    b, t, dq = q.shape
    prev = k_cache.shape[1]
    assert bias.shape == (N_HEADS * t, prev + t)
    has_sink = sink is not None
    row_map = lambda bi: (bi, 0, 0)
    in_specs = [
        pl.BlockSpec((None, t, dq), row_map),
        pl.BlockSpec((None, prev, KV_WIDTH), row_map),
        pl.BlockSpec((None, t, KV_WIDTH), row_map),
        pl.BlockSpec((None, prev, KV_WIDTH), row_map),
        pl.BlockSpec((None, t, KV_WIDTH), row_map),
        _const_spec(bias.shape),
    ]
    args = [q, k_cache, k_new, v_cache, v_new, bias]
    if has_sink:
        in_specs.append(_const_spec(sink.shape))
        args.append(sink)
    return pl.pallas_call(
        functools.partial(_sample_attn_kernel, tq=t, has_sink=has_sink),
        grid=(b,),
        in_specs=in_specs,
        out_specs=pl.BlockSpec((None, t, dq), row_map),
        out_shape=jax.ShapeDtypeStruct((b, t, dq), BF16),
        compiler_params=pltpu.CompilerParams(
            dimension_semantics=("arbitrary",), vmem_limit_bytes=VMEM_LIMIT),
        name="sample_attention",
    )(*args)


def _out_mlp_kernel(x_ref, a_ref, wo_ref, g_ref, wup_ref, wdn_ref, o_ref, *, ff_chunks):
    h = x_ref[...] + jnp.dot(a_ref[...], wo_ref[...], preferred_element_type=F32)
    hn = (_rms_rows(h) * g_ref[...]).astype(BF16)
    d_ff = wup_ref.shape[1]
    step = d_ff // ff_chunks
    acc = h
    for c in range(ff_chunks):
        u = jnp.dot(hn, wup_ref[:, c * step:(c + 1) * step], preferred_element_type=F32)
        u = jnp.square(jnp.maximum(u, 0.0)).astype(BF16)
        acc = acc + jnp.dot(u, wdn_ref[c * step:(c + 1) * step, :], preferred_element_type=F32)
    o_ref[...] = acc


def _out_mlp(x, a, wo, g, wup, wdn):
    n, d = x.shape
    d_ff = wup.shape[1]
    tile = min(MLP_TILE, n)
    assert n % tile == 0
    row_spec = pl.BlockSpec((tile, d), lambda i: (i, 0))
    return pl.pallas_call(
        functools.partial(_out_mlp_kernel, ff_chunks=MLP_FF_CHUNKS),
        grid=(n // tile,),
        in_specs=[
            row_spec,
            row_spec,
            _const_spec((d, d)),
            _const_spec((1, d)),
            _const_spec((d, d_ff)),
            _const_spec((d_ff, d)),
        ],
        out_specs=row_spec,
        out_shape=jax.ShapeDtypeStruct((n, d), F32),
        compiler_params=pltpu.CompilerParams(
            dimension_semantics=("arbitrary",), vmem_limit_bytes=VMEM_LIMIT),
        name="out_proj_mlp",
    )(x, a, wo, g.reshape(1, d), wup, wdn)


def _rel_bias_kernel(tab_ref, o_ref, *, tk, delta):
    n_pad = tab_ref.shape[-1]
    r = lax.broadcasted_iota(jnp.int32, (n_pad, tk), 0)
    j = lax.broadcasted_iota(jnp.int32, (n_pad, tk), 1)
    for qi in range(o_ref.shape[0]):
        q = pl.program_id(1) * o_ref.shape[0] + qi
        idx = jnp.clip(q + delta - j, -MAX_REL, MAX_REL) + MAX_REL
        onehot = jnp.where(r == idx, 1.0, 0.0).astype(BF16)
        acc = jnp.dot(tab_ref[0], onehot, preferred_element_type=F32)
        acc = acc + jnp.dot(tab_ref[1], onehot, preferred_element_type=F32)
        acc = acc + jnp.dot(tab_ref[2], onehot, preferred_element_type=F32)
        o_ref[qi] = acc


def _rel_bias(table, tq, tk, delta):
    n_layers, h, n_rel = table.shape
    n_pad = -(-n_rel // LANES) * LANES
    t = jnp.pad(table, ((0, 0), (0, 0), (0, n_pad - n_rel)))
    hi = t.astype(BF16)
    mid = (t - hi.astype(F32)).astype(BF16)
    lo = (t - hi.astype(F32) - mid.astype(F32)).astype(BF16)
    parts = jnp.stack([hi, mid, lo], axis=1)
    return pl.pallas_call(
        functools.partial(_rel_bias_kernel, tk=tk, delta=delta),
        grid=(n_layers, tq // REL_BIAS_ROWS),
        in_specs=[pl.BlockSpec((None, 3, h, n_pad), lambda l, q: (l, 0, 0, 0))],
        out_specs=pl.BlockSpec((None, REL_BIAS_ROWS, h, tk), lambda l, q: (l, q, 0, 0)),
        out_shape=jax.ShapeDtypeStruct((n_layers, tq, h, tk), F32),
        compiler_params=pltpu.CompilerParams(dimension_semantics=("arbitrary", "arbitrary")),
        name="rel_bias_table",
    )(parts)


def _np_band_mask(q_pos, k_pos, left_chunks):
    qc = q_pos[:, None] // CHUNK
    kc = k_pos[None, :] // CHUNK
    return (k_pos[None, :] >= 0) & (kc <= qc) & (kc >= qc - left_chunks)


def _alibi_bias(q_pos, k_pos, static_mask):
    slopes = (2.0 ** (-8.0 * np.arange(1, N_HEADS + 1, dtype=np.float32) / N_HEADS)).astype(np.float32)
    dist = np.abs(q_pos[:, None] - k_pos[None, :]).astype(np.float32)
    bias = -slopes[:, None, None] * dist[None] * np.float32(LOG2E)
    if static_mask:
        bias = np.where(_np_band_mask(q_pos, k_pos, LEFT_CHUNKS_A)[None], bias, np.float32(NEG))
    return jnp.asarray(bias.astype(np.float32))


def _row_sink(sink, tq):
    return jnp.repeat(sink.astype(F32) * LOG2E, tq).reshape(N_HEADS * tq, 1)


def kernel(x_prompt, x_sample, cache_k_a, cache_v_a, cache_k_b, cache_v_b, g_attn, g_mlp, w_qkv_a,
           g_q_a, g_k_a, sink_a, w_o_a, g_kv, w_kv, g_k_b, w_q_b, g_q_b, rel_bias_b, w_o_b, w_up, w_down):
    batch, seq, d = x_prompt.shape
    dec_batch, t_new, _ = x_sample.shape
    n_layers_a = w_qkv_a.shape[0]
    n_layers_b = w_q_b.shape[0]
    len_a = cache_k_a.shape[2]
    len_b = cache_k_b.shape[1]
    keep_a = min(LEFT_CHUNKS_A * CHUNK, seq)
    keep_b = min(LEFT_CHUNKS_B * CHUNK, seq)
    dq = N_HEADS * HEAD_DIM
    q_scale = HEAD_DIM ** -0.5 * LOG2E
    perm = _group_major_perm()
    qkv_cols = np.concatenate([perm, np.arange(dq, dq + 2 * KV_WIDTH)])
    qv_cols = np.concatenate([perm, np.arange(dq + KV_WIDTH, dq + 2 * KV_WIDTH)])

    hp = x_prompt.reshape(batch * seq, d)
    hs = x_sample.reshape(dec_batch * t_new, d)

    pos_s = PAST_LEN + np.arange(t_new)
    kpos_a = np.concatenate([PAST_LEN - len_a + np.arange(len_a), pos_s])
    kpos_b = np.concatenate([PAST_LEN - len_b + np.arange(len_b), pos_s])
    chunk_q = np.arange(CHUNK)
    bias_a_prompt = _pair_bias(_alibi_bias(chunk_q, np.arange(-LEFT_CHUNKS_A * CHUNK, CHUNK), static_mask=False))
    bias_a_sample = _alibi_bias(pos_s, kpos_a, static_mask=True).reshape(N_HEADS * t_new, -1)

    ones_kv = jnp.ones((KV_WIDTH,), F32)
    seg_a_p = [(0, dq, True), (dq, KV_WIDTH, True)]
    seg_a_s = seg_a_p + [(dq + KV_WIDTH, KV_WIDTH, False)]
    seg_kv = [(0, KV_WIDTH, True), (KV_WIDTH, KV_WIDTH, False)]
    last_rows = lambda h, keep: h.reshape(batch, seq, d)[:, seq - keep:].reshape(batch * keep, d)
    kv4 = lambda a, n, t: a.reshape(n, t, N_KV_HEADS, HEAD_DIM)

    ka_p, va_p, ka_s, va_s = [], [], [], []
    for i in range(n_layers_a):
        w = w_qkv_a[i][:, qkv_cols].astype(BF16)
        wqv_t = w_qkv_a[i][:, qv_cols].T.astype(BF16)
        gains = jnp.concatenate([jnp.tile(g_q_a[i] * q_scale, N_HEADS), jnp.tile(g_k_a[i], N_KV_HEADS), ones_kv])
        wo = w_o_a[i][perm, :].astype(BF16)
        wup = w_up[i].astype(BF16)
        wdn = w_down[i].astype(BF16)

        kp16, qtp, vtp = _project(hp, g_attn[i], w[:, dq:dq + KV_WIDTH], gains[dq:dq + KV_WIDTH], seg_kv[:1],
                                  (BF16,), wt=wqv_t, gains_t=gains[:dq], qt_rows=dq, seq=seq)
        ap = _prompt_attention(qtp.reshape(batch, seq // LANES, dq, LANES), kp16.reshape(batch, seq, KV_WIDTH),
                               vtp, bias_a_prompt, _pair_sink(sink_a[i]), LEFT_CHUNKS_A)
        kp, vp = _project(last_rows(hp, keep_a), g_attn[i], w[:, dq:], gains[dq:], seg_kv, (F32, F32))
        ka_p.append(kv4(kp, batch, keep_a))
        va_p.append(kv4(vp, batch, keep_a))

        qs, ks, vs = _project(hs, g_attn[i], w, gains, seg_a_s, (BF16, F32, F32))
        ks3 = ks.reshape(dec_batch, t_new, KV_WIDTH)
        vs3 = vs.reshape(dec_batch, t_new, KV_WIDTH)
        as_ = _sample_attention(qs.reshape(dec_batch, t_new, dq),
                                cache_k_a[i].reshape(dec_batch, len_a, KV_WIDTH), ks3,
                                cache_v_a[i].reshape(dec_batch, len_a, KV_WIDTH), vs3,
                                bias_a_sample, _row_sink(sink_a[i], t_new))
        ka_s.append(ks3.reshape(dec_batch, t_new, N_KV_HEADS, HEAD_DIM))
        va_s.append(vs3.reshape(dec_batch, t_new, N_KV_HEADS, HEAD_DIM))

        hp = _out_mlp(hp, ap.reshape(batch * seq, dq), wo, g_mlp[i], wup, wdn)
        hs = _out_mlp(hs, as_.reshape(dec_batch * t_new, dq), wo, g_mlp[i], wup, wdn)

    wkv = w_kv.astype(BF16)
    gains_kv = jnp.concatenate([jnp.tile(g_k_b, N_KV_HEADS), ones_kv])
    kb_p16, vtb_p = _project(hp, g_kv, wkv[:, :KV_WIDTH], gains_kv[:KV_WIDTH], seg_kv[:1], (BF16,),
                             wt=wkv[:, KV_WIDTH:].T, seq=seq)
    kb_p, vb_p = _project(last_rows(hp, keep_b), g_kv, wkv, gains_kv, seg_kv, (F32, F32))
    kb_s, vb_s = _project(hs, g_kv, wkv, gains_kv, seg_kv, (F32, F32))
    kb_p16 = kb_p16.reshape(batch, seq, KV_WIDTH)
    kb_s3 = kb_s.reshape(dec_batch, t_new, KV_WIDTH)
    vb_s3 = vb_s.reshape(dec_batch, t_new, KV_WIDTH)
    cache_kb = cache_k_b.reshape(dec_batch, len_b, KV_WIDTH)
    cache_vb = cache_v_b.reshape(dec_batch, len_b, KV_WIDTH)

    pad_b = LEFT_CHUNKS_B * CHUNK
    tk_b = pad_b + CHUNK
    assert int(pos_s[0] - kpos_b[0]) == pad_b and t_new <= CHUNK and len_b + t_new <= tk_b
    assert np.all(np.diff(kpos_b) == 1)
    rel = _rel_bias(rel_bias_b.astype(F32) * LOG2E, CHUNK, tk_b, pad_b)
    rel = rel.transpose(0, 2, 1, 3)
    mask_s = np.where(_np_band_mask(pos_s, kpos_b, LEFT_CHUNKS_B), 0.0, NEG).astype(np.float32)

    seg_q = [(0, dq, True)]
    for j in range(n_layers_b):
        layer = n_layers_a + j
        wq = w_q_b[j][:, perm].astype(BF16)
        gains = jnp.tile(g_q_b[j] * q_scale, N_HEADS)
        wo = w_o_b[j][perm, :].astype(BF16)
        wup = w_up[layer].astype(BF16)
        wdn = w_down[layer].astype(BF16)
        bias_p = _pair_bias(rel[j])
        bias_s = (rel[j][:, :t_new, :len_b + t_new] + mask_s[None]).reshape(N_HEADS * t_new, -1)

        (qtp,) = _project(hp, g_attn[layer], None, None, [], (), wt=wq.T, gains_t=gains, qt_rows=dq)
        ap = _prompt_attention(qtp.reshape(batch, seq // LANES, dq, LANES), kb_p16, vtb_p, bias_p, None,
                               LEFT_CHUNKS_B)
        (qs,) = _project(hs, g_attn[layer], wq, gains, seg_q, (BF16,))
        as_ = _sample_attention(qs.reshape(dec_batch, t_new, dq), cache_kb, kb_s3, cache_vb, vb_s3, bias_s, None)

        hp = _out_mlp(hp, ap.reshape(batch * seq, dq), wo, g_mlp[layer], wup, wdn)
        hs = _out_mlp(hs, as_.reshape(dec_batch * t_new, dq), wo, g_mlp[layer], wup, wdn)

    return (hp.reshape(batch, seq, d), hs.reshape(dec_batch, t_new, d),
            jnp.stack(ka_p), jnp.stack(va_p),
            kv4(kb_p, batch, keep_b), kv4(vb_p, batch, keep_b),
            jnp.stack(ka_s), jnp.stack(va_s),
            kv4(kb_s3, dec_batch, t_new), kv4(vb_s3, dec_batch, t_new))
```

```python
import functools
import math

import numpy as np
import jax
import jax.numpy as jnp
from jax import lax
from jax.experimental import pallas as pl
from jax.experimental.pallas import tpu as pltpu

CHUNK = 64
HEAD_DIM = 64
N_KV_HEADS = 4
GROUP = 4
N_HEADS = N_KV_HEADS * GROUP
LEFT_CHUNKS_A = 2
LEFT_CHUNKS_B = 8
MAX_REL = 128
N_REL = 2 * MAX_REL + 1
PAST_LEN = 1024
EPS = 1e-6
NEG = -1e30
LOG2E = math.log2(math.e)

LANES = 128
KV_WIDTH = N_KV_HEADS * HEAD_DIM
GROUP_WIDTH = GROUP * HEAD_DIM
PAIR = 2 * CHUNK
ONES_ROWS = 16
REL_BIAS_ROWS = 8
ROW_TILE = 1024
MLP_TILE = 1024
MLP_FF_CHUNKS = 8
ATTN_TILE = 2048
SHORT_BAND_LOOKAHEAD = 2
LONG_BAND_LOOKAHEAD = 1
PAIR_UNROLL = 4
VMEM_LIMIT = 56 * 1024 * 1024

BF16 = jnp.bfloat16
F32 = jnp.float32


def _const_spec(shape):
    zeros = (0,) * len(shape)
    return pl.BlockSpec(shape, lambda *_: zeros, pipeline_mode=pl.Buffered(1))


def _head_sum_matrix():
    idx = np.arange(GROUP_WIDTH) // HEAD_DIM
    return jnp.asarray(idx[:, None] == idx[None, :], dtype=BF16)


def _group_major_cols(w):
    rows = w.shape[0]
    return w.reshape(rows, N_KV_HEADS, GROUP, HEAD_DIM).transpose(0, 2, 1, 3).reshape(rows, N_HEADS * HEAD_DIM)


def _group_major_rows(w):
    cols = w.shape[1]
    return w.reshape(N_KV_HEADS, GROUP, HEAD_DIM, cols).transpose(1, 0, 2, 3).reshape(N_HEADS * HEAD_DIM, cols)


def _rms_rows(x):
    return x * lax.rsqrt(jnp.mean(x * x, axis=-1, keepdims=True) + EPS)


def _proj_kernel(*refs, segments, qt_rows, vt_rows):
    refs = list(refs)
    x_ref, g_ref, hsum_ref = refs[:3]
    del refs[:3]
    if segments:
        w_ref, gain_ref = refs[:2]
        del refs[:2]
    if qt_rows or vt_rows:
        wt_ref = refs.pop(0)
    if qt_rows:
        gain_t_ref = refs.pop(0)
    out_refs = refs
    xn = (_rms_rows(x_ref[...]) * g_ref[...]).astype(BF16)
    if segments:
        if w_ref.shape[1] == GROUP_WIDTH:
            half = xn.shape[0] // 2
            y = jnp.concatenate([jnp.dot(xn[:half], w_ref[...], preferred_element_type=F32),
                                 jnp.dot(xn[half:], w_ref[...], preferred_element_type=F32)], axis=0)
        else:
            y = jnp.dot(xn, w_ref[...], preferred_element_type=F32)
        for (start, width, normed), o_ref in zip(segments, out_refs):
            for t in range(width // GROUP_WIDTH):
                lo = start + t * GROUP_WIDTH
                yt = y[:, lo:lo + GROUP_WIDTH]
                if normed:
                    ss = jnp.dot((yt * yt).astype(BF16), hsum_ref[...], preferred_element_type=F32)
                    yt = yt * lax.rsqrt(ss * (1.0 / HEAD_DIM) + EPS) * gain_ref[:, lo:lo + GROUP_WIDTH]
                o_ref[:, t * GROUP_WIDTH:(t + 1) * GROUP_WIDTH] = yt.astype(o_ref.dtype)
    if qt_rows or vt_rows:
        yt = lax.dot_general(wt_ref[...], xn, (((1,), (1,)), ((), ())), preferred_element_type=F32)
        if qt_rows:
            qt_ref = out_refs[len(segments)]
            n_tok = yt.shape[1]
            for grp in range(qt_rows // GROUP_WIDTH):
                r0 = grp * GROUP_WIDTH
                blk = yt[r0:r0 + GROUP_WIDTH, :].reshape(GROUP, HEAD_DIM, n_tok)
                scale = lax.rsqrt(jnp.mean(blk * blk, axis=1, keepdims=True) + EPS)
                gain = gain_t_ref[r0:r0 + GROUP_WIDTH, :].reshape(GROUP, HEAD_DIM, LANES)
                for lb in range(n_tok // LANES):
                    piece = blk[:, :, lb * LANES:(lb + 1) * LANES] * scale[:, :, lb * LANES:(lb + 1) * LANES] * gain
                    qt_ref[lb, r0:r0 + GROUP_WIDTH, :] = piece.reshape(GROUP_WIDTH, LANES).astype(qt_ref.dtype)
        if vt_rows:
            out_refs[-1][...] = yt[qt_rows:qt_rows + vt_rows, :].astype(out_refs[-1].dtype)


def _project(x, g, w, gains, segments, out_dtypes, wt=None, gains_t=None, qt_rows=0, seq=None):
    n, d = x.shape
    tile = min(ROW_TILE, n)
    assert n % tile == 0
    out_shape = [jax.ShapeDtypeStruct((n, width), dt) for (_, width, _), dt in zip(segments, out_dtypes)]
    out_specs = [pl.BlockSpec((tile, width), lambda i: (i, 0)) for (_, width, _) in segments]
    in_specs = [
        pl.BlockSpec((tile, d), lambda i: (i, 0)),
        _const_spec((1, d)),
        _const_spec((GROUP_WIDTH, GROUP_WIDTH)),
    ]
    args = [x, g.reshape(1, d), _head_sum_matrix()]
    if segments:
        c = w.shape[1]
        in_specs += [_const_spec((d, c)), _const_spec((1, c))]
        args += [w, gains.reshape(1, c)]
    vt_rows = 0
    if wt is not None:
        vt_rows = wt.shape[0] - qt_rows
        in_specs.append(_const_spec(wt.shape))
        args.append(wt)
    if qt_rows:
        assert tile % LANES == 0
        in_specs.append(_const_spec((qt_rows, LANES)))
        args.append(jnp.broadcast_to(gains_t.astype(F32)[:, None], (qt_rows, LANES)))
        out_shape.append(jax.ShapeDtypeStruct((n // LANES, qt_rows, LANES), BF16))
        out_specs.append(pl.BlockSpec((tile // LANES, qt_rows, LANES), lambda i: (i, 0, 0)))
    if vt_rows:
        assert seq % tile == 0 and n % seq == 0
        per_seq = seq // tile
        out_shape.append(jax.ShapeDtypeStruct((n // seq, vt_rows, seq), BF16))
        out_specs.append(pl.BlockSpec((None, vt_rows, tile), lambda i: (i // per_seq, 0, i % per_seq)))
    return pl.pallas_call(
        functools.partial(_proj_kernel, segments=tuple(segments), qt_rows=qt_rows, vt_rows=vt_rows),
        grid=(n // tile,),
        in_specs=in_specs,
        out_specs=out_specs,
        out_shape=out_shape,
        compiler_params=pltpu.CompilerParams(
            dimension_semantics=("arbitrary",), vmem_limit_bytes=VMEM_LIMIT),
        name="project",
    )(*args)


def _band_attn_kernel(*refs, prev, n_pairs, has_sink, lookahead):
    if has_sink:
        q_ref, kp_ref, kc_ref, vtp_ref, vtc_ref, bias_ref, sink_ref, o_ref, kcat, vt3, *s_bufs, ot_prev = refs
    else:
        q_ref, kp_ref, kc_ref, vtp_ref, vtc_ref, bias_ref, o_ref, kcat, vt3, *s_bufs, ot_prev = refs
        sink_ref = None
    assert len(s_bufs) == N_KV_HEADS
    tile = n_pairs * PAIR
    band = prev + PAIR
    n_cols = 2 * GROUP * CHUNK

    kcat[0:prev, :] = kp_ref[...]
    kcat[prev:prev + tile, :] = kc_ref[...]
    ones = jnp.ones((ONES_ROWS, LANES), BF16)
    blocks = [vtp_ref[:, b * LANES:(b + 1) * LANES] for b in range(prev // LANES)]
    blocks += [vtc_ref[:, b * LANES:(b + 1) * LANES] for b in range(tile // LANES)]
    for b, blk in enumerate(blocks):
        for h in range(N_KV_HEADS):
            vt3[b, h, 0:HEAD_DIM, :] = blk[h * HEAD_DIM:(h + 1) * HEAD_DIM, :]
            vt3[b, h, HEAD_DIM:HEAD_DIM + ONES_ROWS, :] = ones

    def scores(j, h, masked, s_buf):
        r0 = pl.multiple_of(j * PAIR, PAIR)
        kb = kcat[pl.ds(r0, band), :]
        cols = []
        for g in range(GROUP):
            qt = q_ref[j, g * GROUP_WIDTH + h * HEAD_DIM:g * GROUP_WIDTH + (h + 1) * HEAD_DIM, :]
            parts = []
            if h > 0:
                parts.append(jnp.zeros((h * HEAD_DIM, PAIR), BF16))
            parts.append(qt)
            if h + 1 < N_KV_HEADS:
                parts.append(jnp.zeros(((N_KV_HEADS - 1 - h) * HEAD_DIM, PAIR), BF16))
            cols.append(jnp.concatenate(parts, axis=0))
        qs = jnp.concatenate(cols, axis=1)
        s = jnp.dot(kb, qs, preferred_element_type=F32)
        s = s + bias_ref[h]
        if masked:
            valid = lax.broadcasted_iota(jnp.int32, (band, n_cols), 0) >= prev - r0
            s = jnp.where(valid, s, NEG)
        s_buf[...] = s
        return jnp.max(s, axis=0, keepdims=True)

    def softmax_pv(m, s_buf, j, h):
        if has_sink:
            m = jnp.maximum(m, sink_ref[h])
        ot = jnp.zeros((HEAD_DIM + ONES_ROWS, n_cols), F32)
        n_blocks = band // LANES
        for b in range(0, n_blocks, 2):
            width = min(2, n_blocks - b) * LANES
            eb = jnp.exp2(s_buf[b * LANES:b * LANES + width, :] - m).astype(BF16)
            vt = jnp.concatenate([vt3[j + b + i, h] for i in range(width // LANES)], axis=1)
            ot = ot + jnp.dot(vt, eb, preferred_element_type=F32)
        denom = ot[HEAD_DIM:HEAD_DIM + 1, :]
        if has_sink:
            denom = denom + jnp.exp2(sink_ref[h] - m)
        return ot[0:HEAD_DIM, :] * (1.0 / denom)

    def run(masked):
        m_first = tuple(scores(0, h, masked, s_bufs[h]) for h in range(lookahead))
        ot_prev[...] = jnp.zeros(ot_prev.shape, F32)

        def write_out(j):
            r0 = pl.multiple_of(j * PAIR, PAIR)
            for g in range(GROUP):
                og = ot_prev[:, g * PAIR:(g + 1) * PAIR].T
                o_ref[pl.ds(r0, PAIR), g * KV_WIDTH:(g + 1) * KV_WIDTH] = og.astype(o_ref.dtype)

        def body(j, ms):
            write_out(jnp.maximum(j - 1, 0))
            ms = list(ms)
            for h in range(N_KV_HEADS):
                h_ahead = h + lookahead
                if h_ahead < N_KV_HEADS:
                    ms.append(scores(j, h_ahead, masked, s_bufs[h_ahead]))
                else:
                    h_ahead -= N_KV_HEADS
                    ms.append(scores(jnp.minimum(j + 1, n_pairs - 1), h_ahead, masked, s_bufs[h_ahead]))
                ot_prev[h * HEAD_DIM:(h + 1) * HEAD_DIM, :] = softmax_pv(ms.pop(0), s_bufs[h], j, h)
            return tuple(ms)

        lax.fori_loop(0, n_pairs, body, m_first, unroll=PAIR_UNROLL)
        write_out(n_pairs - 1)

    @pl.when(pl.program_id(1) == 0)
    def _():
        run(True)

    @pl.when(pl.program_id(1) > 0)
    def _():
        run(False)


def _prompt_attention(q, k, vt, bias, sink, left_chunks):
    b, n_blocks, dq, _ = q.shape
    s = n_blocks * LANES
    prev = left_chunks * CHUNK
    tile = min(ATTN_TILE, s)
    band = prev + PAIR
    assert s % tile == 0 and tile % prev == 0 and prev % LANES == 0 and tile % PAIR == 0
    assert bias.shape == (N_KV_HEADS, band, 2 * GROUP * CHUNK)
    ratio = tile // prev
    has_sink = sink is not None
    cur_rows = lambda bi, i: (bi, i, 0)
    prev_rows = lambda bi, i: (bi, jnp.maximum(i * ratio - 1, 0), 0)
    cur_cols = lambda bi, i: (bi, 0, i)
    prev_cols = lambda bi, i: (bi, 0, jnp.maximum(i * ratio - 1, 0))
    in_specs = [
        pl.BlockSpec((None, tile // LANES, dq, LANES), lambda bi, i: (bi, i, 0, 0)),
        pl.BlockSpec((None, prev, KV_WIDTH), prev_rows),
        pl.BlockSpec((None, tile, KV_WIDTH), cur_rows),
        pl.BlockSpec((None, KV_WIDTH, prev), prev_cols),
        pl.BlockSpec((None, KV_WIDTH, tile), cur_cols),
        _const_spec(bias.shape),
    ]
    args = [q, k, k, vt, vt, bias]
    if has_sink:
        in_specs.append(_const_spec(sink.shape))
        args.append(sink)
    return pl.pallas_call(
        functools.partial(_band_attn_kernel, prev=prev, n_pairs=tile // PAIR, has_sink=has_sink,
                          lookahead=SHORT_BAND_LOOKAHEAD if band <= 2 * LANES else LONG_BAND_LOOKAHEAD),
        grid=(b, s // tile),
        in_specs=in_specs,
        out_specs=pl.BlockSpec((None, tile, dq), cur_rows),
        out_shape=jax.ShapeDtypeStruct((b, s, dq), BF16),
        scratch_shapes=[
            pltpu.VMEM((prev + tile, KV_WIDTH), BF16),
            pltpu.VMEM(((prev + tile) // LANES, N_KV_HEADS, HEAD_DIM + ONES_ROWS, LANES), BF16),
            *[pltpu.VMEM((band, 2 * GROUP * CHUNK), F32) for _ in range(N_KV_HEADS)],
            pltpu.VMEM((N_KV_HEADS * HEAD_DIM, 2 * GROUP * CHUNK), F32),
        ],
        compiler_params=pltpu.CompilerParams(
            dimension_semantics=("arbitrary", "arbitrary"), vmem_limit_bytes=VMEM_LIMIT),
        name="band_attention",
    )(*args)


def _pair_bias(per_head):
    h, tq, tk = per_head.shape
    neg = jnp.full((h, tq, CHUNK), NEG, F32)
    both = jnp.stack([jnp.concatenate([per_head, neg], axis=-1),
                      jnp.concatenate([neg, per_head], axis=-1)], axis=1)
    both = both.reshape(N_KV_HEADS, GROUP, 2, tq, tk + CHUNK)
    return both.transpose(0, 4, 1, 2, 3).reshape(N_KV_HEADS, tk + CHUNK, 2 * GROUP * tq)


def _pair_sink(sink):
    s = (sink.astype(F32) * LOG2E).reshape(N_KV_HEADS, GROUP, 1, 1)
    return jnp.broadcast_to(s, (N_KV_HEADS, GROUP, 2, CHUNK)).reshape(N_KV_HEADS, 1, 2 * GROUP * CHUNK)


def _sample_attn_kernel(*refs, tq, has_sink):
    if has_sink:
        q_ref, kp_ref, kc_ref, vp_ref, vc_ref, bias_ref, sink_ref, o_ref = refs
    else:
        q_ref, kp_ref, kc_ref, vp_ref, vc_ref, bias_ref, o_ref = refs
        sink_ref = None
    kcat = jnp.concatenate([kp_ref[...], kc_ref[...]], axis=0).astype(BF16)
    vcat = jnp.concatenate([vp_ref[...], vc_ref[...]], axis=0).astype(BF16)
    lane_head = lax.broadcasted_iota(jnp.int32, (tq, KV_WIDTH), 1) // HEAD_DIM
    qc = q_ref[...]
    qs = jnp.concatenate(
        [jnp.where(lane_head == h, qc[:, g * KV_WIDTH:(g + 1) * KV_WIDTH], 0)
         for h in range(N_KV_HEADS) for g in range(GROUP)], axis=0).astype(BF16)
    s = lax.dot_general(qs, kcat, (((1,), (1,)), ((), ())), preferred_element_type=F32)
    s = s + bias_ref[...]
    m = jnp.max(s, axis=-1, keepdims=True)
    if has_sink:
        m = jnp.maximum(m, sink_ref[...])
    e = jnp.exp2(s - m)
    denom = jnp.sum(e, axis=-1, keepdims=True)
    if has_sink:
        denom = denom + jnp.exp2(sink_ref[...] - m)
    p = (e * (1.0 / denom)).astype(BF16)
    o_all = jnp.dot(p, vcat, preferred_element_type=F32)
    for g in range(GROUP):
        o = jnp.zeros((tq, KV_WIDTH), F32)
        for h in range(N_KV_HEADS):
            r0 = (h * GROUP + g) * tq
            o = jnp.where(lane_head == h, o_all[r0:r0 + tq, :], o)
        o_ref[:, g * KV_WIDTH:(g + 1) * KV_WIDTH] = o.astype(o_ref.dtype)


def _sample_attention(q, k_cache, k_new, v_cache, v_new, bias, sink):
    b, t, dq = q.shape
    prev = k_cache.shape[1]
    assert bias.shape == (N_HEADS * t, prev + t)
    has_sink = sink is not None
    row_map = lambda bi: (bi, 0, 0)
    in_specs = [
        pl.BlockSpec((None, t, dq), row_map),
        pl.BlockSpec((None, prev, KV_WIDTH), row_map),
        pl.BlockSpec((None, t, KV_WIDTH), row_map),
        pl.BlockSpec((None, prev, KV_WIDTH), row_map),
        pl.BlockSpec((None, t, KV_WIDTH), row_map),
        _const_spec(bias.shape),
    ]
    args = [q, k_cache, k_new, v_cache, v_new, bias]
    if has_sink:
        in_specs.append(_const_spec(sink.shape))
        args.append(sink)
    return pl.pallas_call(
        functools.partial(_sample_attn_kernel, tq=t, has_sink=has_sink),
        grid=(b,),
        in_specs=in_specs,
        out_specs=pl.BlockSpec((None, t, dq), row_map),
        out_shape=jax.ShapeDtypeStruct((b, t, dq), BF16),
        compiler_params=pltpu.CompilerParams(
            dimension_semantics=("arbitrary",), vmem_limit_bytes=VMEM_LIMIT),
        name="sample_attention",
    )(*args)


def _out_mlp_kernel(x_ref, a_ref, wo_ref, g_ref, wup_ref, wdn_ref, o_ref, *, ff_chunks):
    h = x_ref[...] + jnp.dot(a_ref[...], wo_ref[...], preferred_element_type=F32)
    hn = (_rms_rows(h) * g_ref[...]).astype(BF16)
    d_ff = wup_ref.shape[1]
    step = d_ff // ff_chunks
    acc = h
    for c in range(ff_chunks):
        u = jnp.dot(hn, wup_ref[:, c * step:(c + 1) * step], preferred_element_type=F32)
        u = jnp.square(jnp.maximum(u, 0.0)).astype(BF16)
        acc = acc + jnp.dot(u, wdn_ref[c * step:(c + 1) * step, :], preferred_element_type=F32)
    o_ref[...] = acc


def _out_mlp(x, a, wo, g, wup, wdn):
    n, d = x.shape
    d_ff = wup.shape[1]
    tile = min(MLP_TILE, n)
    assert n % tile == 0
    row_spec = pl.BlockSpec((tile, d), lambda i: (i, 0))
    return pl.pallas_call(
        functools.partial(_out_mlp_kernel, ff_chunks=MLP_FF_CHUNKS),
        grid=(n // tile,),
        in_specs=[
            row_spec,
            row_spec,
            _const_spec((d, d)),
            _const_spec((1, d)),
            _const_spec((d, d_ff)),
            _const_spec((d_ff, d)),
        ],
        out_specs=row_spec,
        out_shape=jax.ShapeDtypeStruct((n, d), F32),
        compiler_params=pltpu.CompilerParams(
            dimension_semantics=("arbitrary",), vmem_limit_bytes=VMEM_LIMIT),
        name="out_proj_mlp",
    )(x, a, wo, g.reshape(1, d), wup, wdn)


def _rel_bias_kernel(tab_ref, o_ref, *, tk, delta):
    n_pad = tab_ref.shape[-1]
    r = lax.broadcasted_iota(jnp.int32, (n_pad, tk), 0)
    j = lax.broadcasted_iota(jnp.int32, (n_pad, tk), 1)
    for qi in range(o_ref.shape[0]):
        q = pl.program_id(1) * o_ref.shape[0] + qi
        idx = jnp.clip(q + delta - j, -MAX_REL, MAX_REL) + MAX_REL
        onehot = jnp.where(r == idx, 1.0, 0.0).astype(BF16)
        acc = jnp.dot(tab_ref[0], onehot, preferred_element_type=F32)
        acc = acc + jnp.dot(tab_ref[1], onehot, preferred_element_type=F32)
        acc = acc + jnp.dot(tab_ref[2], onehot, preferred_element_type=F32)
        o_ref[qi] = acc


def _rel_bias(table, tq, tk, delta):
    n_layers, h, n_rel = table.shape
    n_pad = -(-n_rel // LANES) * LANES
    t = jnp.pad(table, ((0, 0), (0, 0), (0, n_pad - n_rel)))
    hi = t.astype(BF16)
    mid = (t - hi.astype(F32)).astype(BF16)
    lo = (t - hi.astype(F32) - mid.astype(F32)).astype(BF16)
    parts = jnp.stack([hi, mid, lo], axis=1)
    return pl.pallas_call(
        functools.partial(_rel_bias_kernel, tk=tk, delta=delta),
        grid=(n_layers, tq // REL_BIAS_ROWS),
        in_specs=[pl.BlockSpec((None, 3, h, n_pad), lambda l, q: (l, 0, 0, 0))],
        out_specs=pl.BlockSpec((None, REL_BIAS_ROWS, h, tk), lambda l, q: (l, q, 0, 0)),
        out_shape=jax.ShapeDtypeStruct((n_layers, tq, h, tk), F32),
        compiler_params=pltpu.CompilerParams(dimension_semantics=("arbitrary", "arbitrary")),
        name="rel_bias_table",
    )(parts)


def _np_band_mask(q_pos, k_pos, left_chunks):
    qc = q_pos[:, None] // CHUNK
    kc = k_pos[None, :] // CHUNK
    return (k_pos[None, :] >= 0) & (kc <= qc) & (kc >= qc - left_chunks)


def _alibi_bias(q_pos, k_pos, static_mask):
    slopes = (2.0 ** (-8.0 * np.arange(1, N_HEADS + 1, dtype=np.float32) / N_HEADS)).astype(np.float32)
    dist = np.abs(q_pos[:, None] - k_pos[None, :]).astype(np.float32)
    bias = -slopes[:, None, None] * dist[None] * np.float32(LOG2E)
    if static_mask:
        bias = np.where(_np_band_mask(q_pos, k_pos, LEFT_CHUNKS_A)[None], bias, np.float32(NEG))
    return jnp.asarray(bias.astype(np.float32))


def _row_sink(sink, tq):
    return jnp.repeat(sink.astype(F32) * LOG2E, tq).reshape(N_HEADS * tq, 1)


def kernel(x_prompt, x_sample, cache_k_a, cache_v_a, cache_k_b, cache_v_b, g_attn, g_mlp, w_qkv_a,
           g_q_a, g_k_a, sink_a, w_o_a, g_kv, w_kv, g_k_b, w_q_b, g_q_b, rel_bias_b, w_o_b, w_up, w_down):
    batch, seq, d = x_prompt.shape
    dec_batch, t_new, _ = x_sample.shape
    n_layers_a = w_qkv_a.shape[0]
    n_layers_b = w_q_b.shape[0]
    len_a = cache_k_a.shape[2]
    len_b = cache_k_b.shape[1]
    keep_a = min(LEFT_CHUNKS_A * CHUNK, seq)
    keep_b = min(LEFT_CHUNKS_B * CHUNK, seq)
    dq = N_HEADS * HEAD_DIM
    q_scale = HEAD_DIM ** -0.5 * LOG2E

    hp = x_prompt.reshape(batch * seq, d)
    hs = x_sample.reshape(dec_batch * t_new, d)

    pos_s = PAST_LEN + np.arange(t_new)
    kpos_a = np.concatenate([PAST_LEN - len_a + np.arange(len_a), pos_s])
    kpos_b = np.concatenate([PAST_LEN - len_b + np.arange(len_b), pos_s])
    chunk_q = np.arange(CHUNK)
    bias_a_prompt = _pair_bias(_alibi_bias(chunk_q, np.arange(-LEFT_CHUNKS_A * CHUNK, CHUNK), static_mask=False))
    bias_a_sample = _alibi_bias(pos_s, kpos_a, static_mask=True).reshape(N_HEADS * t_new, -1)

    ones_kv = jnp.ones((KV_WIDTH,), F32)
    seg_a_p = [(0, dq, True), (dq, KV_WIDTH, True)]
    seg_a_s = seg_a_p + [(dq + KV_WIDTH, KV_WIDTH, False)]
    seg_kv = [(0, KV_WIDTH, True), (KV_WIDTH, KV_WIDTH, False)]
    last_rows = lambda h, keep: h.reshape(batch, seq, d)[:, seq - keep:].reshape(batch * keep, d)
    kv4 = lambda a, n, t: a.reshape(n, t, N_KV_HEADS, HEAD_DIM)

    ka_p, va_p, ka_s, va_s = [], [], [], []
    for i in range(n_layers_a):
        wq_gm = _group_major_cols(w_qkv_a[i][:, :dq])
        w = jnp.concatenate([wq_gm, w_qkv_a[i][:, dq:]], axis=1).astype(BF16)
        wqv_t = jnp.concatenate([wq_gm, w_qkv_a[i][:, dq + KV_WIDTH:]], axis=1).T.astype(BF16)
        gains = jnp.concatenate([jnp.tile(g_q_a[i] * q_scale, N_HEADS), jnp.tile(g_k_a[i], N_KV_HEADS), ones_kv])
        wo = _group_major_rows(w_o_a[i]).astype(BF16)
        wup = w_up[i].astype(BF16)
        wdn = w_down[i].astype(BF16)

        kp16, qtp, vtp = _project(hp, g_attn[i], w[:, dq:dq + KV_WIDTH], gains[dq:dq + KV_WIDTH], seg_kv[:1],
                                  (BF16,), wt=wqv_t, gains_t=gains[:dq], qt_rows=dq, seq=seq)
        ap = _prompt_attention(qtp.reshape(batch, seq // LANES, dq, LANES), kp16.reshape(batch, seq, KV_WIDTH),
                               vtp, bias_a_prompt, _pair_sink(sink_a[i]), LEFT_CHUNKS_A)
        kp, vp = _project(last_rows(hp, keep_a), g_attn[i], w[:, dq:], gains[dq:], seg_kv, (F32, F32))
        ka_p.append(kv4(kp, batch, keep_a))
        va_p.append(kv4(vp, batch, keep_a))

        qs, ks, vs = _project(hs, g_attn[i], w, gains, seg_a_s, (BF16, F32, F32))
        ks3 = ks.reshape(dec_batch, t_new, KV_WIDTH)
        vs3 = vs.reshape(dec_batch, t_new, KV_WIDTH)
        as_ = _sample_attention(qs.reshape(dec_batch, t_new, dq),
                                cache_k_a[i].reshape(dec_batch, len_a, KV_WIDTH), ks3,
                                cache_v_a[i].reshape(dec_batch, len_a, KV_WIDTH), vs3,
                                bias_a_sample, _row_sink(sink_a[i], t_new))
        ka_s.append(ks3.reshape(dec_batch, t_new, N_KV_HEADS, HEAD_DIM))
        va_s.append(vs3.reshape(dec_batch, t_new, N_KV_HEADS, HEAD_DIM))

        hp = _out_mlp(hp, ap.reshape(batch * seq, dq), wo, g_mlp[i], wup, wdn)
        hs = _out_mlp(hs, as_.reshape(dec_batch * t_new, dq), wo, g_mlp[i], wup, wdn)

    wkv = w_kv.astype(BF16)
    gains_kv = jnp.concatenate([jnp.tile(g_k_b, N_KV_HEADS), ones_kv])
    kb_p16, vtb_p = _project(hp, g_kv, wkv[:, :KV_WIDTH], gains_kv[:KV_WIDTH], seg_kv[:1], (BF16,),
                             wt=wkv[:, KV_WIDTH:].T, seq=seq)
    kb_p, vb_p = _project(last_rows(hp, keep_b), g_kv, wkv, gains_kv, seg_kv, (F32, F32))
    kb_s, vb_s = _project(hs, g_kv, wkv, gains_kv, seg_kv, (F32, F32))
    kb_p16 = kb_p16.reshape(batch, seq, KV_WIDTH)
    kb_s3 = kb_s.reshape(dec_batch, t_new, KV_WIDTH)
    vb_s3 = vb_s.reshape(dec_batch, t_new, KV_WIDTH)
    cache_kb = cache_k_b.reshape(dec_batch, len_b, KV_WIDTH)
    cache_vb = cache_v_b.reshape(dec_batch, len_b, KV_WIDTH)

    pad_b = LEFT_CHUNKS_B * CHUNK
    tk_b = pad_b + CHUNK
    assert int(pos_s[0] - kpos_b[0]) == pad_b and t_new <= CHUNK and len_b + t_new <= tk_b
    assert np.all(np.diff(kpos_b) == 1)
    rel = _rel_bias(rel_bias_b.astype(F32) * LOG2E, CHUNK, tk_b, pad_b)
    rel = rel.transpose(0, 2, 1, 3)
    mask_s = np.where(_np_band_mask(pos_s, kpos_b, LEFT_CHUNKS_B), 0.0, NEG).astype(np.float32)

    seg_q = [(0, dq, True)]
    for j in range(n_layers_b):
        layer = n_layers_a + j
        wq = _group_major_cols(w_q_b[j]).astype(BF16)
        gains = jnp.tile(g_q_b[j] * q_scale, N_HEADS)
        wo = _group_major_rows(w_o_b[j]).astype(BF16)
        wup = w_up[layer].astype(BF16)
        wdn = w_down[layer].astype(BF16)
        bias_p = _pair_bias(rel[j])
        bias_s = (rel[j][:, :t_new, :len_b + t_new] + mask_s[None]).reshape(N_HEADS * t_new, -1)

        (qtp,) = _project(hp, g_attn[layer], None, None, [], (), wt=wq.T, gains_t=gains, qt_rows=dq)
        ap = _prompt_attention(qtp.reshape(batch, seq // LANES, dq, LANES), kb_p16, vtb_p, bias_p, None,
                               LEFT_CHUNKS_B)
        (qs,) = _project(hs, g_attn[layer], wq, gains, seg_q, (BF16,))
        as_ = _sample_attention(qs.reshape(dec_batch, t_new, dq), cache_kb, kb_s3, cache_vb, vb_s3, bias_s, None)

        hp = _out_mlp(hp, ap.reshape(batch * seq, dq), wo, g_mlp[layer], wup, wdn)
        hs = _out_mlp(hs, as_.reshape(dec_batch * t_new, dq), wo, g_mlp[layer], wup, wdn)

    return (hp.reshape(batch, seq, d), hs.reshape(dec_batch, t_new, d),
            jnp.stack(ka_p), jnp.stack(va_p),
            kv4(kb_p, batch, keep_b), kv4(vb_p, batch, keep_b),
            jnp.stack(ka_s), jnp.stack(va_s),
            kv4(kb_s3, dec_batch, t_new), kv4(vb_s3, dec_batch, t_new))
```

```python
import functools
import math

import numpy as np
import jax
import jax.numpy as jnp
from jax import lax
from jax.experimental import pallas as pl
from jax.experimental.pallas import tpu as pltpu

CHUNK = 64
HEAD_DIM = 64
N_KV_HEADS = 4
GROUP = 4
N_HEADS = N_KV_HEADS * GROUP
LEFT_CHUNKS_A = 2
LEFT_CHUNKS_B = 8
MAX_REL = 128
N_REL = 2 * MAX_REL + 1
PAST_LEN = 1024
EPS = 1e-6
NEG = -1e30
LOG2E = math.log2(math.e)

LANES = 128
KV_WIDTH = N_KV_HEADS * HEAD_DIM
GROUP_WIDTH = GROUP * HEAD_DIM
PAIR = 2 * CHUNK
ONES_ROWS = 16
REL_BIAS_ROWS = 8
ROW_TILE = 2048
MLP_TILE = 1024
MLP_FF_CHUNKS = 8
ATTN_TILE = 2048
SHORT_BAND_LOOKAHEAD = 2
LONG_BAND_LOOKAHEAD = 1
PAIR_UNROLL = 4
VMEM_LIMIT = 56 * 1024 * 1024

BF16 = jnp.bfloat16
F32 = jnp.float32


def _const_spec(shape):
    zeros = (0,) * len(shape)
    return pl.BlockSpec(shape, lambda *_: zeros, pipeline_mode=pl.Buffered(1))


def _head_sum_matrix():
    idx = np.arange(GROUP_WIDTH) // HEAD_DIM
    return jnp.asarray(idx[:, None] == idx[None, :], dtype=BF16)


def _group_major_cols(w):
    rows = w.shape[0]
    return w.reshape(rows, N_KV_HEADS, GROUP, HEAD_DIM).transpose(0, 2, 1, 3).reshape(rows, N_HEADS * HEAD_DIM)


def _group_major_rows(w):
    cols = w.shape[1]
    return w.reshape(N_KV_HEADS, GROUP, HEAD_DIM, cols).transpose(1, 0, 2, 3).reshape(N_HEADS * HEAD_DIM, cols)


def _rms_rows(x):
    return x * lax.rsqrt(jnp.mean(x * x, axis=-1, keepdims=True) + EPS)


def _proj_kernel(*refs, segments, qt_rows, vt_rows):
    refs = list(refs)
    x_ref, g_ref, hsum_ref = refs[:3]
    del refs[:3]
    if segments:
        w_ref, gain_ref = refs[:2]
        del refs[:2]
    if qt_rows or vt_rows:
        wt_ref = refs.pop(0)
    if qt_rows:
        gain_t_ref = refs.pop(0)
    out_refs = refs
    xn = (_rms_rows(x_ref[...]) * g_ref[...]).astype(BF16)
    if segments:
        if w_ref.shape[1] == GROUP_WIDTH:
            half = xn.shape[0] // 2
            y = jnp.concatenate([jnp.dot(xn[:half], w_ref[...], preferred_element_type=F32),
                                 jnp.dot(xn[half:], w_ref[...], preferred_element_type=F32)], axis=0)
        else:
            y = jnp.dot(xn, w_ref[...], preferred_element_type=F32)
        for (start, width, normed), o_ref in zip(segments, out_refs):
            for t in range(width // GROUP_WIDTH):
                lo = start + t * GROUP_WIDTH
                yt = y[:, lo:lo + GROUP_WIDTH]
                if normed:
                    ss = jnp.dot((yt * yt).astype(BF16), hsum_ref[...], preferred_element_type=F32)
                    yt = yt * lax.rsqrt(ss * (1.0 / HEAD_DIM) + EPS) * gain_ref[:, lo:lo + GROUP_WIDTH]
                o_ref[:, t * GROUP_WIDTH:(t + 1) * GROUP_WIDTH] = yt.astype(o_ref.dtype)
    if qt_rows or vt_rows:
        yt = lax.dot_general(wt_ref[...], xn, (((1,), (1,)), ((), ())), preferred_element_type=F32)
        if qt_rows:
            qt_ref = out_refs[len(segments)]
            n_tok = yt.shape[1]
            for grp in range(qt_rows // GROUP_WIDTH):
                r0 = grp * GROUP_WIDTH
                blk = yt[r0:r0 + GROUP_WIDTH, :].reshape(GROUP, HEAD_DIM, n_tok)
                scale = lax.rsqrt(jnp.mean(blk * blk, axis=1, keepdims=True) + EPS)
                gain = gain_t_ref[r0:r0 + GROUP_WIDTH, :].reshape(GROUP, HEAD_DIM, LANES)
                for lb in range(n_tok // LANES):
                    piece = blk[:, :, lb * LANES:(lb + 1) * LANES] * scale[:, :, lb * LANES:(lb + 1) * LANES] * gain
                    qt_ref[lb, r0:r0 + GROUP_WIDTH, :] = piece.reshape(GROUP_WIDTH, LANES).astype(qt_ref.dtype)
        if vt_rows:
            out_refs[-1][...] = yt[qt_rows:qt_rows + vt_rows, :].astype(out_refs[-1].dtype)


def _project(x, g, w, gains, segments, out_dtypes, wt=None, gains_t=None, qt_rows=0, seq=None):
    n, d = x.shape
    tile = min(ROW_TILE, n)
    assert n % tile == 0
    out_shape = [jax.ShapeDtypeStruct((n, width), dt) for (_, width, _), dt in zip(segments, out_dtypes)]
    out_specs = [pl.BlockSpec((tile, width), lambda i: (i, 0)) for (_, width, _) in segments]
    in_specs = [
        pl.BlockSpec((tile, d), lambda i: (i, 0)),
        _const_spec((1, d)),
        _const_spec((GROUP_WIDTH, GROUP_WIDTH)),
    ]
    args = [x, g.reshape(1, d), _head_sum_matrix()]
    if segments:
        c = w.shape[1]
        in_specs += [_const_spec((d, c)), _const_spec((1, c))]
        args += [w, gains.reshape(1, c)]
    vt_rows = 0
    if wt is not None:
        vt_rows = wt.shape[0] - qt_rows
        in_specs.append(_const_spec(wt.shape))
        args.append(wt)
    if qt_rows:
        assert tile % LANES == 0
        in_specs.append(_const_spec((qt_rows, LANES)))
        args.append(jnp.broadcast_to(gains_t.astype(F32)[:, None], (qt_rows, LANES)))
        out_shape.append(jax.ShapeDtypeStruct((n // LANES, qt_rows, LANES), BF16))
        out_specs.append(pl.BlockSpec((tile // LANES, qt_rows, LANES), lambda i: (i, 0, 0)))
    if vt_rows:
        assert seq % tile == 0 and n % seq == 0
        per_seq = seq // tile
        out_shape.append(jax.ShapeDtypeStruct((n // seq, vt_rows, seq), BF16))
        out_specs.append(pl.BlockSpec((None, vt_rows, tile), lambda i: (i // per_seq, 0, i % per_seq)))
    return pl.pallas_call(
        functools.partial(_proj_kernel, segments=tuple(segments), qt_rows=qt_rows, vt_rows=vt_rows),
        grid=(n // tile,),
        in_specs=in_specs,
        out_specs=out_specs,
        out_shape=out_shape,
        compiler_params=pltpu.CompilerParams(
            dimension_semantics=("arbitrary",), vmem_limit_bytes=VMEM_LIMIT),
        name="project",
    )(*args)


def _band_attn_kernel(*refs, prev, n_pairs, has_sink, lookahead):
    if has_sink:
        q_ref, kp_ref, kc_ref, vtp_ref, vtc_ref, bias_ref, sink_ref, o_ref, kcat, vt3, *s_bufs, ot_prev = refs
    else:
        q_ref, kp_ref, kc_ref, vtp_ref, vtc_ref, bias_ref, o_ref, kcat, vt3, *s_bufs, ot_prev = refs
        sink_ref = None
    assert len(s_bufs) == N_KV_HEADS
    tile = n_pairs * PAIR
    band = prev + PAIR
    n_cols = 2 * GROUP * CHUNK

    kcat[0:prev, :] = kp_ref[...]
    kcat[prev:prev + tile, :] = kc_ref[...]
    ones = jnp.ones((ONES_ROWS, LANES), BF16)
    blocks = [vtp_ref[:, b * LANES:(b + 1) * LANES] for b in range(prev // LANES)]
    blocks += [vtc_ref[:, b * LANES:(b + 1) * LANES] for b in range(tile // LANES)]
    for b, blk in enumerate(blocks):
        for h in range(N_KV_HEADS):
            vt3[b, h, 0:HEAD_DIM, :] = blk[h * HEAD_DIM:(h + 1) * HEAD_DIM, :]
            vt3[b, h, HEAD_DIM:HEAD_DIM + ONES_ROWS, :] = ones

    def scores(j, h, masked, s_buf):
        r0 = pl.multiple_of(j * PAIR, PAIR)
        kb = kcat[pl.ds(r0, band), :]
        cols = []
        for g in range(GROUP):
            qt = q_ref[j, g * GROUP_WIDTH + h * HEAD_DIM:g * GROUP_WIDTH + (h + 1) * HEAD_DIM, :]
            parts = []
            if h > 0:
                parts.append(jnp.zeros((h * HEAD_DIM, PAIR), BF16))
            parts.append(qt)
            if h + 1 < N_KV_HEADS:
                parts.append(jnp.zeros(((N_KV_HEADS - 1 - h) * HEAD_DIM, PAIR), BF16))
            cols.append(jnp.concatenate(parts, axis=0))
        qs = jnp.concatenate(cols, axis=1)
        s = jnp.dot(kb, qs, preferred_element_type=F32)
        s = s + bias_ref[h]
        if masked:
            valid = lax.broadcasted_iota(jnp.int32, (band, n_cols), 0) >= prev - r0
            s = jnp.where(valid, s, NEG)
        s_buf[...] = s
        return jnp.max(s, axis=0, keepdims=True)

    def softmax_pv(m, s_buf, j, h):
        if has_sink:
            m = jnp.maximum(m, sink_ref[h])
        ot = jnp.zeros((HEAD_DIM + ONES_ROWS, n_cols), F32)
        n_blocks = band // LANES
        for b in range(0, n_blocks, 2):
            width = min(2, n_blocks - b) * LANES
            eb = jnp.exp2(s_buf[b * LANES:b * LANES + width, :] - m).astype(BF16)
            vt = jnp.concatenate([vt3[j + b + i, h] for i in range(width // LANES)], axis=1)
            ot = ot + jnp.dot(vt, eb, preferred_element_type=F32)
        denom = ot[HEAD_DIM:HEAD_DIM + 1, :]
        if has_sink:
            denom = denom + jnp.exp2(sink_ref[h] - m)
        return ot[0:HEAD_DIM, :] * (1.0 / denom)

    def run(masked):
        m_first = tuple(scores(0, h, masked, s_bufs[h]) for h in range(lookahead))
        ot_prev[...] = jnp.zeros(ot_prev.shape, F32)

        def write_out(j):
            r0 = pl.multiple_of(j * PAIR, PAIR)
            for g in range(GROUP):
                og = ot_prev[:, g * PAIR:(g + 1) * PAIR].T
                o_ref[pl.ds(r0, PAIR), g * KV_WIDTH:(g + 1) * KV_WIDTH] = og.astype(o_ref.dtype)

        def body(j, ms):
            write_out(jnp.maximum(j - 1, 0))
            ms = list(ms)
            for h in range(N_KV_HEADS):
                h_ahead = h + lookahead
                if h_ahead < N_KV_HEADS:
                    ms.append(scores(j, h_ahead, masked, s_bufs[h_ahead]))
                else:
                    h_ahead -= N_KV_HEADS
                    ms.append(scores(jnp.minimum(j + 1, n_pairs - 1), h_ahead, masked, s_bufs[h_ahead]))
                ot_prev[h * HEAD_DIM:(h + 1) * HEAD_DIM, :] = softmax_pv(ms.pop(0), s_bufs[h], j, h)
            return tuple(ms)

        lax.fori_loop(0, n_pairs, body, m_first, unroll=PAIR_UNROLL)
        write_out(n_pairs - 1)

    @pl.when(pl.program_id(1) == 0)
    def _():
        run(True)

    @pl.when(pl.program_id(1) > 0)
    def _():
        run(False)


def _prompt_attention(q, k, vt, bias, sink, left_chunks):
    b, n_blocks, dq, _ = q.shape
    s = n_blocks * LANES
    prev = left_chunks * CHUNK
    tile = min(ATTN_TILE, s)
    band = prev + PAIR
    assert s % tile == 0 and tile % prev == 0 and prev % LANES == 0 and tile % PAIR == 0
    assert bias.shape == (N_KV_HEADS, band, 2 * GROUP * CHUNK)
    ratio = tile // prev
    has_sink = sink is not None
    cur_rows = lambda bi, i: (bi, i, 0)
    prev_rows = lambda bi, i: (bi, jnp.maximum(i * ratio - 1, 0), 0)
    cur_cols = lambda bi, i: (bi, 0, i)
    prev_cols = lambda bi, i: (bi, 0, jnp.maximum(i * ratio - 1, 0))
    in_specs = [
        pl.BlockSpec((None, tile // LANES, dq, LANES), lambda bi, i: (bi, i, 0, 0)),
        pl.BlockSpec((None, prev, KV_WIDTH), prev_rows),
        pl.BlockSpec((None, tile, KV_WIDTH), cur_rows),
        pl.BlockSpec((None, KV_WIDTH, prev), prev_cols),
        pl.BlockSpec((None, KV_WIDTH, tile), cur_cols),
        _const_spec(bias.shape),
    ]
    args = [q, k, k, vt, vt, bias]
    if has_sink:
        in_specs.append(_const_spec(sink.shape))
        args.append(sink)
    return pl.pallas_call(
        functools.partial(_band_attn_kernel, prev=prev, n_pairs=tile // PAIR, has_sink=has_sink,
                          lookahead=SHORT_BAND_LOOKAHEAD if band <= 2 * LANES else LONG_BAND_LOOKAHEAD),
        grid=(b, s // tile),
        in_specs=in_specs,
        out_specs=pl.BlockSpec((None, tile, dq), cur_rows),
        out_shape=jax.ShapeDtypeStruct((b, s, dq), BF16),
        scratch_shapes=[
            pltpu.VMEM((prev + tile, KV_WIDTH), BF16),
            pltpu.VMEM(((prev + tile) // LANES, N_KV_HEADS, HEAD_DIM + ONES_ROWS, LANES), BF16),
            *[pltpu.VMEM((band, 2 * GROUP * CHUNK), F32) for _ in range(N_KV_HEADS)],
            pltpu.VMEM((N_KV_HEADS * HEAD_DIM, 2 * GROUP * CHUNK), F32),
        ],
        compiler_params=pltpu.CompilerParams(
            dimension_semantics=("arbitrary", "arbitrary"), vmem_limit_bytes=VMEM_LIMIT),
        name="band_attention",
    )(*args)


def _pair_bias(per_head):
    h, tq, tk = per_head.shape
    neg = jnp.full((h, tq, CHUNK), NEG, F32)
    both = jnp.stack([jnp.concatenate([per_head, neg], axis=-1),
                      jnp.concatenate([neg, per_head], axis=-1)], axis=1)
    both = both.reshape(N_KV_HEADS, GROUP, 2, tq, tk + CHUNK)
    return both.transpose(0, 4, 1, 2, 3).reshape(N_KV_HEADS, tk + CHUNK, 2 * GROUP * tq)


def _pair_sink(sink):
    s = (sink.astype(F32) * LOG2E).reshape(N_KV_HEADS, GROUP, 1, 1)
    return jnp.broadcast_to(s, (N_KV_HEADS, GROUP, 2, CHUNK)).reshape(N_KV_HEADS, 1, 2 * GROUP * CHUNK)


def _sample_attn_kernel(*refs, tq, has_sink):
    if has_sink:
        q_ref, kp_ref, kc_ref, vp_ref, vc_ref, bias_ref, sink_ref, o_ref = refs
    else:
        q_ref, kp_ref, kc_ref, vp_ref, vc_ref, bias_ref, o_ref = refs
        sink_ref = None
    kcat = jnp.concatenate([kp_ref[...], kc_ref[...]], axis=0).astype(BF16)
    vcat = jnp.concatenate([vp_ref[...], vc_ref[...]], axis=0).astype(BF16)
    lane_head = lax.broadcasted_iota(jnp.int32, (tq, KV_WIDTH), 1) // HEAD_DIM
    qc = q_ref[...]
    qs = jnp.concatenate(
        [jnp.where(lane_head == h, qc[:, g * KV_WIDTH:(g + 1) * KV_WIDTH], 0)
         for h in range(N_KV_HEADS) for g in range(GROUP)], axis=0).astype(BF16)
    s = lax.dot_general(qs, kcat, (((1,), (1,)), ((), ())), preferred_element_type=F32)
    s = s + bias_ref[...]
    m = jnp.max(s, axis=-1, keepdims=True)
    if has_sink:
        m = jnp.maximum(m, sink_ref[...])
    e = jnp.exp2(s - m)
    denom = jnp.sum(e, axis=-1, keepdims=True)
    if has_sink:
        denom = denom + jnp.exp2(sink_ref[...] - m)
    p = (e * (1.0 / denom)).astype(BF16)
    o_all = jnp.dot(p, vcat, preferred_element_type=F32)
    for g in range(GROUP):
        o = jnp.zeros((tq, KV_WIDTH), F32)
        for h in range(N_KV_HEADS):
            r0 = (h * GROUP + g) * tq
            o = jnp.where(lane_head == h, o_all[r0:r0 + tq, :], o)
        o_ref[:, g * KV_WIDTH:(g + 1) * KV_WIDTH] = o.astype(o_ref.dtype)


def _sample_attention(q, k_cache, k_new, v_cache, v_new, bias, sink):
    b, t, dq = q.shape
    prev = k_cache.shape[1]
    assert bias.shape == (N_HEADS * t, prev + t)
    has_sink = sink is not None
    row_map = lambda bi: (bi, 0, 0)
    in_specs = [
        pl.BlockSpec((None, t, dq), row_map),
        pl.BlockSpec((None, prev, KV_WIDTH), row_map),
        pl.BlockSpec((None, t, KV_WIDTH), row_map),
        pl.BlockSpec((None, prev, KV_WIDTH), row_map),
        pl.BlockSpec((None, t, KV_WIDTH), row_map),
        _const_spec(bias.shape),
    ]
    args = [q, k_cache, k_new, v_cache, v_new, bias]
    if has_sink:
        in_specs.append(_const_spec(sink.shape))
        args.append(sink)
    return pl.pallas_call(
        functools.partial(_sample_attn_kernel, tq=t, has_sink=has_sink),
        grid=(b,),
        in_specs=in_specs,
        out_specs=pl.BlockSpec((None, t, dq), row_map),
        out_shape=jax.ShapeDtypeStruct((b, t, dq), BF16),
        compiler_params=pltpu.CompilerParams(
            dimension_semantics=("arbitrary",), vmem_limit_bytes=VMEM_LIMIT),
        name="sample_attention",
    )(*args)


def _out_mlp_kernel(x_ref, a_ref, wo_ref, g_ref, wup_ref, wdn_ref, o_ref, *, ff_chunks):
    h = x_ref[...] + jnp.dot(a_ref[...], wo_ref[...], preferred_element_type=F32)
    hn = (_rms_rows(h) * g_ref[...]).astype(BF16)
    d_ff = wup_ref.shape[1]
    step = d_ff // ff_chunks
    acc = h
    for c in range(ff_chunks):
        u = jnp.dot(hn, wup_ref[:, c * step:(c + 1) * step], preferred_element_type=F32)
        u = jnp.square(jnp.maximum(u, 0.0)).astype(BF16)
        acc = acc + jnp.dot(u, wdn_ref[c * step:(c + 1) * step, :], preferred_element_type=F32)
    o_ref[...] = acc


def _out_mlp(x, a, wo, g, wup, wdn):
    n, d = x.shape
    d_ff = wup.shape[1]
    tile = min(MLP_TILE, n)
    assert n % tile == 0
    row_spec = pl.BlockSpec((tile, d), lambda i: (i, 0))
    return pl.pallas_call(
        functools.partial(_out_mlp_kernel, ff_chunks=MLP_FF_CHUNKS),
        grid=(n // tile,),
        in_specs=[
            row_spec,
            row_spec,
            _const_spec((d, d)),
            _const_spec((1, d)),
            _const_spec((d, d_ff)),
            _const_spec((d_ff, d)),
        ],
        out_specs=row_spec,
        out_shape=jax.ShapeDtypeStruct((n, d), F32),
        compiler_params=pltpu.CompilerParams(
            dimension_semantics=("arbitrary",), vmem_limit_bytes=VMEM_LIMIT),
        name="out_proj_mlp",
    )(x, a, wo, g.reshape(1, d), wup, wdn)


def _rel_bias_kernel(tab_ref, o_ref, *, tk, delta):
    n_pad = tab_ref.shape[-1]
    r = lax.broadcasted_iota(jnp.int32, (n_pad, tk), 0)
    j = lax.broadcasted_iota(jnp.int32, (n_pad, tk), 1)
    for qi in range(o_ref.shape[0]):
        q = pl.program_id(1) * o_ref.shape[0] + qi
        idx = jnp.clip(q + delta - j, -MAX_REL, MAX_REL) + MAX_REL
        onehot = jnp.where(r == idx, 1.0, 0.0).astype(BF16)
        acc = jnp.dot(tab_ref[0], onehot, preferred_element_type=F32)
        acc = acc + jnp.dot(tab_ref[1], onehot, preferred_element_type=F32)
        acc = acc + jnp.dot(tab_ref[2], onehot, preferred_element_type=F32)
        o_ref[qi] = acc


def _rel_bias(table, tq, tk, delta):
    n_layers, h, n_rel = table.shape
    n_pad = -(-n_rel // LANES) * LANES
    t = jnp.pad(table, ((0, 0), (0, 0), (0, n_pad - n_rel)))
    hi = t.astype(BF16)
    mid = (t - hi.astype(F32)).astype(BF16)
    lo = (t - hi.astype(F32) - mid.astype(F32)).astype(BF16)
    parts = jnp.stack([hi, mid, lo], axis=1)
    return pl.pallas_call(
        functools.partial(_rel_bias_kernel, tk=tk, delta=delta),
        grid=(n_layers, tq // REL_BIAS_ROWS),
        in_specs=[pl.BlockSpec((None, 3, h, n_pad), lambda l, q: (l, 0, 0, 0))],
        out_specs=pl.BlockSpec((None, REL_BIAS_ROWS, h, tk), lambda l, q: (l, q, 0, 0)),
        out_shape=jax.ShapeDtypeStruct((n_layers, tq, h, tk), F32),
        compiler_params=pltpu.CompilerParams(dimension_semantics=("arbitrary", "arbitrary")),
        name="rel_bias_table",
    )(parts)


def _np_band_mask(q_pos, k_pos, left_chunks):
    qc = q_pos[:, None] // CHUNK
    kc = k_pos[None, :] // CHUNK
    return (k_pos[None, :] >= 0) & (kc <= qc) & (kc >= qc - left_chunks)


def _alibi_bias(q_pos, k_pos, static_mask):
    slopes = (2.0 ** (-8.0 * np.arange(1, N_HEADS + 1, dtype=np.float32) / N_HEADS)).astype(np.float32)
    dist = np.abs(q_pos[:, None] - k_pos[None, :]).astype(np.float32)
    bias = -slopes[:, None, None] * dist[None] * np.float32(LOG2E)
    if static_mask:
        bias = np.where(_np_band_mask(q_pos, k_pos, LEFT_CHUNKS_A)[None], bias, np.float32(NEG))
    return jnp.asarray(bias.astype(np.float32))


def _row_sink(sink, tq):
    return jnp.repeat(sink.astype(F32) * LOG2E, tq).reshape(N_HEADS * tq, 1)


def kernel(x_prompt, x_sample, cache_k_a, cache_v_a, cache_k_b, cache_v_b, g_attn, g_mlp, w_qkv_a,
           g_q_a, g_k_a, sink_a, w_o_a, g_kv, w_kv, g_k_b, w_q_b, g_q_b, rel_bias_b, w_o_b, w_up, w_down):
    batch, seq, d = x_prompt.shape
    dec_batch, t_new, _ = x_sample.shape
    n_layers_a = w_qkv_a.shape[0]
    n_layers_b = w_q_b.shape[0]
    len_a = cache_k_a.shape[2]
    len_b = cache_k_b.shape[1]
    keep_a = min(LEFT_CHUNKS_A * CHUNK, seq)
    keep_b = min(LEFT_CHUNKS_B * CHUNK, seq)
    dq = N_HEADS * HEAD_DIM
    q_scale = HEAD_DIM ** -0.5 * LOG2E

    hp = x_prompt.reshape(batch * seq, d)
    hs = x_sample.reshape(dec_batch * t_new, d)

    pos_s = PAST_LEN + np.arange(t_new)
    kpos_a = np.concatenate([PAST_LEN - len_a + np.arange(len_a), pos_s])
    kpos_b = np.concatenate([PAST_LEN - len_b + np.arange(len_b), pos_s])
    chunk_q = np.arange(CHUNK)
    bias_a_prompt = _pair_bias(_alibi_bias(chunk_q, np.arange(-LEFT_CHUNKS_A * CHUNK, CHUNK), static_mask=False))
    bias_a_sample = _alibi_bias(pos_s, kpos_a, static_mask=True).reshape(N_HEADS * t_new, -1)

    ones_kv = jnp.ones((KV_WIDTH,), F32)
    seg_a_p = [(0, dq, True), (dq, KV_WIDTH, True)]
    seg_a_s = seg_a_p + [(dq + KV_WIDTH, KV_WIDTH, False)]
    seg_kv = [(0, KV_WIDTH, True), (KV_WIDTH, KV_WIDTH, False)]
    last_rows = lambda h, keep: h.reshape(batch, seq, d)[:, seq - keep:].reshape(batch * keep, d)
    kv4 = lambda a, n, t: a.reshape(n, t, N_KV_HEADS, HEAD_DIM)

    ka_p, va_p, ka_s, va_s = [], [], [], []
    for i in range(n_layers_a):
        wq_gm = _group_major_cols(w_qkv_a[i][:, :dq])
        w = jnp.concatenate([wq_gm, w_qkv_a[i][:, dq:]], axis=1).astype(BF16)
        wqv_t = jnp.concatenate([wq_gm, w_qkv_a[i][:, dq + KV_WIDTH:]], axis=1).T.astype(BF16)
        gains = jnp.concatenate([jnp.tile(g_q_a[i] * q_scale, N_HEADS), jnp.tile(g_k_a[i], N_KV_HEADS), ones_kv])
        wo = _group_major_rows(w_o_a[i]).astype(BF16)
        wup = w_up[i].astype(BF16)
        wdn = w_down[i].astype(BF16)

        kp16, qtp, vtp = _project(hp, g_attn[i], w[:, dq:dq + KV_WIDTH], gains[dq:dq + KV_WIDTH], seg_kv[:1],
                                  (BF16,), wt=wqv_t, gains_t=gains[:dq], qt_rows=dq, seq=seq)
        ap = _prompt_attention(qtp.reshape(batch, seq // LANES, dq, LANES), kp16.reshape(batch, seq, KV_WIDTH),
                               vtp, bias_a_prompt, _pair_sink(sink_a[i]), LEFT_CHUNKS_A)
        kp, vp = _project(last_rows(hp, keep_a), g_attn[i], w[:, dq:], gains[dq:], seg_kv, (F32, F32))
        ka_p.append(kv4(kp, batch, keep_a))
        va_p.append(kv4(vp, batch, keep_a))

        qs, ks, vs = _project(hs, g_attn[i], w, gains, seg_a_s, (BF16, F32, F32))
        ks3 = ks.reshape(dec_batch, t_new, KV_WIDTH)
        vs3 = vs.reshape(dec_batch, t_new, KV_WIDTH)
        as_ = _sample_attention(qs.reshape(dec_batch, t_new, dq),
                                cache_k_a[i].reshape(dec_batch, len_a, KV_WIDTH), ks3,
                                cache_v_a[i].reshape(dec_batch, len_a, KV_WIDTH), vs3,
                                bias_a_sample, _row_sink(sink_a[i], t_new))
        ka_s.append(ks3.reshape(dec_batch, t_new, N_KV_HEADS, HEAD_DIM))
        va_s.append(vs3.reshape(dec_batch, t_new, N_KV_HEADS, HEAD_DIM))

        hp = _out_mlp(hp, ap.reshape(batch * seq, dq), wo, g_mlp[i], wup, wdn)
        hs = _out_mlp(hs, as_.reshape(dec_batch * t_new, dq), wo, g_mlp[i], wup, wdn)

    wkv = w_kv.astype(BF16)
    gains_kv = jnp.concatenate([jnp.tile(g_k_b, N_KV_HEADS), ones_kv])
    kb_p16, vtb_p = _project(hp, g_kv, wkv[:, :KV_WIDTH], gains_kv[:KV_WIDTH], seg_kv[:1], (BF16,),
                             wt=wkv[:, KV_WIDTH:].T, seq=seq)
    kb_p, vb_p = _project(last_rows(hp, keep_b), g_kv, wkv, gains_kv, seg_kv, (F32, F32))
    kb_s, vb_s = _project(hs, g_kv, wkv, gains_kv, seg_kv, (F32, F32))
    kb_p16 = kb_p16.reshape(batch, seq, KV_WIDTH)
    kb_s3 = kb_s.reshape(dec_batch, t_new, KV_WIDTH)
    vb_s3 = vb_s.reshape(dec_batch, t_new, KV_WIDTH)
    cache_kb = cache_k_b.reshape(dec_batch, len_b, KV_WIDTH)
    cache_vb = cache_v_b.reshape(dec_batch, len_b, KV_WIDTH)

    pad_b = LEFT_CHUNKS_B * CHUNK
    tk_b = pad_b + CHUNK
    assert int(pos_s[0] - kpos_b[0]) == pad_b and t_new <= CHUNK and len_b + t_new <= tk_b
    assert np.all(np.diff(kpos_b) == 1)
    rel = _rel_bias(rel_bias_b.astype(F32) * LOG2E, CHUNK, tk_b, pad_b)
    rel = rel.transpose(0, 2, 1, 3)
    mask_s = np.where(_np_band_mask(pos_s, kpos_b, LEFT_CHUNKS_B), 0.0, NEG).astype(np.float32)

    seg_q = [(0, dq, True)]
    for j in range(n_layers_b):
        layer = n_layers_a + j
        wq = _group_major_cols(w_q_b[j]).astype(BF16)
        gains = jnp.tile(g_q_b[j] * q_scale, N_HEADS)
        wo = _group_major_rows(w_o_b[j]).astype(BF16)
        wup = w_up[layer].astype(BF16)
        wdn = w_down[layer].astype(BF16)
        bias_p = _pair_bias(rel[j])
        bias_s = (rel[j][:, :t_new, :len_b + t_new] + mask_s[None]).reshape(N_HEADS * t_new, -1)

        (qtp,) = _project(hp, g_attn[layer], None, None, [], (), wt=wq.T, gains_t=gains, qt_rows=dq)
        ap = _prompt_attention(qtp.reshape(batch, seq // LANES, dq, LANES), kb_p16, vtb_p, bias_p, None,
                               LEFT_CHUNKS_B)
        (qs,) = _project(hs, g_attn[layer], wq, gains, seg_q, (BF16,))
        as_ = _sample_attention(qs.reshape(dec_batch, t_new, dq), cache_kb, kb_s3, cache_vb, vb_s3, bias_s, None)

        hp = _out_mlp(hp, ap.reshape(batch * seq, dq), wo, g_mlp[layer], wup, wdn)
        hs = _out_mlp(hs, as_.reshape(dec_batch * t_new, dq), wo, g_mlp[layer], wup, wdn)

    return (hp.reshape(batch, seq, d), hs.reshape(dec_batch, t_new, d),
            jnp.stack(ka_p), jnp.stack(va_p),
            kv4(kb_p, batch, keep_b), kv4(vb_p, batch, keep_b),
            jnp.stack(ka_s), jnp.stack(va_s),
            kv4(kb_s3, dec_batch, t_new), kv4(vb_s3, dec_batch, t_new))
```

```python
import functools
import math

import numpy as np
import jax
import jax.numpy as jnp
from jax import lax
from jax.experimental import pallas as pl
from jax.experimental.pallas import tpu as pltpu

CHUNK = 64
HEAD_DIM = 64
N_KV_HEADS = 4
GROUP = 4
N_HEADS = N_KV_HEADS * GROUP
LEFT_CHUNKS_A = 2
LEFT_CHUNKS_B = 8
MAX_REL = 128
N_REL = 2 * MAX_REL + 1
PAST_LEN = 1024
EPS = 1e-6
NEG = -1e30
LOG2E = math.log2(math.e)

LANES = 128
KV_WIDTH = N_KV_HEADS * HEAD_DIM
GROUP_WIDTH = GROUP * HEAD_DIM
PAIR = 2 * CHUNK
ONES_ROWS = 16
REL_BIAS_ROWS = 8
ROW_TILE = 2048
MLP_TILE = 1024
MLP_FF_CHUNKS = 8
ATTN_TILE = 2048
SHORT_BAND_LOOKAHEAD = 2
LONG_BAND_LOOKAHEAD = 2
PAIR_UNROLL = 4
VMEM_LIMIT = 56 * 1024 * 1024

BF16 = jnp.bfloat16
F32 = jnp.float32


def _const_spec(shape):
    zeros = (0,) * len(shape)
    return pl.BlockSpec(shape, lambda *_: zeros, pipeline_mode=pl.Buffered(1))


def _head_sum_matrix():
    idx = np.arange(GROUP_WIDTH) // HEAD_DIM
    return jnp.asarray(idx[:, None] == idx[None, :], dtype=BF16)


def _group_major_cols(w):
    rows = w.shape[0]
    return w.reshape(rows, N_KV_HEADS, GROUP, HEAD_DIM).transpose(0, 2, 1, 3).reshape(rows, N_HEADS * HEAD_DIM)


def _group_major_rows(w):
    cols = w.shape[1]
    return w.reshape(N_KV_HEADS, GROUP, HEAD_DIM, cols).transpose(1, 0, 2, 3).reshape(N_HEADS * HEAD_DIM, cols)


def _rms_rows(x):
    return x * lax.rsqrt(jnp.mean(x * x, axis=-1, keepdims=True) + EPS)


def _proj_kernel(*refs, segments, qt_rows, vt_rows):
    refs = list(refs)
    x_ref, g_ref, hsum_ref = refs[:3]
    del refs[:3]
    if segments:
        w_ref, gain_ref = refs[:2]
        del refs[:2]
    if qt_rows or vt_rows:
        wt_ref = refs.pop(0)
    if qt_rows:
        gain_t_ref = refs.pop(0)
    out_refs = refs
    xn = (_rms_rows(x_ref[...]) * g_ref[...]).astype(BF16)
    if segments:
        if w_ref.shape[1] == GROUP_WIDTH:
            half = xn.shape[0] // 2
            y = jnp.concatenate([jnp.dot(xn[:half], w_ref[...], preferred_element_type=F32),
                                 jnp.dot(xn[half:], w_ref[...], preferred_element_type=F32)], axis=0)
        else:
            y = jnp.dot(xn, w_ref[...], preferred_element_type=F32)
        for (start, width, normed), o_ref in zip(segments, out_refs):
            for t in range(width // GROUP_WIDTH):
                lo = start + t * GROUP_WIDTH
                yt = y[:, lo:lo + GROUP_WIDTH]
                if normed:
                    ss = jnp.dot((yt * yt).astype(BF16), hsum_ref[...], preferred_element_type=F32)
                    yt = yt * lax.rsqrt(ss * (1.0 / HEAD_DIM) + EPS) * gain_ref[:, lo:lo + GROUP_WIDTH]
                o_ref[:, t * GROUP_WIDTH:(t + 1) * GROUP_WIDTH] = yt.astype(o_ref.dtype)
    if qt_rows or vt_rows:
        yt = lax.dot_general(wt_ref[...], xn, (((1,), (1,)), ((), ())), preferred_element_type=F32)
        if qt_rows:
            qt_ref = out_refs[len(segments)]
            n_tok = yt.shape[1]
            for grp in range(qt_rows // GROUP_WIDTH):
                r0 = grp * GROUP_WIDTH
                blk = yt[r0:r0 + GROUP_WIDTH, :].reshape(GROUP, HEAD_DIM, n_tok)
                scale = lax.rsqrt(jnp.mean(blk * blk, axis=1, keepdims=True) + EPS)
                gain = gain_t_ref[r0:r0 + GROUP_WIDTH, :].reshape(GROUP, HEAD_DIM, LANES)
                for lb in range(n_tok // LANES):
                    piece = blk[:, :, lb * LANES:(lb + 1) * LANES] * scale[:, :, lb * LANES:(lb + 1) * LANES] * gain
                    qt_ref[lb, r0:r0 + GROUP_WIDTH, :] = piece.reshape(GROUP_WIDTH, LANES).astype(qt_ref.dtype)
        if vt_rows:
            out_refs[-1][...] = yt[qt_rows:qt_rows + vt_rows, :].astype(out_refs[-1].dtype)


def _project(x, g, w, gains, segments, out_dtypes, wt=None, gains_t=None, qt_rows=0, seq=None):
    n, d = x.shape
    tile = min(ROW_TILE, n)
    assert n % tile == 0
    out_shape = [jax.ShapeDtypeStruct((n, width), dt) for (_, width, _), dt in zip(segments, out_dtypes)]
    out_specs = [pl.BlockSpec((tile, width), lambda i: (i, 0)) for (_, width, _) in segments]
    in_specs = [
        pl.BlockSpec((tile, d), lambda i: (i, 0)),
        _const_spec((1, d)),
        _const_spec((GROUP_WIDTH, GROUP_WIDTH)),
    ]
    args = [x, g.reshape(1, d), _head_sum_matrix()]
    if segments:
        c = w.shape[1]
        in_specs += [_const_spec((d, c)), _const_spec((1, c))]
        args += [w, gains.reshape(1, c)]
    vt_rows = 0
    if wt is not None:
        vt_rows = wt.shape[0] - qt_rows
        in_specs.append(_const_spec(wt.shape))
        args.append(wt)
    if qt_rows:
        assert tile % LANES == 0
        in_specs.append(_const_spec((qt_rows, LANES)))
        args.append(jnp.broadcast_to(gains_t.astype(F32)[:, None], (qt_rows, LANES)))
        out_shape.append(jax.ShapeDtypeStruct((n // LANES, qt_rows, LANES), BF16))
        out_specs.append(pl.BlockSpec((tile // LANES, qt_rows, LANES), lambda i: (i, 0, 0)))
    if vt_rows:
        assert seq % tile == 0 and n % seq == 0
        per_seq = seq // tile
        out_shape.append(jax.ShapeDtypeStruct((n // seq, vt_rows, seq), BF16))
        out_specs.append(pl.BlockSpec((None, vt_rows, tile), lambda i: (i // per_seq, 0, i % per_seq)))
    return pl.pallas_call(
        functools.partial(_proj_kernel, segments=tuple(segments), qt_rows=qt_rows, vt_rows=vt_rows),
        grid=(n // tile,),
        in_specs=in_specs,
        out_specs=out_specs,
        out_shape=out_shape,
        compiler_params=pltpu.CompilerParams(
            dimension_semantics=("arbitrary",), vmem_limit_bytes=VMEM_LIMIT),
        name="project",
    )(*args)


def _band_attn_kernel(*refs, prev, n_pairs, has_sink, lookahead):
    if has_sink:
        q_ref, kp_ref, kc_ref, vtp_ref, vtc_ref, bias_ref, sink_ref, o_ref, kcat, vt3, *s_bufs, ot_prev = refs
    else:
        q_ref, kp_ref, kc_ref, vtp_ref, vtc_ref, bias_ref, o_ref, kcat, vt3, *s_bufs, ot_prev = refs
        sink_ref = None
    assert len(s_bufs) == N_KV_HEADS
    tile = n_pairs * PAIR
    band = prev + PAIR
    n_cols = 2 * GROUP * CHUNK

    kcat[0:prev, :] = kp_ref[...]
    kcat[prev:prev + tile, :] = kc_ref[...]
    ones = jnp.ones((ONES_ROWS, LANES), BF16)
    blocks = [vtp_ref[:, b * LANES:(b + 1) * LANES] for b in range(prev // LANES)]
    blocks += [vtc_ref[:, b * LANES:(b + 1) * LANES] for b in range(tile // LANES)]
    for b, blk in enumerate(blocks):
        for h in range(N_KV_HEADS):
            vt3[b, h, 0:HEAD_DIM, :] = blk[h * HEAD_DIM:(h + 1) * HEAD_DIM, :]
            vt3[b, h, HEAD_DIM:HEAD_DIM + ONES_ROWS, :] = ones

    def scores(j, h, masked, s_buf):
        r0 = pl.multiple_of(j * PAIR, PAIR)
        kb = kcat[pl.ds(r0, band), :]
        cols = []
        for g in range(GROUP):
            qt = q_ref[j, g * GROUP_WIDTH + h * HEAD_DIM:g * GROUP_WIDTH + (h + 1) * HEAD_DIM, :]
            parts = []
            if h > 0:
                parts.append(jnp.zeros((h * HEAD_DIM, PAIR), BF16))
            parts.append(qt)
            if h + 1 < N_KV_HEADS:
                parts.append(jnp.zeros(((N_KV_HEADS - 1 - h) * HEAD_DIM, PAIR), BF16))
            cols.append(jnp.concatenate(parts, axis=0))
        qs = jnp.concatenate(cols, axis=1)
        s = jnp.dot(kb, qs, preferred_element_type=F32)
        s = s + bias_ref[h]
        if masked:
            valid = lax.broadcasted_iota(jnp.int32, (band, n_cols), 0) >= prev - r0
            s = jnp.where(valid, s, NEG)
        s_buf[...] = s
        return jnp.max(s, axis=0, keepdims=True)

    def softmax_pv(m, s_buf, j, h):
        if has_sink:
            m = jnp.maximum(m, sink_ref[h])
        ot = jnp.zeros((HEAD_DIM + ONES_ROWS, n_cols), F32)
        n_blocks = band // LANES
        for b in range(0, n_blocks, 2):
            width = min(2, n_blocks - b) * LANES
            eb = jnp.exp2(s_buf[b * LANES:b * LANES + width, :] - m).astype(BF16)
            vt = jnp.concatenate([vt3[j + b + i, h] for i in range(width // LANES)], axis=1)
            ot = ot + jnp.dot(vt, eb, preferred_element_type=F32)
        denom = ot[HEAD_DIM:HEAD_DIM + 1, :]
        if has_sink:
            denom = denom + jnp.exp2(sink_ref[h] - m)
        return ot[0:HEAD_DIM, :] * (1.0 / denom)

    def run(masked):
        m_first = tuple(scores(0, h, masked, s_bufs[h]) for h in range(lookahead))
        ot_prev[...] = jnp.zeros(ot_prev.shape, F32)

        def write_out(j):
            r0 = pl.multiple_of(j * PAIR, PAIR)
            for g in range(GROUP):
                og = ot_prev[:, g * PAIR:(g + 1) * PAIR].T
                o_ref[pl.ds(r0, PAIR), g * KV_WIDTH:(g + 1) * KV_WIDTH] = og.astype(o_ref.dtype)

        def body(j, ms):
            write_out(jnp.maximum(j - 1, 0))
            ms = list(ms)
            for h in range(N_KV_HEADS):
                h_ahead = h + lookahead
                if h_ahead < N_KV_HEADS:
                    ms.append(scores(j, h_ahead, masked, s_bufs[h_ahead]))
                else:
                    h_ahead -= N_KV_HEADS
                    ms.append(scores(jnp.minimum(j + 1, n_pairs - 1), h_ahead, masked, s_bufs[h_ahead]))
                ot_prev[h * HEAD_DIM:(h + 1) * HEAD_DIM, :] = softmax_pv(ms.pop(0), s_bufs[h], j, h)
            return tuple(ms)

        lax.fori_loop(0, n_pairs, body, m_first, unroll=PAIR_UNROLL)
        write_out(n_pairs - 1)

    @pl.when(pl.program_id(1) == 0)
    def _():
        run(True)

    @pl.when(pl.program_id(1) > 0)
    def _():
        run(False)


def _prompt_attention(q, k, vt, bias, sink, left_chunks):
    b, n_blocks, dq, _ = q.shape
    s = n_blocks * LANES
    prev = left_chunks * CHUNK
    tile = min(ATTN_TILE, s)
    band = prev + PAIR
    assert s % tile == 0 and tile % prev == 0 and prev % LANES == 0 and tile % PAIR == 0
    assert bias.shape == (N_KV_HEADS, band, 2 * GROUP * CHUNK)
    ratio = tile // prev
    has_sink = sink is not None
    cur_rows = lambda bi, i: (bi, i, 0)
    prev_rows = lambda bi, i: (bi, jnp.maximum(i * ratio - 1, 0), 0)
    cur_cols = lambda bi, i: (bi, 0, i)
    prev_cols = lambda bi, i: (bi, 0, jnp.maximum(i * ratio - 1, 0))
    in_specs = [
        pl.BlockSpec((None, tile // LANES, dq, LANES), lambda bi, i: (bi, i, 0, 0)),
        pl.BlockSpec((None, prev, KV_WIDTH), prev_rows),
        pl.BlockSpec((None, tile, KV_WIDTH), cur_rows),
        pl.BlockSpec((None, KV_WIDTH, prev), prev_cols),
        pl.BlockSpec((None, KV_WIDTH, tile), cur_cols),
        _const_spec(bias.shape),
    ]
    args = [q, k, k, vt, vt, bias]
    if has_sink:
        in_specs.append(_const_spec(sink.shape))
        args.append(sink)
    return pl.pallas_call(
        functools.partial(_band_attn_kernel, prev=prev, n_pairs=tile // PAIR, has_sink=has_sink,
                          lookahead=SHORT_BAND_LOOKAHEAD if band <= 2 * LANES else LONG_BAND_LOOKAHEAD),
        grid=(b, s // tile),
        in_specs=in_specs,
        out_specs=pl.BlockSpec((None, tile, dq), cur_rows),
        out_shape=jax.ShapeDtypeStruct((b, s, dq), BF16),
        scratch_shapes=[
            pltpu.VMEM((prev + tile, KV_WIDTH), BF16),
            pltpu.VMEM(((prev + tile) // LANES, N_KV_HEADS, HEAD_DIM + ONES_ROWS, LANES), BF16),
            *[pltpu.VMEM((band, 2 * GROUP * CHUNK), F32) for _ in range(N_KV_HEADS)],
            pltpu.VMEM((N_KV_HEADS * HEAD_DIM, 2 * GROUP * CHUNK), F32),
        ],
        compiler_params=pltpu.CompilerParams(
            dimension_semantics=("arbitrary", "arbitrary"), vmem_limit_bytes=VMEM_LIMIT),
        name="band_attention",
    )(*args)


def _pair_bias(per_head):
    h, tq, tk = per_head.shape
    neg = jnp.full((h, tq, CHUNK), NEG, F32)
    both = jnp.stack([jnp.concatenate([per_head, neg], axis=-1),
                      jnp.concatenate([neg, per_head], axis=-1)], axis=1)
    both = both.reshape(N_KV_HEADS, GROUP, 2, tq, tk + CHUNK)
    return both.transpose(0, 4, 1, 2, 3).reshape(N_KV_HEADS, tk + CHUNK, 2 * GROUP * tq)


def _pair_sink(sink):
    s = (sink.astype(F32) * LOG2E).reshape(N_KV_HEADS, GROUP, 1, 1)
    return jnp.broadcast_to(s, (N_KV_HEADS, GROUP, 2, CHUNK)).reshape(N_KV_HEADS, 1, 2 * GROUP * CHUNK)


def _sample_attn_kernel(*refs, tq, has_sink):
    if has_sink:
        q_ref, kp_ref, kc_ref, vp_ref, vc_ref, bias_ref, sink_ref, o_ref = refs
    else:
        q_ref, kp_ref, kc_ref, vp_ref, vc_ref, bias_ref, o_ref = refs
        sink_ref = None
    kcat = jnp.concatenate([kp_ref[...], kc_ref[...]], axis=0).astype(BF16)
    vcat = jnp.concatenate([vp_ref[...], vc_ref[...]], axis=0).astype(BF16)
    lane_head = lax.broadcasted_iota(jnp.int32, (tq, KV_WIDTH), 1) // HEAD_DIM
    qc = q_ref[...]
    qs = jnp.concatenate(
        [jnp.where(lane_head == h, qc[:, g * KV_WIDTH:(g + 1) * KV_WIDTH], 0)
         for h in range(N_KV_HEADS) for g in range(GROUP)], axis=0).astype(BF16)
    s = lax.dot_general(qs, kcat, (((1,), (1,)), ((), ())), preferred_element_type=F32)
    s = s + bias_ref[...]
    m = jnp.max(s, axis=-1, keepdims=True)
    if has_sink:
        m = jnp.maximum(m, sink_ref[...])
    e = jnp.exp2(s - m)
    denom = jnp.sum(e, axis=-1, keepdims=True)
    if has_sink:
        denom = denom + jnp.exp2(sink_ref[...] - m)
    p = (e * (1.0 / denom)).astype(BF16)
    o_all = jnp.dot(p, vcat, preferred_element_type=F32)
    for g in range(GROUP):
        o = jnp.zeros((tq, KV_WIDTH), F32)
        for h in range(N_KV_HEADS):
            r0 = (h * GROUP + g) * tq
            o = jnp.where(lane_head == h, o_all[r0:r0 + tq, :], o)
        o_ref[:, g * KV_WIDTH:(g + 1) * KV_WIDTH] = o.astype(o_ref.dtype)


def _sample_attention(q, k_cache, k_new, v_cache, v_new, bias, sink):
    b, t, dq = q.shape
    prev = k_cache.shape[1]
    assert bias.shape == (N_HEADS * t, prev + t)
    has_sink = sink is not None
    row_map = lambda bi: (bi, 0, 0)
    in_specs = [
        pl.BlockSpec((None, t, dq), row_map),
        pl.BlockSpec((None, prev, KV_WIDTH), row_map),
        pl.BlockSpec((None, t, KV_WIDTH), row_map),
        pl.BlockSpec((None, prev, KV_WIDTH), row_map),
        pl.BlockSpec((None, t, KV_WIDTH), row_map),
        _const_spec(bias.shape),
    ]
    args = [q, k_cache, k_new, v_cache, v_new, bias]
    if has_sink:
        in_specs.append(_const_spec(sink.shape))
        args.append(sink)
    return pl.pallas_call(
        functools.partial(_sample_attn_kernel, tq=t, has_sink=has_sink),
        grid=(b,),
        in_specs=in_specs,
        out_specs=pl.BlockSpec((None, t, dq), row_map),
        out_shape=jax.ShapeDtypeStruct((b, t, dq), BF16),
        compiler_params=pltpu.CompilerParams(
            dimension_semantics=("arbitrary",), vmem_limit_bytes=VMEM_LIMIT),
        name="sample_attention",
    )(*args)


def _out_mlp_kernel(x_ref, a_ref, wo_ref, g_ref, wup_ref, wdn_ref, o_ref, *, ff_chunks):
    h = x_ref[...] + jnp.dot(a_ref[...], wo_ref[...], preferred_element_type=F32)
    hn = (_rms_rows(h) * g_ref[...]).astype(BF16)
    d_ff = wup_ref.shape[1]
    step = d_ff // ff_chunks
    acc = h
    for c in range(ff_chunks):
        u = jnp.dot(hn, wup_ref[:, c * step:(c + 1) * step], preferred_element_type=F32)
        u = jnp.square(jnp.maximum(u, 0.0)).astype(BF16)
        acc = acc + jnp.dot(u, wdn_ref[c * step:(c + 1) * step, :], preferred_element_type=F32)
    o_ref[...] = acc


def _out_mlp(x, a, wo, g, wup, wdn):
    n, d = x.shape
    d_ff = wup.shape[1]
    tile = min(MLP_TILE, n)
    assert n % tile == 0
    row_spec = pl.BlockSpec((tile, d), lambda i: (i, 0))
    return pl.pallas_call(
        functools.partial(_out_mlp_kernel, ff_chunks=MLP_FF_CHUNKS),
        grid=(n // tile,),
        in_specs=[
            row_spec,
            row_spec,
            _const_spec((d, d)),
            _const_spec((1, d)),
            _const_spec((d, d_ff)),
            _const_spec((d_ff, d)),
        ],
        out_specs=row_spec,
        out_shape=jax.ShapeDtypeStruct((n, d), F32),
        compiler_params=pltpu.CompilerParams(
            dimension_semantics=("arbitrary",), vmem_limit_bytes=VMEM_LIMIT),
        name="out_proj_mlp",
    )(x, a, wo, g.reshape(1, d), wup, wdn)


def _rel_bias_kernel(tab_ref, o_ref, *, tk, delta):
    n_pad = tab_ref.shape[-1]
    r = lax.broadcasted_iota(jnp.int32, (n_pad, tk), 0)
    j = lax.broadcasted_iota(jnp.int32, (n_pad, tk), 1)
    for qi in range(o_ref.shape[0]):
        q = pl.program_id(1) * o_ref.shape[0] + qi
        idx = jnp.clip(q + delta - j, -MAX_REL, MAX_REL) + MAX_REL
        onehot = jnp.where(r == idx, 1.0, 0.0).astype(BF16)
        acc = jnp.dot(tab_ref[0], onehot, preferred_element_type=F32)
        acc = acc + jnp.dot(tab_ref[1], onehot, preferred_element_type=F32)
        acc = acc + jnp.dot(tab_ref[2], onehot, preferred_element_type=F32)
        o_ref[qi] = acc


def _rel_bias(table, tq, tk, delta):
    n_layers, h, n_rel = table.shape
    n_pad = -(-n_rel // LANES) * LANES
    t = jnp.pad(table, ((0, 0), (0, 0), (0, n_pad - n_rel)))
    hi = t.astype(BF16)
    mid = (t - hi.astype(F32)).astype(BF16)
    lo = (t - hi.astype(F32) - mid.astype(F32)).astype(BF16)
    parts = jnp.stack([hi, mid, lo], axis=1)
    return pl.pallas_call(
        functools.partial(_rel_bias_kernel, tk=tk, delta=delta),
        grid=(n_layers, tq // REL_BIAS_ROWS),
        in_specs=[pl.BlockSpec((None, 3, h, n_pad), lambda l, q: (l, 0, 0, 0))],
        out_specs=pl.BlockSpec((None, REL_BIAS_ROWS, h, tk), lambda l, q: (l, q, 0, 0)),
        out_shape=jax.ShapeDtypeStruct((n_layers, tq, h, tk), F32),
        compiler_params=pltpu.CompilerParams(dimension_semantics=("arbitrary", "arbitrary")),
        name="rel_bias_table",
    )(parts)


def _np_band_mask(q_pos, k_pos, left_chunks):
    qc = q_pos[:, None] // CHUNK
    kc = k_pos[None, :] // CHUNK
    return (k_pos[None, :] >= 0) & (kc <= qc) & (kc >= qc - left_chunks)


def _alibi_bias(q_pos, k_pos, static_mask):
    slopes = (2.0 ** (-8.0 * np.arange(1, N_HEADS + 1, dtype=np.float32) / N_HEADS)).astype(np.float32)
    dist = np.abs(q_pos[:, None] - k_pos[None, :]).astype(np.float32)
    bias = -slopes[:, None, None] * dist[None] * np.float32(LOG2E)
    if static_mask:
        bias = np.where(_np_band_mask(q_pos, k_pos, LEFT_CHUNKS_A)[None], bias, np.float32(NEG))
    return jnp.asarray(bias.astype(np.float32))


def _row_sink(sink, tq):
    return jnp.repeat(sink.astype(F32) * LOG2E, tq).reshape(N_HEADS * tq, 1)


def kernel(x_prompt, x_sample, cache_k_a, cache_v_a, cache_k_b, cache_v_b, g_attn, g_mlp, w_qkv_a,
           g_q_a, g_k_a, sink_a, w_o_a, g_kv, w_kv, g_k_b, w_q_b, g_q_b, rel_bias_b, w_o_b, w_up, w_down):
    batch, seq, d = x_prompt.shape
    dec_batch, t_new, _ = x_sample.shape
    n_layers_a = w_qkv_a.shape[0]
    n_layers_b = w_q_b.shape[0]
    len_a = cache_k_a.shape[2]
    len_b = cache_k_b.shape[1]
    keep_a = min(LEFT_CHUNKS_A * CHUNK, seq)
    keep_b = min(LEFT_CHUNKS_B * CHUNK, seq)
    dq = N_HEADS * HEAD_DIM
    q_scale = HEAD_DIM ** -0.5 * LOG2E

    hp = x_prompt.reshape(batch * seq, d)
    hs = x_sample.reshape(dec_batch * t_new, d)

    pos_s = PAST_LEN + np.arange(t_new)
    kpos_a = np.concatenate([PAST_LEN - len_a + np.arange(len_a), pos_s])
    kpos_b = np.concatenate([PAST_LEN - len_b + np.arange(len_b), pos_s])
    chunk_q = np.arange(CHUNK)
    bias_a_prompt = _pair_bias(_alibi_bias(chunk_q, np.arange(-LEFT_CHUNKS_A * CHUNK, CHUNK), static_mask=False))
    bias_a_sample = _alibi_bias(pos_s, kpos_a, static_mask=True).reshape(N_HEADS * t_new, -1)

    ones_kv = jnp.ones((KV_WIDTH,), F32)
    seg_a_p = [(0, dq, True), (dq, KV_WIDTH, True)]
    seg_a_s = seg_a_p + [(dq + KV_WIDTH, KV_WIDTH, False)]
    seg_kv = [(0, KV_WIDTH, True), (KV_WIDTH, KV_WIDTH, False)]
    last_rows = lambda h, keep: h.reshape(batch, seq, d)[:, seq - keep:].reshape(batch * keep, d)
    kv4 = lambda a, n, t: a.reshape(n, t, N_KV_HEADS, HEAD_DIM)

    ka_p, va_p, ka_s, va_s = [], [], [], []
    for i in range(n_layers_a):
        wq_gm = _group_major_cols(w_qkv_a[i][:, :dq])
        w = jnp.concatenate([wq_gm, w_qkv_a[i][:, dq:]], axis=1).astype(BF16)
        wqv_t = jnp.concatenate([wq_gm, w_qkv_a[i][:, dq + KV_WIDTH:]], axis=1).T.astype(BF16)
        gains = jnp.concatenate([jnp.tile(g_q_a[i] * q_scale, N_HEADS), jnp.tile(g_k_a[i], N_KV_HEADS), ones_kv])
        wo = _group_major_rows(w_o_a[i]).astype(BF16)
        wup = w_up[i].astype(BF16)
        wdn = w_down[i].astype(BF16)

        kp16, qtp, vtp = _project(hp, g_attn[i], w[:, dq:dq + KV_WIDTH], gains[dq:dq + KV_WIDTH], seg_kv[:1],
                                  (BF16,), wt=wqv_t, gains_t=gains[:dq], qt_rows=dq, seq=seq)
        ap = _prompt_attention(qtp.reshape(batch, seq // LANES, dq, LANES), kp16.reshape(batch, seq, KV_WIDTH),
                               vtp, bias_a_prompt, _pair_sink(sink_a[i]), LEFT_CHUNKS_A)
        kp, vp = _project(last_rows(hp, keep_a), g_attn[i], w[:, dq:], gains[dq:], seg_kv, (F32, F32))
        ka_p.append(kv4(kp, batch, keep_a))
        va_p.append(kv4(vp, batch, keep_a))

        qs, ks, vs = _project(hs, g_attn[i], w, gains, seg_a_s, (BF16, F32, F32))
        ks3 = ks.reshape(dec_batch, t_new, KV_WIDTH)
        vs3 = vs.reshape(dec_batch, t_new, KV_WIDTH)
        as_ = _sample_attention(qs.reshape(dec_batch, t_new, dq),
                                cache_k_a[i].reshape(dec_batch, len_a, KV_WIDTH), ks3,
                                cache_v_a[i].reshape(dec_batch, len_a, KV_WIDTH), vs3,
                                bias_a_sample, _row_sink(sink_a[i], t_new))
        ka_s.append(ks3.reshape(dec_batch, t_new, N_KV_HEADS, HEAD_DIM))
        va_s.append(vs3.reshape(dec_batch, t_new, N_KV_HEADS, HEAD_DIM))

        hp = _out_mlp(hp, ap.reshape(batch * seq, dq), wo, g_mlp[i], wup, wdn)
        hs = _out_mlp(hs, as_.reshape(dec_batch * t_new, dq), wo, g_mlp[i], wup, wdn)

    wkv = w_kv.astype(BF16)
    gains_kv = jnp.concatenate([jnp.tile(g_k_b, N_KV_HEADS), ones_kv])
    kb_p16, vtb_p = _project(hp, g_kv, wkv[:, :KV_WIDTH], gains_kv[:KV_WIDTH], seg_kv[:1], (BF16,),
                             wt=wkv[:, KV_WIDTH:].T, seq=seq)
    kb_p, vb_p = _project(last_rows(hp, keep_b), g_kv, wkv, gains_kv, seg_kv, (F32, F32))
    kb_s, vb_s = _project(hs, g_kv, wkv, gains_kv, seg_kv, (F32, F32))
    kb_p16 = kb_p16.reshape(batch, seq, KV_WIDTH)
    kb_s3 = kb_s.reshape(dec_batch, t_new, KV_WIDTH)
    vb_s3 = vb_s.reshape(dec_batch, t_new, KV_WIDTH)
    cache_kb = cache_k_b.reshape(dec_batch, len_b, KV_WIDTH)
    cache_vb = cache_v_b.reshape(dec_batch, len_b, KV_WIDTH)

    pad_b = LEFT_CHUNKS_B * CHUNK
    tk_b = pad_b + CHUNK
    assert int(pos_s[0] - kpos_b[0]) == pad_b and t_new <= CHUNK and len_b + t_new <= tk_b
    assert np.all(np.diff(kpos_b) == 1)
    rel = _rel_bias(rel_bias_b.astype(F32) * LOG2E, CHUNK, tk_b, pad_b)
    rel = rel.transpose(0, 2, 1, 3)
    mask_s = np.where(_np_band_mask(pos_s, kpos_b, LEFT_CHUNKS_B), 0.0, NEG).astype(np.float32)

    seg_q = [(0, dq, True)]
    for j in range(n_layers_b):
        layer = n_layers_a + j
        wq = _group_major_cols(w_q_b[j]).astype(BF16)
        gains = jnp.tile(g_q_b[j] * q_scale, N_HEADS)
        wo = _group_major_rows(w_o_b[j]).astype(BF16)
        wup = w_up[layer].astype(BF16)
        wdn = w_down[layer].astype(BF16)
        bias_p = _pair_bias(rel[j])
        bias_s = (rel[j][:, :t_new, :len_b + t_new] + mask_s[None]).reshape(N_HEADS * t_new, -1)

        (qtp,) = _project(hp, g_attn[layer], None, None, [], (), wt=wq.T, gains_t=gains, qt_rows=dq)
        ap = _prompt_attention(qtp.reshape(batch, seq // LANES, dq, LANES), kb_p16, vtb_p, bias_p, None,
                               LEFT_CHUNKS_B)
        (qs,) = _project(hs, g_attn[layer], wq, gains, seg_q, (BF16,))
        as_ = _sample_attention(qs.reshape(dec_batch, t_new, dq), cache_kb, kb_s3, cache_vb, vb_s3, bias_s, None)

        hp = _out_mlp(hp, ap.reshape(batch * seq, dq), wo, g_mlp[layer], wup, wdn)
        hs = _out_mlp(hs, as_.reshape(dec_batch * t_new, dq), wo, g_mlp[layer], wup, wdn)

    return (hp.reshape(batch, seq, d), hs.reshape(dec_batch, t_new, d),
            jnp.stack(ka_p), jnp.stack(va_p),
            kv4(kb_p, batch, keep_b), kv4(vb_p, batch, keep_b),
            jnp.stack(ka_s), jnp.stack(va_s),
            kv4(kb_s3, dec_batch, t_new), kv4(vb_s3, dec_batch, t_new))
```

```python
import functools
import math

import numpy as np
import jax
import jax.numpy as jnp
from jax import lax
from jax.experimental import pallas as pl
from jax.experimental.pallas import tpu as pltpu

CHUNK = 64
HEAD_DIM = 64
N_KV_HEADS = 4
GROUP = 4
N_HEADS = N_KV_HEADS * GROUP
LEFT_CHUNKS_A = 2
LEFT_CHUNKS_B = 8
MAX_REL = 128
N_REL = 2 * MAX_REL + 1
PAST_LEN = 1024
EPS = 1e-6
NEG = -1e30
LOG2E = math.log2(math.e)

LANES = 128
KV_WIDTH = N_KV_HEADS * HEAD_DIM
GROUP_WIDTH = GROUP * HEAD_DIM
PAIR = 2 * CHUNK
ONES_ROWS = 16
REL_BIAS_ROWS = 8
ROW_TILE = 2048
MLP_TILE = 1024
MLP_FF_CHUNKS = 8
ATTN_TILE = 2048
SHORT_BAND_LOOKAHEAD = 2
LONG_BAND_LOOKAHEAD = 1
PAIR_UNROLL = 4
VMEM_LIMIT = 56 * 1024 * 1024

BF16 = jnp.bfloat16
F32 = jnp.float32


def _const_spec(shape):
    zeros = (0,) * len(shape)
    return pl.BlockSpec(shape, lambda *_: zeros, pipeline_mode=pl.Buffered(1))


def _head_sum_matrix():
    idx = np.arange(GROUP_WIDTH) // HEAD_DIM
    return jnp.asarray(idx[:, None] == idx[None, :], dtype=BF16)


def _group_major_cols(w):
    rows = w.shape[0]
    return w.reshape(rows, N_KV_HEADS, GROUP, HEAD_DIM).transpose(0, 2, 1, 3).reshape(rows, N_HEADS * HEAD_DIM)


def _group_major_rows(w):
    cols = w.shape[1]
    return w.reshape(N_KV_HEADS, GROUP, HEAD_DIM, cols).transpose(1, 0, 2, 3).reshape(N_HEADS * HEAD_DIM, cols)


def _rms_rows(x):
    return x * lax.rsqrt(jnp.mean(x * x, axis=-1, keepdims=True) + EPS)


def _proj_kernel(*refs, segments, qt_rows, vt_rows):
    refs = list(refs)
    x_ref, g_ref, hsum_ref = refs[:3]
    del refs[:3]
    if segments:
        w_ref, gain_ref = refs[:2]
        del refs[:2]
    if qt_rows or vt_rows:
        wt_ref = refs.pop(0)
    if qt_rows:
        gain_t_ref = refs.pop(0)
    out_refs = refs
    xn = (_rms_rows(x_ref[...]) * g_ref[...]).astype(BF16)
    if segments:
        if w_ref.shape[1] == GROUP_WIDTH:
            half = xn.shape[0] // 2
            y = jnp.concatenate([jnp.dot(xn[:half], w_ref[...], preferred_element_type=F32),
                                 jnp.dot(xn[half:], w_ref[...], preferred_element_type=F32)], axis=0)
        else:
            y = jnp.dot(xn, w_ref[...], preferred_element_type=F32)
        for (start, width, normed), o_ref in zip(segments, out_refs):
            for t in range(width // GROUP_WIDTH):
                lo = start + t * GROUP_WIDTH
                yt = y[:, lo:lo + GROUP_WIDTH]
                if normed:
                    ss = jnp.dot((yt * yt).astype(BF16), hsum_ref[...], preferred_element_type=F32)
                    yt = yt * lax.rsqrt(ss * (1.0 / HEAD_DIM) + EPS) * gain_ref[:, lo:lo + GROUP_WIDTH]
                o_ref[:, t * GROUP_WIDTH:(t + 1) * GROUP_WIDTH] = yt.astype(o_ref.dtype)
    if qt_rows or vt_rows:
        yt = lax.dot_general(wt_ref[...], xn, (((1,), (1,)), ((), ())), preferred_element_type=F32)
        if qt_rows:
            qt_ref = out_refs[len(segments)]
            n_tok = yt.shape[1]
            for grp in range(qt_rows // GROUP_WIDTH):
                r0 = grp * GROUP_WIDTH
                blk = yt[r0:r0 + GROUP_WIDTH, :].reshape(GROUP, HEAD_DIM, n_tok)
                scale = lax.rsqrt(jnp.mean(blk * blk, axis=1, keepdims=True) + EPS)
                gain = gain_t_ref[r0:r0 + GROUP_WIDTH, :].reshape(GROUP, HEAD_DIM, LANES)
                for lb in range(n_tok // LANES):
                    piece = blk[:, :, lb * LANES:(lb + 1) * LANES] * scale[:, :, lb * LANES:(lb + 1) * LANES] * gain
                    qt_ref[lb, r0:r0 + GROUP_WIDTH, :] = piece.reshape(GROUP_WIDTH, LANES).astype(qt_ref.dtype)
        if vt_rows:
            out_refs[-1][...] = yt[qt_rows:qt_rows + vt_rows, :].astype(out_refs[-1].dtype)


def _project(x, g, w, gains, segments, out_dtypes, wt=None, gains_t=None, qt_rows=0, seq=None):
    n, d = x.shape
    tile = min(ROW_TILE, n)
    assert n % tile == 0
    out_shape = [jax.ShapeDtypeStruct((n, width), dt) for (_, width, _), dt in zip(segments, out_dtypes)]
    out_specs = [pl.BlockSpec((tile, width), lambda i: (i, 0)) for (_, width, _) in segments]
    in_specs = [
        pl.BlockSpec((tile, d), lambda i: (i, 0)),
        _const_spec((1, d)),
        _const_spec((GROUP_WIDTH, GROUP_WIDTH)),
    ]
    args = [x, g.reshape(1, d), _head_sum_matrix()]
    if segments:
        c = w.shape[1]
        in_specs += [_const_spec((d, c)), _const_spec((1, c))]
        args += [w, gains.reshape(1, c)]
    vt_rows = 0
    if wt is not None:
        vt_rows = wt.shape[0] - qt_rows
        in_specs.append(_const_spec(wt.shape))
        args.append(wt)
    if qt_rows:
        assert tile % LANES == 0
        in_specs.append(_const_spec((qt_rows, LANES)))
        args.append(jnp.broadcast_to(gains_t.astype(F32)[:, None], (qt_rows, LANES)))
        out_shape.append(jax.ShapeDtypeStruct((n // LANES, qt_rows, LANES), BF16))
        out_specs.append(pl.BlockSpec((tile // LANES, qt_rows, LANES), lambda i: (i, 0, 0)))
    if vt_rows:
        assert seq % tile == 0 and n % seq == 0
        per_seq = seq // tile
        out_shape.append(jax.ShapeDtypeStruct((n // seq, vt_rows, seq), BF16))
        out_specs.append(pl.BlockSpec((None, vt_rows, tile), lambda i: (i // per_seq, 0, i % per_seq)))
    return pl.pallas_call(
        functools.partial(_proj_kernel, segments=tuple(segments), qt_rows=qt_rows, vt_rows=vt_rows),
        grid=(n // tile,),
        in_specs=in_specs,
        out_specs=out_specs,
        out_shape=out_shape,
        compiler_params=pltpu.CompilerParams(
            dimension_semantics=("arbitrary",), vmem_limit_bytes=VMEM_LIMIT),
        name="project",
    )(*args)


def _band_attn_kernel(*refs, prev, n_pairs, has_sink, lookahead):
    if has_sink:
        q_ref, kp_ref, kc_ref, vtp_ref, vtc_ref, bias_ref, sink_ref, o_ref, kcat, vt3, *s_bufs, ot_prev = refs
    else:
        q_ref, kp_ref, kc_ref, vtp_ref, vtc_ref, bias_ref, o_ref, kcat, vt3, *s_bufs, ot_prev = refs
        sink_ref = None
    assert len(s_bufs) == N_KV_HEADS
    tile = n_pairs * PAIR
    band = prev + PAIR
    n_cols = 2 * GROUP * CHUNK

    kcat[0:prev, :] = kp_ref[...]
    kcat[prev:prev + tile, :] = kc_ref[...]
    ones = jnp.ones((ONES_ROWS, LANES), BF16)
    blocks = [vtp_ref[:, b * LANES:(b + 1) * LANES] for b in range(prev // LANES)]
    blocks += [vtc_ref[:, b * LANES:(b + 1) * LANES] for b in range(tile // LANES)]
    for b, blk in enumerate(blocks):
        for h in range(N_KV_HEADS):
            vt3[b, h, 0:HEAD_DIM, :] = blk[h * HEAD_DIM:(h + 1) * HEAD_DIM, :]
            vt3[b, h, HEAD_DIM:HEAD_DIM + ONES_ROWS, :] = ones

    def scores(j, h, masked, s_buf):
        r0 = pl.multiple_of(j * PAIR, PAIR)
        kb = kcat[pl.ds(r0, band), :]
        cols = []
        for g in range(GROUP):
            qt = q_ref[j, g * GROUP_WIDTH + h * HEAD_DIM:g * GROUP_WIDTH + (h + 1) * HEAD_DIM, :]
            parts = []
            if h > 0:
                parts.append(jnp.zeros((h * HEAD_DIM, PAIR), BF16))
            parts.append(qt)
            if h + 1 < N_KV_HEADS:
                parts.append(jnp.zeros(((N_KV_HEADS - 1 - h) * HEAD_DIM, PAIR), BF16))
            cols.append(jnp.concatenate(parts, axis=0))
        qs = jnp.concatenate(cols, axis=1)
        s = jnp.dot(kb, qs, preferred_element_type=F32)
        s = s + bias_ref[h]
        if masked:
            valid = lax.broadcasted_iota(jnp.int32, (band, n_cols), 0) >= prev - r0
            s = jnp.where(valid, s, NEG)
        s_buf[...] = s
        return jnp.max(s, axis=0, keepdims=True)

    def softmax_pv(m, s_buf, j, h):
        if has_sink:
            m = jnp.maximum(m, sink_ref[h])
        ot = jnp.zeros((HEAD_DIM + ONES_ROWS, n_cols), F32)
        n_blocks = band // LANES
        for b in range(0, n_blocks, 2):
            width = min(2, n_blocks - b) * LANES
            eb = jnp.exp2(s_buf[b * LANES:b * LANES + width, :] - m).astype(BF16)
            vt = jnp.concatenate([vt3[j + b + i, h] for i in range(width // LANES)], axis=1)
            ot = ot + jnp.dot(vt, eb, preferred_element_type=F32)
        denom = ot[HEAD_DIM:HEAD_DIM + 1, :]
        if has_sink:
            denom = denom + jnp.exp2(sink_ref[h] - m)
        return ot[0:HEAD_DIM, :] * (1.0 / denom)

    def run(masked):
        m_first = tuple(scores(0, h, masked, s_bufs[h]) for h in range(lookahead))
        ot_prev[...] = jnp.zeros(ot_prev.shape, F32)

        def write_out(j):
            for g in range(GROUP):
                o_ref[j, g * KV_WIDTH:(g + 1) * KV_WIDTH, :] = (
                    ot_prev[:, g * PAIR:(g + 1) * PAIR].astype(o_ref.dtype))

        def body(j, ms):
            write_out(jnp.maximum(j - 1, 0))
            ms = list(ms)
            for h in range(N_KV_HEADS):
                h_ahead = h + lookahead
                if h_ahead < N_KV_HEADS:
                    ms.append(scores(j, h_ahead, masked, s_bufs[h_ahead]))
                else:
                    h_ahead -= N_KV_HEADS
                    ms.append(scores(jnp.minimum(j + 1, n_pairs - 1), h_ahead, masked, s_bufs[h_ahead]))
                ot_prev[h * HEAD_DIM:(h + 1) * HEAD_DIM, :] = softmax_pv(ms.pop(0), s_bufs[h], j, h)
            return tuple(ms)

        lax.fori_loop(0, n_pairs, body, m_first, unroll=PAIR_UNROLL)
        write_out(n_pairs - 1)

    @pl.when(pl.program_id(1) == 0)
    def _():
        run(True)

    @pl.when(pl.program_id(1) > 0)
    def _():
        run(False)


def _prompt_attention(q, k, vt, bias, sink, left_chunks):
    b, n_blocks, dq, _ = q.shape
    s = n_blocks * LANES
    prev = left_chunks * CHUNK
    tile = min(ATTN_TILE, s)
    band = prev + PAIR
    assert s % tile == 0 and tile % prev == 0 and prev % LANES == 0 and tile % PAIR == 0
    assert bias.shape == (N_KV_HEADS, band, 2 * GROUP * CHUNK)
    ratio = tile // prev
    has_sink = sink is not None
    cur_rows = lambda bi, i: (bi, i, 0)
    prev_rows = lambda bi, i: (bi, jnp.maximum(i * ratio - 1, 0), 0)
    cur_cols = lambda bi, i: (bi, 0, i)
    prev_cols = lambda bi, i: (bi, 0, jnp.maximum(i * ratio - 1, 0))
    in_specs = [
        pl.BlockSpec((None, tile // LANES, dq, LANES), lambda bi, i: (bi, i, 0, 0)),
        pl.BlockSpec((None, prev, KV_WIDTH), prev_rows),
        pl.BlockSpec((None, tile, KV_WIDTH), cur_rows),
        pl.BlockSpec((None, KV_WIDTH, prev), prev_cols),
        pl.BlockSpec((None, KV_WIDTH, tile), cur_cols),
        _const_spec(bias.shape),
    ]
    args = [q, k, k, vt, vt, bias]
    if has_sink:
        in_specs.append(_const_spec(sink.shape))
        args.append(sink)
    return pl.pallas_call(
        functools.partial(_band_attn_kernel, prev=prev, n_pairs=tile // PAIR, has_sink=has_sink,
                          lookahead=SHORT_BAND_LOOKAHEAD if band <= 2 * LANES else LONG_BAND_LOOKAHEAD),
        grid=(b, s // tile),
        in_specs=in_specs,
        out_specs=pl.BlockSpec((None, tile // LANES, dq, LANES), lambda bi, i: (bi, i, 0, 0)),
        out_shape=jax.ShapeDtypeStruct((b, s // LANES, dq, LANES), BF16),
        scratch_shapes=[
            pltpu.VMEM((prev + tile, KV_WIDTH), BF16),
            pltpu.VMEM(((prev + tile) // LANES, N_KV_HEADS, HEAD_DIM + ONES_ROWS, LANES), BF16),
            *[pltpu.VMEM((band, 2 * GROUP * CHUNK), F32) for _ in range(N_KV_HEADS)],
            pltpu.VMEM((N_KV_HEADS * HEAD_DIM, 2 * GROUP * CHUNK), F32),
        ],
        compiler_params=pltpu.CompilerParams(
            dimension_semantics=("arbitrary", "arbitrary"), vmem_limit_bytes=VMEM_LIMIT),
        name="band_attention",
    )(*args)


def _pair_bias(per_head):
    h, tq, tk = per_head.shape
    neg = jnp.full((h, tq, CHUNK), NEG, F32)
    both = jnp.stack([jnp.concatenate([per_head, neg], axis=-1),
                      jnp.concatenate([neg, per_head], axis=-1)], axis=1)
    both = both.reshape(N_KV_HEADS, GROUP, 2, tq, tk + CHUNK)
    return both.transpose(0, 4, 1, 2, 3).reshape(N_KV_HEADS, tk + CHUNK, 2 * GROUP * tq)


def _pair_sink(sink):
    s = (sink.astype(F32) * LOG2E).reshape(N_KV_HEADS, GROUP, 1, 1)
    return jnp.broadcast_to(s, (N_KV_HEADS, GROUP, 2, CHUNK)).reshape(N_KV_HEADS, 1, 2 * GROUP * CHUNK)


def _sample_attn_kernel(*refs, tq, has_sink):
    if has_sink:
        q_ref, kp_ref, kc_ref, vp_ref, vc_ref, bias_ref, sink_ref, o_ref = refs
    else:
        q_ref, kp_ref, kc_ref, vp_ref, vc_ref, bias_ref, o_ref = refs
        sink_ref = None
    kcat = jnp.concatenate([kp_ref[...], kc_ref[...]], axis=0).astype(BF16)
    vcat = jnp.concatenate([vp_ref[...], vc_ref[...]], axis=0).astype(BF16)
    lane_head = lax.broadcasted_iota(jnp.int32, (tq, KV_WIDTH), 1) // HEAD_DIM
    qc = q_ref[...]
    qs = jnp.concatenate(
        [jnp.where(lane_head == h, qc[:, g * KV_WIDTH:(g + 1) * KV_WIDTH], 0)
         for h in range(N_KV_HEADS) for g in range(GROUP)], axis=0).astype(BF16)
    s = lax.dot_general(qs, kcat, (((1,), (1,)), ((), ())), preferred_element_type=F32)
    s = s + bias_ref[...]
    m = jnp.max(s, axis=-1, keepdims=True)
    if has_sink:
        m = jnp.maximum(m, sink_ref[...])
    e = jnp.exp2(s - m)
    denom = jnp.sum(e, axis=-1, keepdims=True)
    if has_sink:
        denom = denom + jnp.exp2(sink_ref[...] - m)
    p = (e * (1.0 / denom)).astype(BF16)
    o_all = jnp.dot(p, vcat, preferred_element_type=F32)
    for g in range(GROUP):
        o = jnp.zeros((tq, KV_WIDTH), F32)
        for h in range(N_KV_HEADS):
            r0 = (h * GROUP + g) * tq
            o = jnp.where(lane_head == h, o_all[r0:r0 + tq, :], o)
        o_ref[:, g * KV_WIDTH:(g + 1) * KV_WIDTH] = o.astype(o_ref.dtype)


def _sample_attention(q, k_cache, k_new, v_cache, v_new, bias, sink):
    b, t, dq = q.shape
    prev = k_cache.shape[1]
    assert bias.shape == (N_HEADS * t, prev + t)
    has_sink = sink is not None
    row_map = lambda bi: (bi, 0, 0)
    in_specs = [
        pl.BlockSpec((None, t, dq), row_map),
        pl.BlockSpec((None, prev, KV_WIDTH), row_map),
        pl.BlockSpec((None, t, KV_WIDTH), row_map),
        pl.BlockSpec((None, prev, KV_WIDTH), row_map),
        pl.BlockSpec((None, t, KV_WIDTH), row_map),
        _const_spec(bias.shape),
    ]
    args = [q, k_cache, k_new, v_cache, v_new, bias]
    if has_sink:
        in_specs.append(_const_spec(sink.shape))
        args.append(sink)
    return pl.pallas_call(
        functools.partial(_sample_attn_kernel, tq=t, has_sink=has_sink),
        grid=(b,),
        in_specs=in_specs,
        out_specs=pl.BlockSpec((None, t, dq), row_map),
        out_shape=jax.ShapeDtypeStruct((b, t, dq), BF16),
        compiler_params=pltpu.CompilerParams(
            dimension_semantics=("arbitrary",), vmem_limit_bytes=VMEM_LIMIT),
        name="sample_attention",
    )(*args)


def _out_mlp_kernel(x_ref, a_ref, wo_ref, g_ref, wup_ref, wdn_ref, o_ref, *, ff_chunks, a_transposed):
    if a_transposed:
        at = jnp.concatenate([a_ref[b] for b in range(a_ref.shape[0])], axis=1)
        proj = lax.dot_general(at, wo_ref[...], (((0,), (0,)), ((), ())), preferred_element_type=F32)
    else:
        proj = jnp.dot(a_ref[...], wo_ref[...], preferred_element_type=F32)
    h = x_ref[...] + proj
    hn = (_rms_rows(h) * g_ref[...]).astype(BF16)
    d_ff = wup_ref.shape[1]
    step = d_ff // ff_chunks
    acc = h
    for c in range(ff_chunks):
        u = jnp.dot(hn, wup_ref[:, c * step:(c + 1) * step], preferred_element_type=F32)
        u = jnp.square(jnp.maximum(u, 0.0)).astype(BF16)
        acc = acc + jnp.dot(u, wdn_ref[c * step:(c + 1) * step, :], preferred_element_type=F32)
    o_ref[...] = acc


def _out_mlp(x, a, wo, g, wup, wdn):
    n, d = x.shape
    d_ff = wup.shape[1]
    tile = min(MLP_TILE, n)
    assert n % tile == 0
    row_spec = pl.BlockSpec((tile, d), lambda i: (i, 0))
    a_transposed = a.ndim == 3
    a_spec = pl.BlockSpec((tile // LANES, d, LANES), lambda i: (i, 0, 0)) if a_transposed else row_spec
    return pl.pallas_call(
        functools.partial(_out_mlp_kernel, ff_chunks=MLP_FF_CHUNKS, a_transposed=a_transposed),
        grid=(n // tile,),
        in_specs=[
            row_spec,
            a_spec,
            _const_spec((d, d)),
            _const_spec((1, d)),
            _const_spec((d, d_ff)),
            _const_spec((d_ff, d)),
        ],
        out_specs=row_spec,
        out_shape=jax.ShapeDtypeStruct((n, d), F32),
        compiler_params=pltpu.CompilerParams(
            dimension_semantics=("arbitrary",), vmem_limit_bytes=VMEM_LIMIT),
        name="out_proj_mlp",
    )(x, a, wo, g.reshape(1, d), wup, wdn)


def _rel_bias_kernel(tab_ref, o_ref, *, tk, delta):
    n_pad = tab_ref.shape[-1]
    r = lax.broadcasted_iota(jnp.int32, (n_pad, tk), 0)
    j = lax.broadcasted_iota(jnp.int32, (n_pad, tk), 1)
    for qi in range(o_ref.shape[0]):
        q = pl.program_id(1) * o_ref.shape[0] + qi
        idx = jnp.clip(q + delta - j, -MAX_REL, MAX_REL) + MAX_REL
        onehot = jnp.where(r == idx, 1.0, 0.0).astype(BF16)
        acc = jnp.dot(tab_ref[0], onehot, preferred_element_type=F32)
        acc = acc + jnp.dot(tab_ref[1], onehot, preferred_element_type=F32)
        acc = acc + jnp.dot(tab_ref[2], onehot, preferred_element_type=F32)
        o_ref[qi] = acc


def _rel_bias(table, tq, tk, delta):
    n_layers, h, n_rel = table.shape
    n_pad = -(-n_rel // LANES) * LANES
    t = jnp.pad(table, ((0, 0), (0, 0), (0, n_pad - n_rel)))
    hi = t.astype(BF16)
    mid = (t - hi.astype(F32)).astype(BF16)
    lo = (t - hi.astype(F32) - mid.astype(F32)).astype(BF16)
    parts = jnp.stack([hi, mid, lo], axis=1)
    return pl.pallas_call(
        functools.partial(_rel_bias_kernel, tk=tk, delta=delta),
        grid=(n_layers, tq // REL_BIAS_ROWS),
        in_specs=[pl.BlockSpec((None, 3, h, n_pad), lambda l, q: (l, 0, 0, 0))],
        out_specs=pl.BlockSpec((None, REL_BIAS_ROWS, h, tk), lambda l, q: (l, q, 0, 0)),
        out_shape=jax.ShapeDtypeStruct((n_layers, tq, h, tk), F32),
        compiler_params=pltpu.CompilerParams(dimension_semantics=("arbitrary", "arbitrary")),
        name="rel_bias_table",
    )(parts)


def _np_band_mask(q_pos, k_pos, left_chunks):
    qc = q_pos[:, None] // CHUNK
    kc = k_pos[None, :] // CHUNK
    return (k_pos[None, :] >= 0) & (kc <= qc) & (kc >= qc - left_chunks)


def _alibi_bias(q_pos, k_pos, static_mask):
    slopes = (2.0 ** (-8.0 * np.arange(1, N_HEADS + 1, dtype=np.float32) / N_HEADS)).astype(np.float32)
    dist = np.abs(q_pos[:, None] - k_pos[None, :]).astype(np.float32)
    bias = -slopes[:, None, None] * dist[None] * np.float32(LOG2E)
    if static_mask:
        bias = np.where(_np_band_mask(q_pos, k_pos, LEFT_CHUNKS_A)[None], bias, np.float32(NEG))
    return jnp.asarray(bias.astype(np.float32))


def _row_sink(sink, tq):
    return jnp.repeat(sink.astype(F32) * LOG2E, tq).reshape(N_HEADS * tq, 1)


def kernel(x_prompt, x_sample, cache_k_a, cache_v_a, cache_k_b, cache_v_b, g_attn, g_mlp, w_qkv_a,
           g_q_a, g_k_a, sink_a, w_o_a, g_kv, w_kv, g_k_b, w_q_b, g_q_b, rel_bias_b, w_o_b, w_up, w_down):
    batch, seq, d = x_prompt.shape
    dec_batch, t_new, _ = x_sample.shape
    n_layers_a = w_qkv_a.shape[0]
    n_layers_b = w_q_b.shape[0]
    len_a = cache_k_a.shape[2]
    len_b = cache_k_b.shape[1]
    keep_a = min(LEFT_CHUNKS_A * CHUNK, seq)
    keep_b = min(LEFT_CHUNKS_B * CHUNK, seq)
    dq = N_HEADS * HEAD_DIM
    q_scale = HEAD_DIM ** -0.5 * LOG2E

    hp = x_prompt.reshape(batch * seq, d)
    hs = x_sample.reshape(dec_batch * t_new, d)

    pos_s = PAST_LEN + np.arange(t_new)
    kpos_a = np.concatenate([PAST_LEN - len_a + np.arange(len_a), pos_s])
    kpos_b = np.concatenate([PAST_LEN - len_b + np.arange(len_b), pos_s])
    chunk_q = np.arange(CHUNK)
    bias_a_prompt = _pair_bias(_alibi_bias(chunk_q, np.arange(-LEFT_CHUNKS_A * CHUNK, CHUNK), static_mask=False))
    bias_a_sample = _alibi_bias(pos_s, kpos_a, static_mask=True).reshape(N_HEADS * t_new, -1)

    ones_kv = jnp.ones((KV_WIDTH,), F32)
    seg_a_p = [(0, dq, True), (dq, KV_WIDTH, True)]
    seg_a_s = seg_a_p + [(dq + KV_WIDTH, KV_WIDTH, False)]
    seg_kv = [(0, KV_WIDTH, True), (KV_WIDTH, KV_WIDTH, False)]
    last_rows = lambda h, keep: h.reshape(batch, seq, d)[:, seq - keep:].reshape(batch * keep, d)
    kv4 = lambda a, n, t: a.reshape(n, t, N_KV_HEADS, HEAD_DIM)

    ka_p, va_p, ka_s, va_s = [], [], [], []
    for i in range(n_layers_a):
        wq_gm = _group_major_cols(w_qkv_a[i][:, :dq])
        w = jnp.concatenate([wq_gm, w_qkv_a[i][:, dq:]], axis=1).astype(BF16)
        wqv_t = jnp.concatenate([wq_gm, w_qkv_a[i][:, dq + KV_WIDTH:]], axis=1).T.astype(BF16)
        gains = jnp.concatenate([jnp.tile(g_q_a[i] * q_scale, N_HEADS), jnp.tile(g_k_a[i], N_KV_HEADS), ones_kv])
        wo = _group_major_rows(w_o_a[i]).astype(BF16)
        wup = w_up[i].astype(BF16)
        wdn = w_down[i].astype(BF16)

        kp16, qtp, vtp = _project(hp, g_attn[i], w[:, dq:dq + KV_WIDTH], gains[dq:dq + KV_WIDTH], seg_kv[:1],
                                  (BF16,), wt=wqv_t, gains_t=gains[:dq], qt_rows=dq, seq=seq)
        ap = _prompt_attention(qtp.reshape(batch, seq // LANES, dq, LANES), kp16.reshape(batch, seq, KV_WIDTH),
                               vtp, bias_a_prompt, _pair_sink(sink_a[i]), LEFT_CHUNKS_A)
        kp, vp = _project(last_rows(hp, keep_a), g_attn[i], w[:, dq:], gains[dq:], seg_kv, (F32, F32))
        ka_p.append(kv4(kp, batch, keep_a))
        va_p.append(kv4(vp, batch, keep_a))

        qs, ks, vs = _project(hs, g_attn[i], w, gains, seg_a_s, (BF16, F32, F32))
        ks3 = ks.reshape(dec_batch, t_new, KV_WIDTH)
        vs3 = vs.reshape(dec_batch, t_new, KV_WIDTH)
        as_ = _sample_attention(qs.reshape(dec_batch, t_new, dq),
                                cache_k_a[i].reshape(dec_batch, len_a, KV_WIDTH), ks3,
                                cache_v_a[i].reshape(dec_batch, len_a, KV_WIDTH), vs3,
                                bias_a_sample, _row_sink(sink_a[i], t_new))
        ka_s.append(ks3.reshape(dec_batch, t_new, N_KV_HEADS, HEAD_DIM))
        va_s.append(vs3.reshape(dec_batch, t_new, N_KV_HEADS, HEAD_DIM))

        hp = _out_mlp(hp, ap.reshape(batch * seq // LANES, dq, LANES), wo, g_mlp[i], wup, wdn)
        hs = _out_mlp(hs, as_.reshape(dec_batch * t_new, dq), wo, g_mlp[i], wup, wdn)

    wkv = w_kv.astype(BF16)
    gains_kv = jnp.concatenate([jnp.tile(g_k_b, N_KV_HEADS), ones_kv])
    kb_p16, vtb_p = _project(hp, g_kv, wkv[:, :KV_WIDTH], gains_kv[:KV_WIDTH], seg_kv[:1], (BF16,),
                             wt=wkv[:, KV_WIDTH:].T, seq=seq)
    kb_p, vb_p = _project(last_rows(hp, keep_b), g_kv, wkv, gains_kv, seg_kv, (F32, F32))
    kb_s, vb_s = _project(hs, g_kv, wkv, gains_kv, seg_kv, (F32, F32))
    kb_p16 = kb_p16.reshape(batch, seq, KV_WIDTH)
    kb_s3 = kb_s.reshape(dec_batch, t_new, KV_WIDTH)
    vb_s3 = vb_s.reshape(dec_batch, t_new, KV_WIDTH)
    cache_kb = cache_k_b.reshape(dec_batch, len_b, KV_WIDTH)
    cache_vb = cache_v_b.reshape(dec_batch, len_b, KV_WIDTH)

    pad_b = LEFT_CHUNKS_B * CHUNK
    tk_b = pad_b + CHUNK
    assert int(pos_s[0] - kpos_b[0]) == pad_b and t_new <= CHUNK and len_b + t_new <= tk_b
    assert np.all(np.diff(kpos_b) == 1)
    rel = _rel_bias(rel_bias_b.astype(F32) * LOG2E, CHUNK, tk_b, pad_b)
    rel = rel.transpose(0, 2, 1, 3)
    mask_s = np.where(_np_band_mask(pos_s, kpos_b, LEFT_CHUNKS_B), 0.0, NEG).astype(np.float32)

    seg_q = [(0, dq, True)]
    for j in range(n_layers_b):
        layer = n_layers_a + j
        wq = _group_major_cols(w_q_b[j]).astype(BF16)
        gains = jnp.tile(g_q_b[j] * q_scale, N_HEADS)
        wo = _group_major_rows(w_o_b[j]).astype(BF16)
        wup = w_up[layer].astype(BF16)
        wdn = w_down[layer].astype(BF16)
        bias_p = _pair_bias(rel[j])
        bias_s = (rel[j][:, :t_new, :len_b + t_new] + mask_s[None]).reshape(N_HEADS * t_new, -1)

        (qtp,) = _project(hp, g_attn[layer], None, None, [], (), wt=wq.T, gains_t=gains, qt_rows=dq)
        ap = _prompt_attention(qtp.reshape(batch, seq // LANES, dq, LANES), kb_p16, vtb_p, bias_p, None,
                               LEFT_CHUNKS_B)
        (qs,) = _project(hs, g_attn[layer], wq, gains, seg_q, (BF16,))
        as_ = _sample_attention(qs.reshape(dec_batch, t_new, dq), cache_kb, kb_s3, cache_vb, vb_s3, bias_s, None)

        hp = _out_mlp(hp, ap.reshape(batch * seq // LANES, dq, LANES), wo, g_mlp[layer], wup, wdn)
        hs = _out_mlp(hs, as_.reshape(dec_batch * t_new, dq), wo, g_mlp[layer], wup, wdn)

    return (hp.reshape(batch, seq, d), hs.reshape(dec_batch, t_new, d),
            jnp.stack(ka_p), jnp.stack(va_p),
            kv4(kb_p, batch, keep_b), kv4(vb_p, batch, keep_b),
            jnp.stack(ka_s), jnp.stack(va_s),
            kv4(kb_s3, dec_batch, t_new), kv4(vb_s3, dec_batch, t_new))
```

```python
import functools
import math

import numpy as np
import jax
import jax.numpy as jnp
from jax import lax
from jax.experimental import pallas as pl
from jax.experimental.pallas import tpu as pltpu

CHUNK = 64
HEAD_DIM = 64
N_KV_HEADS = 4
GROUP = 4
N_HEADS = N_KV_HEADS * GROUP
LEFT_CHUNKS_A = 2
LEFT_CHUNKS_B = 8
MAX_REL = 128
N_REL = 2 * MAX_REL + 1
PAST_LEN = 1024
EPS = 1e-6
NEG = -1e30
LOG2E = math.log2(math.e)

LANES = 128
KV_WIDTH = N_KV_HEADS * HEAD_DIM
GROUP_WIDTH = GROUP * HEAD_DIM
PAIR = 2 * CHUNK
ONES_ROWS = 16
REL_BIAS_ROWS = 8
ROW_TILE = 2048
MLP_TILE = 1024
MLP_FF_CHUNKS = 8
ATTN_TILE = 2048
SHORT_BAND_LOOKAHEAD = 2
LONG_BAND_LOOKAHEAD = 1
PAIR_UNROLL = 4
VMEM_LIMIT = 56 * 1024 * 1024

BF16 = jnp.bfloat16
F32 = jnp.float32


def _const_spec(shape):
    zeros = (0,) * len(shape)
    return pl.BlockSpec(shape, lambda *_: zeros, pipeline_mode=pl.Buffered(1))


def _head_sum_matrix():
    idx = np.arange(GROUP_WIDTH) // HEAD_DIM
    return jnp.asarray(idx[:, None] == idx[None, :], dtype=BF16)


def _group_major_cols(w):
    rows = w.shape[0]
    return w.reshape(rows, N_KV_HEADS, GROUP, HEAD_DIM).transpose(0, 2, 1, 3).reshape(rows, N_HEADS * HEAD_DIM)


def _group_major_rows(w):
    cols = w.shape[1]
    return w.reshape(N_KV_HEADS, GROUP, HEAD_DIM, cols).transpose(1, 0, 2, 3).reshape(N_HEADS * HEAD_DIM, cols)


def _rms_rows(x):
    return x * lax.rsqrt(jnp.mean(x * x, axis=-1, keepdims=True) + EPS)


def _proj_kernel(*refs, segments, qt_rows, vt_rows):
    refs = list(refs)
    x_ref, g_ref, hsum_ref = refs[:3]
    del refs[:3]
    if segments:
        w_ref, gain_ref = refs[:2]
        del refs[:2]
    if qt_rows or vt_rows:
        wt_ref = refs.pop(0)
    if qt_rows:
        gain_t_ref = refs.pop(0)
    out_refs = refs
    xn = (_rms_rows(x_ref[...]) * g_ref[...]).astype(BF16)
    if segments:
        if w_ref.shape[1] == GROUP_WIDTH:
            half = xn.shape[0] // 2
            y = jnp.concatenate([jnp.dot(xn[:half], w_ref[...], preferred_element_type=F32),
                                 jnp.dot(xn[half:], w_ref[...], preferred_element_type=F32)], axis=0)
        else:
            y = jnp.dot(xn, w_ref[...], preferred_element_type=F32)
        for (start, width, normed), o_ref in zip(segments, out_refs):
            for t in range(width // GROUP_WIDTH):
                lo = start + t * GROUP_WIDTH
                yt = y[:, lo:lo + GROUP_WIDTH]
                if normed:
                    ss = jnp.dot((yt * yt).astype(BF16), hsum_ref[...], preferred_element_type=F32)
                    yt = yt * lax.rsqrt(ss * (1.0 / HEAD_DIM) + EPS) * gain_ref[:, lo:lo + GROUP_WIDTH]
                o_ref[:, t * GROUP_WIDTH:(t + 1) * GROUP_WIDTH] = yt.astype(o_ref.dtype)
    if qt_rows or vt_rows:
        yt = lax.dot_general(wt_ref[...], xn, (((1,), (1,)), ((), ())), preferred_element_type=F32)
        if qt_rows:
            qt_ref = out_refs[len(segments)]
            n_tok = yt.shape[1]
            for grp in range(qt_rows // GROUP_WIDTH):
                r0 = grp * GROUP_WIDTH
                blk = yt[r0:r0 + GROUP_WIDTH, :].reshape(GROUP, HEAD_DIM, n_tok)
                scale = lax.rsqrt(jnp.mean(blk * blk, axis=1, keepdims=True) + EPS)
                gain = gain_t_ref[r0:r0 + GROUP_WIDTH, :].reshape(GROUP, HEAD_DIM, LANES)
                for lb in range(n_tok // LANES):
                    piece = blk[:, :, lb * LANES:(lb + 1) * LANES] * scale[:, :, lb * LANES:(lb + 1) * LANES] * gain
                    qt_ref[lb, r0:r0 + GROUP_WIDTH, :] = piece.reshape(GROUP_WIDTH, LANES).astype(qt_ref.dtype)
        if vt_rows:
            out_refs[-1][...] = yt[qt_rows:qt_rows + vt_rows, :].astype(out_refs[-1].dtype)


def _project(x, g, w, gains, segments, out_dtypes, wt=None, gains_t=None, qt_rows=0, seq=None):
    n, d = x.shape
    tile = min(ROW_TILE, n)
    assert n % tile == 0
    out_shape = [jax.ShapeDtypeStruct((n, width), dt) for (_, width, _), dt in zip(segments, out_dtypes)]
    out_specs = [pl.BlockSpec((tile, width), lambda i: (i, 0)) for (_, width, _) in segments]
    in_specs = [
        pl.BlockSpec((tile, d), lambda i: (i, 0)),
        _const_spec((1, d)),
        _const_spec((GROUP_WIDTH, GROUP_WIDTH)),
    ]
    args = [x, g.reshape(1, d), _head_sum_matrix()]
    if segments:
        c = w.shape[1]
        in_specs += [_const_spec((d, c)), _const_spec((1, c))]
        args += [w, gains.reshape(1, c)]
    vt_rows = 0
    if wt is not None:
        vt_rows = wt.shape[0] - qt_rows
        in_specs.append(_const_spec(wt.shape))
        args.append(wt)
    if qt_rows:
        assert tile % LANES == 0
        in_specs.append(_const_spec((qt_rows, LANES)))
        args.append(jnp.broadcast_to(gains_t.astype(F32)[:, None], (qt_rows, LANES)))
        out_shape.append(jax.ShapeDtypeStruct((n // LANES, qt_rows, LANES), BF16))
        out_specs.append(pl.BlockSpec((tile // LANES, qt_rows, LANES), lambda i: (i, 0, 0)))
    if vt_rows:
        assert seq % tile == 0 and n % seq == 0
        per_seq = seq // tile
        out_shape.append(jax.ShapeDtypeStruct((n // seq, vt_rows, seq), BF16))
        out_specs.append(pl.BlockSpec((None, vt_rows, tile), lambda i: (i // per_seq, 0, i % per_seq)))
    return pl.pallas_call(
        functools.partial(_proj_kernel, segments=tuple(segments), qt_rows=qt_rows, vt_rows=vt_rows),
        grid=(n // tile,),
        in_specs=in_specs,
        out_specs=out_specs,
        out_shape=out_shape,
        compiler_params=pltpu.CompilerParams(
            dimension_semantics=("arbitrary",), vmem_limit_bytes=VMEM_LIMIT),
        name="project",
    )(*args)


def _band_attn_kernel(*refs, prev, n_pairs, has_sink, lookahead):
    if has_sink:
        q_ref, kp_ref, kc_ref, vtp_ref, vtc_ref, bias_ref, sink_ref, o_ref, kcat, vt3, *s_bufs, ot_prev = refs
    else:
        q_ref, kp_ref, kc_ref, vtp_ref, vtc_ref, bias_ref, o_ref, kcat, vt3, *s_bufs, ot_prev = refs
        sink_ref = None
    assert len(s_bufs) == N_KV_HEADS
    tile = n_pairs * PAIR
    band = prev + PAIR
    n_cols = 2 * GROUP * CHUNK

    kcat[0:prev, :] = kp_ref[...]
    kcat[prev:prev + tile, :] = kc_ref[...]
    ones = jnp.ones((ONES_ROWS, LANES), BF16)
    blocks = [vtp_ref[:, b * LANES:(b + 1) * LANES] for b in range(prev // LANES)]
    blocks += [vtc_ref[:, b * LANES:(b + 1) * LANES] for b in range(tile // LANES)]
    for b, blk in enumerate(blocks):
        for h in range(N_KV_HEADS):
            vt3[b, h, 0:HEAD_DIM, :] = blk[h * HEAD_DIM:(h + 1) * HEAD_DIM, :]
            vt3[b, h, HEAD_DIM:HEAD_DIM + ONES_ROWS, :] = ones

    def scores(j, h, masked, s_buf):
        r0 = pl.multiple_of(j * PAIR, PAIR)
        kb = kcat[pl.ds(r0, band), :]
        cols = []
        for g in range(GROUP):
            qt = q_ref[j, g * GROUP_WIDTH + h * HEAD_DIM:g * GROUP_WIDTH + (h + 1) * HEAD_DIM, :]
            parts = []
            if h > 0:
                parts.append(jnp.zeros((h * HEAD_DIM, PAIR), BF16))
            parts.append(qt)
            if h + 1 < N_KV_HEADS:
                parts.append(jnp.zeros(((N_KV_HEADS - 1 - h) * HEAD_DIM, PAIR), BF16))
            cols.append(jnp.concatenate(parts, axis=0))
        qs = jnp.concatenate(cols, axis=1)
        s = jnp.dot(kb, qs, preferred_element_type=F32)
        s = s + bias_ref[h]
        if masked:
            valid = lax.broadcasted_iota(jnp.int32, (band, n_cols), 0) >= prev - r0
            s = jnp.where(valid, s, NEG)
        s_buf[...] = s
        return jnp.max(s, axis=0, keepdims=True)

    def softmax_pv(m, s_buf, j, h):
        if has_sink:
            m = jnp.maximum(m, sink_ref[h])
        ot = jnp.zeros((HEAD_DIM + ONES_ROWS, n_cols), F32)
        n_blocks = band // LANES
        for b in range(0, n_blocks, 2):
            width = min(2, n_blocks - b) * LANES
            eb = jnp.exp2(s_buf[b * LANES:b * LANES + width, :] - m).astype(BF16)
            vt = jnp.concatenate([vt3[j + b + i, h] for i in range(width // LANES)], axis=1)
            ot = ot + jnp.dot(vt, eb, preferred_element_type=F32)
        denom = ot[HEAD_DIM:HEAD_DIM + 1, :]
        if has_sink:
            denom = denom + jnp.exp2(sink_ref[h] - m)
        return ot[0:HEAD_DIM, :] * (1.0 / denom)

    def run(masked):
        m_first = tuple(scores(0, h, masked, s_bufs[h]) for h in range(lookahead))
        ot_prev[...] = jnp.zeros(ot_prev.shape, F32)

        def write_out(j):
            for g in range(GROUP):
                o_ref[j, g * KV_WIDTH:(g + 1) * KV_WIDTH, :] = (
                    ot_prev[:, g * PAIR:(g + 1) * PAIR].astype(o_ref.dtype))

        def body(j, ms):
            write_out(jnp.maximum(j - 1, 0))
            ms = list(ms)
            for h in range(N_KV_HEADS):
                h_ahead = h + lookahead
                if h_ahead < N_KV_HEADS:
                    ms.append(scores(j, h_ahead, masked, s_bufs[h_ahead]))
                else:
                    h_ahead -= N_KV_HEADS
                    ms.append(scores(jnp.minimum(j + 1, n_pairs - 1), h_ahead, masked, s_bufs[h_ahead]))
                ot_prev[h * HEAD_DIM:(h + 1) * HEAD_DIM, :] = softmax_pv(ms.pop(0), s_bufs[h], j, h)
            return tuple(ms)

        lax.fori_loop(0, n_pairs, body, m_first, unroll=PAIR_UNROLL)
        write_out(n_pairs - 1)

    @pl.when(pl.program_id(1) == 0)
    def _():
        run(True)

    @pl.when(pl.program_id(1) > 0)
    def _():
        run(False)


def _prompt_attention(q, k, vt, bias, sink, left_chunks):
    b, n_blocks, dq, _ = q.shape
    s = n_blocks * LANES
    prev = left_chunks * CHUNK
    tile = min(ATTN_TILE, s)
    band = prev + PAIR
    assert s % tile == 0 and tile % prev == 0 and prev % LANES == 0 and tile % PAIR == 0
    assert bias.shape == (N_KV_HEADS, band, 2 * GROUP * CHUNK)
    ratio = tile // prev
    has_sink = sink is not None
    cur_rows = lambda bi, i: (bi, i, 0)
    prev_rows = lambda bi, i: (bi, jnp.maximum(i * ratio - 1, 0), 0)
    cur_cols = lambda bi, i: (bi, 0, i)
    prev_cols = lambda bi, i: (bi, 0, jnp.maximum(i * ratio - 1, 0))
    in_specs = [
        pl.BlockSpec((None, tile // LANES, dq, LANES), lambda bi, i: (bi, i, 0, 0)),
        pl.BlockSpec((None, prev, KV_WIDTH), prev_rows),
        pl.BlockSpec((None, tile, KV_WIDTH), cur_rows),
        pl.BlockSpec((None, KV_WIDTH, prev), prev_cols),
        pl.BlockSpec((None, KV_WIDTH, tile), cur_cols),
        _const_spec(bias.shape),
    ]
    args = [q, k, k, vt, vt, bias]
    if has_sink:
        in_specs.append(_const_spec(sink.shape))
        args.append(sink)
    return pl.pallas_call(
        functools.partial(_band_attn_kernel, prev=prev, n_pairs=tile // PAIR, has_sink=has_sink,
                          lookahead=SHORT_BAND_LOOKAHEAD if band <= 2 * LANES else LONG_BAND_LOOKAHEAD),
        grid=(b, s // tile),
        in_specs=in_specs,
        out_specs=pl.BlockSpec((None, tile // LANES, dq, LANES), lambda bi, i: (bi, i, 0, 0)),
        out_shape=jax.ShapeDtypeStruct((b, s // LANES, dq, LANES), BF16),
        scratch_shapes=[
            pltpu.VMEM((prev + tile, KV_WIDTH), BF16),
            pltpu.VMEM(((prev + tile) // LANES, N_KV_HEADS, HEAD_DIM + ONES_ROWS, LANES), BF16),
            *[pltpu.VMEM((band, 2 * GROUP * CHUNK), F32) for _ in range(N_KV_HEADS)],
            pltpu.VMEM((N_KV_HEADS * HEAD_DIM, 2 * GROUP * CHUNK), F32),
        ],
        compiler_params=pltpu.CompilerParams(
            dimension_semantics=("arbitrary", "arbitrary"), vmem_limit_bytes=VMEM_LIMIT),
        name="band_attention",
    )(*args)


def _pair_bias(per_head):
    h, tq, tk = per_head.shape
    neg = jnp.full((h, tq, CHUNK), NEG, F32)
    both = jnp.stack([jnp.concatenate([per_head, neg], axis=-1),
                      jnp.concatenate([neg, per_head], axis=-1)], axis=1)
    both = both.reshape(N_KV_HEADS, GROUP, 2, tq, tk + CHUNK)
    return both.transpose(0, 4, 1, 2, 3).reshape(N_KV_HEADS, tk + CHUNK, 2 * GROUP * tq)


def _pair_sink(sink):
    s = (sink.astype(F32) * LOG2E).reshape(N_KV_HEADS, GROUP, 1, 1)
    return jnp.broadcast_to(s, (N_KV_HEADS, GROUP, 2, CHUNK)).reshape(N_KV_HEADS, 1, 2 * GROUP * CHUNK)


def _sample_attn_kernel(*refs, tq, has_sink):
    if has_sink:
        q_ref, kp_ref, kc_ref, vp_ref, vc_ref, bias_ref, sink_ref, o_ref = refs
    else:
        q_ref, kp_ref, kc_ref, vp_ref, vc_ref, bias_ref, o_ref = refs
        sink_ref = None
    kcat = jnp.concatenate([kp_ref[...], kc_ref[...]], axis=0).astype(BF16)
    vcat = jnp.concatenate([vp_ref[...], vc_ref[...]], axis=0).astype(BF16)
    lane_head = lax.broadcasted_iota(jnp.int32, (tq, KV_WIDTH), 1) // HEAD_DIM
    qc = q_ref[...]
    qs = jnp.concatenate(
        [jnp.where(lane_head == h, qc[:, g * KV_WIDTH:(g + 1) * KV_WIDTH], 0)
         for h in range(N_KV_HEADS) for g in range(GROUP)], axis=0).astype(BF16)
    s = lax.dot_general(qs, kcat, (((1,), (1,)), ((), ())), preferred_element_type=F32)
    s = s + bias_ref[...]
    m = jnp.max(s, axis=-1, keepdims=True)
    if has_sink:
        m = jnp.maximum(m, sink_ref[...])
    e = jnp.exp2(s - m)
    denom = jnp.sum(e, axis=-1, keepdims=True)
    if has_sink:
        denom = denom + jnp.exp2(sink_ref[...] - m)
    p = (e * (1.0 / denom)).astype(BF16)
    o_all = jnp.dot(p, vcat, preferred_element_type=F32)
    for g in range(GROUP):
        o = jnp.zeros((tq, KV_WIDTH), F32)
        for h in range(N_KV_HEADS):
            r0 = (h * GROUP + g) * tq
            o = jnp.where(lane_head == h, o_all[r0:r0 + tq, :], o)
        o_ref[:, g * KV_WIDTH:(g + 1) * KV_WIDTH] = o.astype(o_ref.dtype)


def _sample_attention(q, k_cache, k_new, v_cache, v_new, bias, sink):
    b, t, dq = q.shape
    prev = k_cache.shape[1]
    assert bias.shape == (N_HEADS * t, prev + t)
    has_sink = sink is not None
    row_map = lambda bi: (bi, 0, 0)
    in_specs = [
        pl.BlockSpec((None, t, dq), row_map),
        pl.BlockSpec((None, prev, KV_WIDTH), row_map),
        pl.BlockSpec((None, t, KV_WIDTH), row_map),
        pl.BlockSpec((None, prev, KV_WIDTH), row_map),
        pl.BlockSpec((None, t, KV_WIDTH), row_map),
        _const_spec(bias.shape),
    ]
    args = [q, k_cache, k_new, v_cache, v_new, bias]
    if has_sink:
        in_specs.append(_const_spec(sink.shape))
        args.append(sink)
    return pl.pallas_call(
        functools.partial(_sample_attn_kernel, tq=t, has_sink=has_sink),
        grid=(b,),
        in_specs=in_specs,
        out_specs=pl.BlockSpec((None, t, dq), row_map),
        out_shape=jax.ShapeDtypeStruct((b, t, dq), BF16),
        compiler_params=pltpu.CompilerParams(
            dimension_semantics=("arbitrary",), vmem_limit_bytes=VMEM_LIMIT),
        name="sample_attention",
    )(*args)


def _out_mlp_kernel(x_ref, at_ref, xs_ref, as_ref, wo_ref, g_ref, wup_ref, wdn_ref, o_ref, os_ref, *,
                    ff_chunks, n_main):
    def residual_mlp(x, proj):
        h = x + proj
        hn = (_rms_rows(h) * g_ref[...]).astype(BF16)
        step = wup_ref.shape[1] // ff_chunks
        acc = h
        for c in range(ff_chunks):
            u = jnp.dot(hn, wup_ref[:, c * step:(c + 1) * step], preferred_element_type=F32)
            u = jnp.square(jnp.maximum(u, 0.0)).astype(BF16)
            acc = acc + jnp.dot(u, wdn_ref[c * step:(c + 1) * step, :], preferred_element_type=F32)
        return acc

    @pl.when(pl.program_id(0) < n_main)
    def _():
        at = jnp.concatenate([at_ref[b] for b in range(at_ref.shape[0])], axis=1)
        proj = lax.dot_general(at, wo_ref[...], (((0,), (0,)), ((), ())), preferred_element_type=F32)
        o_ref[...] = residual_mlp(x_ref[...], proj)

    @pl.when(pl.program_id(0) == n_main)
    def _():
        os_ref[...] = residual_mlp(xs_ref[...], jnp.dot(as_ref[...], wo_ref[...], preferred_element_type=F32))


def _out_mlp(x, at, xs, a_s, wo, g, wup, wdn):
    n, d = x.shape
    ns = xs.shape[0]
    d_ff = wup.shape[1]
    tile = min(MLP_TILE, n)
    assert n % tile == 0
    n_main = n // tile
    main_rows = lambda i: (jnp.minimum(i, n_main - 1), 0)
    row_spec = pl.BlockSpec((tile, d), main_rows)
    sample_spec = pl.BlockSpec((ns, d), lambda i: (0, 0))
    return pl.pallas_call(
        functools.partial(_out_mlp_kernel, ff_chunks=MLP_FF_CHUNKS, n_main=n_main),
        grid=(n_main + 1,),
        in_specs=[
            row_spec,
            pl.BlockSpec((tile // LANES, d, LANES), lambda i: (jnp.minimum(i, n_main - 1), 0, 0)),
            sample_spec,
            sample_spec,
            _const_spec((d, d)),
            _const_spec((1, d)),
            _const_spec((d, d_ff)),
            _const_spec((d_ff, d)),
        ],
        out_specs=[row_spec, sample_spec],
        out_shape=[jax.ShapeDtypeStruct((n, d), F32), jax.ShapeDtypeStruct((ns, d), F32)],
        compiler_params=pltpu.CompilerParams(
            dimension_semantics=("arbitrary",), vmem_limit_bytes=VMEM_LIMIT),
        name="out_proj_mlp",
    )(x, at, xs, a_s, wo, g.reshape(1, d), wup, wdn)


def _rel_bias_kernel(tab_ref, o_ref, *, tk, delta):
    n_pad = tab_ref.shape[-1]
    r = lax.broadcasted_iota(jnp.int32, (n_pad, tk), 0)
    j = lax.broadcasted_iota(jnp.int32, (n_pad, tk), 1)
    for qi in range(o_ref.shape[0]):
        q = pl.program_id(1) * o_ref.shape[0] + qi
        idx = jnp.clip(q + delta - j, -MAX_REL, MAX_REL) + MAX_REL
        onehot = jnp.where(r == idx, 1.0, 0.0).astype(BF16)
        acc = jnp.dot(tab_ref[0], onehot, preferred_element_type=F32)
        acc = acc + jnp.dot(tab_ref[1], onehot, preferred_element_type=F32)
        acc = acc + jnp.dot(tab_ref[2], onehot, preferred_element_type=F32)
        o_ref[qi] = acc


def _rel_bias(table, tq, tk, delta):
    n_layers, h, n_rel = table.shape
    n_pad = -(-n_rel // LANES) * LANES
    t = jnp.pad(table, ((0, 0), (0, 0), (0, n_pad - n_rel)))
    hi = t.astype(BF16)
    mid = (t - hi.astype(F32)).astype(BF16)
    lo = (t - hi.astype(F32) - mid.astype(F32)).astype(BF16)
    parts = jnp.stack([hi, mid, lo], axis=1)
    return pl.pallas_call(
        functools.partial(_rel_bias_kernel, tk=tk, delta=delta),
        grid=(n_layers, tq // REL_BIAS_ROWS),
        in_specs=[pl.BlockSpec((None, 3, h, n_pad), lambda l, q: (l, 0, 0, 0))],
        out_specs=pl.BlockSpec((None, REL_BIAS_ROWS, h, tk), lambda l, q: (l, q, 0, 0)),
        out_shape=jax.ShapeDtypeStruct((n_layers, tq, h, tk), F32),
        compiler_params=pltpu.CompilerParams(dimension_semantics=("arbitrary", "arbitrary")),
        name="rel_bias_table",
    )(parts)


def _np_band_mask(q_pos, k_pos, left_chunks):
    qc = q_pos[:, None] // CHUNK
    kc = k_pos[None, :] // CHUNK
    return (k_pos[None, :] >= 0) & (kc <= qc) & (kc >= qc - left_chunks)


def _alibi_bias(q_pos, k_pos, static_mask):
    slopes = (2.0 ** (-8.0 * np.arange(1, N_HEADS + 1, dtype=np.float32) / N_HEADS)).astype(np.float32)
    dist = np.abs(q_pos[:, None] - k_pos[None, :]).astype(np.float32)
    bias = -slopes[:, None, None] * dist[None] * np.float32(LOG2E)
    if static_mask:
        bias = np.where(_np_band_mask(q_pos, k_pos, LEFT_CHUNKS_A)[None], bias, np.float32(NEG))
    return jnp.asarray(bias.astype(np.float32))


def _row_sink(sink, tq):
    return jnp.repeat(sink.astype(F32) * LOG2E, tq).reshape(N_HEADS * tq, 1)


def kernel(x_prompt, x_sample, cache_k_a, cache_v_a, cache_k_b, cache_v_b, g_attn, g_mlp, w_qkv_a,
           g_q_a, g_k_a, sink_a, w_o_a, g_kv, w_kv, g_k_b, w_q_b, g_q_b, rel_bias_b, w_o_b, w_up, w_down):
    batch, seq, d = x_prompt.shape
    dec_batch, t_new, _ = x_sample.shape
    n_layers_a = w_qkv_a.shape[0]
    n_layers_b = w_q_b.shape[0]
    len_a = cache_k_a.shape[2]
    len_b = cache_k_b.shape[1]
    keep_a = min(LEFT_CHUNKS_A * CHUNK, seq)
    keep_b = min(LEFT_CHUNKS_B * CHUNK, seq)
    dq = N_HEADS * HEAD_DIM
    q_scale = HEAD_DIM ** -0.5 * LOG2E

    hp = x_prompt.reshape(batch * seq, d)
    hs = x_sample.reshape(dec_batch * t_new, d)

    pos_s = PAST_LEN + np.arange(t_new)
    kpos_a = np.concatenate([PAST_LEN - len_a + np.arange(len_a), pos_s])
    kpos_b = np.concatenate([PAST_LEN - len_b + np.arange(len_b), pos_s])
    chunk_q = np.arange(CHUNK)
    bias_a_prompt = _pair_bias(_alibi_bias(chunk_q, np.arange(-LEFT_CHUNKS_A * CHUNK, CHUNK), static_mask=False))
    bias_a_sample = _alibi_bias(pos_s, kpos_a, static_mask=True).reshape(N_HEADS * t_new, -1)

    ones_kv = jnp.ones((KV_WIDTH,), F32)
    seg_a_p = [(0, dq, True), (dq, KV_WIDTH, True)]
    seg_a_s = seg_a_p + [(dq + KV_WIDTH, KV_WIDTH, False)]
    seg_kv = [(0, KV_WIDTH, True), (KV_WIDTH, KV_WIDTH, False)]
    last_rows = lambda h, keep: h.reshape(batch, seq, d)[:, seq - keep:].reshape(batch * keep, d)
    kv4 = lambda a, n, t: a.reshape(n, t, N_KV_HEADS, HEAD_DIM)

    ka_p, va_p, ka_s, va_s = [], [], [], []
    for i in range(n_layers_a):
        wq_gm = _group_major_cols(w_qkv_a[i][:, :dq])
        w = jnp.concatenate([wq_gm, w_qkv_a[i][:, dq:]], axis=1).astype(BF16)
        wqv_t = jnp.concatenate([wq_gm, w_qkv_a[i][:, dq + KV_WIDTH:]], axis=1).T.astype(BF16)
        gains = jnp.concatenate([jnp.tile(g_q_a[i] * q_scale, N_HEADS), jnp.tile(g_k_a[i], N_KV_HEADS), ones_kv])
        wo = _group_major_rows(w_o_a[i]).astype(BF16)
        wup = w_up[i].astype(BF16)
        wdn = w_down[i].astype(BF16)

        kp16, qtp, vtp = _project(hp, g_attn[i], w[:, dq:dq + KV_WIDTH], gains[dq:dq + KV_WIDTH], seg_kv[:1],
                                  (BF16,), wt=wqv_t, gains_t=gains[:dq], qt_rows=dq, seq=seq)
        ap = _prompt_attention(qtp.reshape(batch, seq // LANES, dq, LANES), kp16.reshape(batch, seq, KV_WIDTH),
                               vtp, bias_a_prompt, _pair_sink(sink_a[i]), LEFT_CHUNKS_A)
        kp, vp = _project(last_rows(hp, keep_a), g_attn[i], w[:, dq:], gains[dq:], seg_kv, (F32, F32))
        ka_p.append(kv4(kp, batch, keep_a))
        va_p.append(kv4(vp, batch, keep_a))

        qs, ks, vs = _project(hs, g_attn[i], w, gains, seg_a_s, (BF16, F32, F32))
        ks3 = ks.reshape(dec_batch, t_new, KV_WIDTH)
        vs3 = vs.reshape(dec_batch, t_new, KV_WIDTH)
        as_ = _sample_attention(qs.reshape(dec_batch, t_new, dq),
                                cache_k_a[i].reshape(dec_batch, len_a, KV_WIDTH), ks3,
                                cache_v_a[i].reshape(dec_batch, len_a, KV_WIDTH), vs3,
                                bias_a_sample, _row_sink(sink_a[i], t_new))
        ka_s.append(ks3.reshape(dec_batch, t_new, N_KV_HEADS, HEAD_DIM))
        va_s.append(vs3.reshape(dec_batch, t_new, N_KV_HEADS, HEAD_DIM))

        hp, hs = _out_mlp(hp, ap.reshape(batch * seq // LANES, dq, LANES), hs,
                          as_.reshape(dec_batch * t_new, dq), wo, g_mlp[i], wup, wdn)

    wkv = w_kv.astype(BF16)
    gains_kv = jnp.concatenate([jnp.tile(g_k_b, N_KV_HEADS), ones_kv])
    kb_p16, vtb_p = _project(hp, g_kv, wkv[:, :KV_WIDTH], gains_kv[:KV_WIDTH], seg_kv[:1], (BF16,),
                             wt=wkv[:, KV_WIDTH:].T, seq=seq)
    kb_p, vb_p = _project(last_rows(hp, keep_b), g_kv, wkv, gains_kv, seg_kv, (F32, F32))
    kb_s, vb_s = _project(hs, g_kv, wkv, gains_kv, seg_kv, (F32, F32))
    kb_p16 = kb_p16.reshape(batch, seq, KV_WIDTH)
    kb_s3 = kb_s.reshape(dec_batch, t_new, KV_WIDTH)
    vb_s3 = vb_s.reshape(dec_batch, t_new, KV_WIDTH)
    cache_kb = cache_k_b.reshape(dec_batch, len_b, KV_WIDTH)
    cache_vb = cache_v_b.reshape(dec_batch, len_b, KV_WIDTH)

    pad_b = LEFT_CHUNKS_B * CHUNK
    tk_b = pad_b + CHUNK
    assert int(pos_s[0] - kpos_b[0]) == pad_b and t_new <= CHUNK and len_b + t_new <= tk_b
    assert np.all(np.diff(kpos_b) == 1)
    rel = _rel_bias(rel_bias_b.astype(F32) * LOG2E, CHUNK, tk_b, pad_b)
    rel = rel.transpose(0, 2, 1, 3)
    mask_s = np.where(_np_band_mask(pos_s, kpos_b, LEFT_CHUNKS_B), 0.0, NEG).astype(np.float32)

    seg_q = [(0, dq, True)]
    for j in range(n_layers_b):
        layer = n_layers_a + j
        wq = _group_major_cols(w_q_b[j]).astype(BF16)
        gains = jnp.tile(g_q_b[j] * q_scale, N_HEADS)
        wo = _group_major_rows(w_o_b[j]).astype(BF16)
        wup = w_up[layer].astype(BF16)
        wdn = w_down[layer].astype(BF16)
        bias_p = _pair_bias(rel[j])
        bias_s = (rel[j][:, :t_new, :len_b + t_new] + mask_s[None]).reshape(N_HEADS * t_new, -1)

        (qtp,) = _project(hp, g_attn[layer], None, None, [], (), wt=wq.T, gains_t=gains, qt_rows=dq)
        ap = _prompt_attention(qtp.reshape(batch, seq // LANES, dq, LANES), kb_p16, vtb_p, bias_p, None,
                               LEFT_CHUNKS_B)
        (qs,) = _project(hs, g_attn[layer], wq, gains, seg_q, (BF16,))
        as_ = _sample_attention(qs.reshape(dec_batch, t_new, dq), cache_kb, kb_s3, cache_vb, vb_s3, bias_s, None)

        hp, hs = _out_mlp(hp, ap.reshape(batch * seq // LANES, dq, LANES), hs,
                          as_.reshape(dec_batch * t_new, dq), wo, g_mlp[layer], wup, wdn)

    return (hp.reshape(batch, seq, d), hs.reshape(dec_batch, t_new, d),
            jnp.stack(ka_p), jnp.stack(va_p),
            kv4(kb_p, batch, keep_b), kv4(vb_p, batch, keep_b),
            jnp.stack(ka_s), jnp.stack(va_s),
            kv4(kb_s3, dec_batch, t_new), kv4(vb_s3, dec_batch, t_new))
```
